```python
import math
import jax, jax.numpy as jnp
from jax import lax
import numpy as np

D_MODEL = 1024
BATCH = 8
SEQ = 2048
DEPTH = 2
DEC_BATCH = 16
DEC_SEQ = 64
PAST_LEN = 1024

CHUNK = 64
Q_BLOCK = 128
MLA_HEADS = 8
MLA_NOPE = 64
MLA_ROPE = 32
MLA_V = 64
MLA_Q_RANK = 384
MLA_KV_RANK = 256
RET_HEADS = 4
RET_DK = 64
RET_DV = 64
SB_HEADS = 4
SB_DK = 64
SB_DV = 64
MIX_WIDTH = MLA_HEADS * MLA_V + RET_HEADS * RET_DV + SB_HEADS * SB_DV
IN_WIDTHS = (MLA_Q_RANK, MLA_KV_RANK, MLA_ROPE,
             RET_HEADS * RET_DK, RET_HEADS * RET_DK, RET_HEADS * RET_DV, RET_HEADS * RET_DV,
             SB_HEADS * SB_DK, SB_HEADS * SB_DK, SB_HEADS * SB_DV)
IN_COLS = sum(IN_WIDTHS)
D_FF = 2816
CONV_W = 3
ROPE_BASE = 10000.0
EPS = 1e-6

kernel_name = "hybrid_streaming_encoder_step"


def rmsnorm(x, g):
    xf = x.astype(jnp.float32)
    y = xf * lax.rsqrt(jnp.mean(xf * xf, axis=-1, keepdims=True) + EPS)
    return (y * g.astype(jnp.float32)).astype(x.dtype)


def rope(x, pos):
    d = x.shape[-1]
    inv = ROPE_BASE ** (-jnp.arange(0, d, 2, dtype=jnp.float32) / d)
    ang = pos.astype(jnp.float32)[:, None] * inv[None, :]
    ang = ang.reshape((ang.shape[0],) + (1,) * (x.ndim - 3) + (d // 2,))
    cos, sin = jnp.cos(ang), jnp.sin(ang)
    x1 = x[..., : d // 2].astype(jnp.float32)
    x2 = x[..., d // 2:].astype(jnp.float32)
    return jnp.concatenate([x1 * cos - x2 * sin, x1 * sin + x2 * cos], axis=-1).astype(x.dtype)


def sweep_query_blocks(fn, q_args, q_pos):
    nq = q_pos.shape[0]
    if nq <= Q_BLOCK:
        return fn(q_args, q_pos)
    nb = nq // Q_BLOCK

    def one(i):
        s = i * Q_BLOCK
        qa = tuple(lax.dynamic_slice_in_dim(a, s, Q_BLOCK, axis=1) for a in q_args)
        return fn(qa, lax.dynamic_slice_in_dim(q_pos, s, Q_BLOCK))

    out = jnp.moveaxis(lax.map(one, jnp.arange(nb)), 0, 1)
    return out.reshape((out.shape[0], nq) + out.shape[3:])


def _retention_chunk(S, q, k, v):
    L = q.shape[1]
    lg = jnp.log(1.0 - 2.0 ** (-5.0 - jnp.arange(RET_HEADS, dtype=jnp.float32)))
    i = jnp.arange(L, dtype=jnp.float32)
    rel = i[:, None] - i[None, :]
    decay = jnp.where(rel >= 0, jnp.exp(lg[:, None, None] * jnp.maximum(rel, 0.0)), 0.0)
    qf, kf, vf = q.astype(jnp.float32), k.astype(jnp.float32), v.astype(jnp.float32)
    att = jnp.einsum("blhd,bmhd->bhlm", qf, kf) * decay[None]
    o = jnp.einsum("bhlm,bmhe->blhe", att, vf)
    q_decay = jnp.exp(lg[None, :] * (i[:, None] + 1.0))
    o = o + jnp.einsum("blhd,bhde->blhe", qf, S) * q_decay[None, :, :, None]
    k_decay = jnp.exp(lg[None, :] * (L - 1.0 - i[:, None]))
    S_new = jnp.exp(lg * L)[None, :, None, None] * S + jnp.einsum(
        "blhd,blhe->bhde", kf * k_decay[None, :, :, None], vf)
    return S_new, o


def _mixers(h, l, pos0, cache, p):
    B, L, _ = h.shape
    pos = pos0 + jnp.arange(L)
    z = h @ p["w_in"][l]
    splits = [int(s) for s in np.cumsum(IN_WIDTHS)[:-1]]
    zq, zkv, zkr, rq, rk, rv, rg, sq, sk, sv = jnp.split(z, splits, axis=-1)

    q = (rmsnorm(zq, p["g_q_norm"][l]) @ p["w_uq"][l]).reshape(B, L, MLA_HEADS, MLA_NOPE + MLA_ROPE)
    q_nope = q[..., :MLA_NOPE]
    q_rope = rope(q[..., MLA_NOPE:], pos)
    latent = rmsnorm(zkv, p["g_kv_norm"][l])
    k_rope = rope(zkr, pos)
    rq = rope(rq.reshape(B, L, RET_HEADS, RET_DK), pos) * (RET_DK ** -0.5)
    rk = rope(rk.reshape(B, L, RET_HEADS, RET_DK), pos)
    rv = rv.reshape(B, L, RET_HEADS, RET_DV)
    sq = sq.reshape(B, L, SB_HEADS, SB_DK)
    sk = sk.reshape(B, L, SB_HEADS, SB_DK)
    sv = sv.reshape(B, L, SB_HEADS, SB_DV)

    if cache is None:
        lat_all, kr_all, sk_all, sv_all = latent, k_rope, sk, sv
        S0 = jnp.zeros((B, RET_HEADS, RET_DK, RET_DV), jnp.float32)
    else:
        lat_all = jnp.concatenate([cache["lat"].astype(latent.dtype), latent], axis=1)
        kr_all = jnp.concatenate([cache["kr"].astype(k_rope.dtype), k_rope], axis=1)
        sk_all = jnp.concatenate([cache["sk"].astype(sk.dtype), sk], axis=1)
        sv_all = jnp.concatenate([cache["sv"].astype(sv.dtype), sv], axis=1)
        S0 = cache["S"].astype(jnp.float32)
    k_pos = jnp.arange(lat_all.shape[1])

    kv = (lat_all @ p["w_ukv"][l]).reshape(B, -1, MLA_HEADS, MLA_NOPE + MLA_V)
    k_nope, v_mla = kv[..., :MLA_NOPE], kv[..., MLA_NOPE:]
    mla_scale = (MLA_NOPE + MLA_ROPE) ** -0.5

    def mla_fn(qa, qp):
        qn, qr = qa
        s = (jnp.einsum("bqhd,bkhd->bhqk", qn, k_nope) + jnp.einsum("bqhd,bkd->bhqk", qr, kr_all)).astype(jnp.float32) * mla_scale
        mask = (k_pos[None, :] // CHUNK) <= (qp[:, None] // CHUNK)
        s = jnp.where(mask, s, -1e30)
        w = jax.nn.softmax(s, axis=-1).astype(v_mla.dtype)
        return jnp.einsum("bhqk,bkhd->bqhd", w, v_mla)

    mla_out = sweep_query_blocks(mla_fn, (q_nope, q_rope), pos).reshape(B, L, MLA_HEADS * MLA_V)

    if cache is None:
        n = L // CHUNK
        def to_chunks(a):
            return jnp.swapaxes(a.reshape((B, n, CHUNK) + a.shape[2:]), 0, 1)
        def step(S, inp):
            return _retention_chunk(S, *inp)
        S_new, ro = lax.scan(step, S0, (to_chunks(rq), to_chunks(rk), to_chunks(rv)))
        ro = jnp.swapaxes(ro, 0, 1).reshape(B, L, RET_HEADS, RET_DV)
    else:
        S_new, ro = _retention_chunk(S0, rq, rk, rv)
    mu = jnp.mean(ro, axis=-1, keepdims=True)
    var = jnp.mean((ro - mu) ** 2, axis=-1, keepdims=True)
    ro = ((ro - mu) * lax.rsqrt(var + EPS)).reshape(B, L, RET_HEADS * RET_DV) * p["g_ret_norm"][l].astype(jnp.float32)
    ret_out = (jax.nn.silu(rg.astype(jnp.float32)) * ro).astype(h.dtype)

    sb_scale = SB_DK ** -0.5

    def sb_fn(qa, qp):
        (qs,) = qa
        zz = jnp.einsum("bqhd,bkhd->bhqk", qs, sk_all).astype(jnp.float32) * sb_scale
        mask = k_pos[None, :] < qp[:, None]
        log_beta = jax.nn.log_sigmoid(zz)
        log_1mb = jnp.where(mask, log_beta - zz, 0.0)
        after = lax.cumsum(log_1mb, axis=3, reverse=True) - log_1mb
        w = jnp.where(mask, jnp.exp(log_beta + after), 0.0).astype(sv_all.dtype)
        return jnp.einsum("bhqk,bkhd->bqhd", w, sv_all)

    sb_out = sweep_query_blocks(sb_fn, (sq,), pos).reshape(B, L, SB_HEADS * SB_DV)

    mix = jnp.concatenate([mla_out, ret_out, sb_out], axis=-1) @ p["w_o"][l]
    return mix, (latent, k_rope, sk, sv, S_new.astype(h.dtype))


def _conv_ffn(h, prev, l, p):
    L = h.shape[1]
    up = h @ p["w_up"][l]
    a, b = up[..., :D_FF], up[..., D_FF:]
    a_ext = jnp.concatenate([prev.astype(a.dtype), a], axis=1)
    w = p["conv_w"][l]
    conv = p["conv_b"][l] + w[0] * a_ext[:, 0:L]
    for j in range(1, CONV_W):
        conv = conv + w[j] * a_ext[:, j:j + L]
    y = (jax.nn.silu(conv) * b) @ p["w_down"][l]
    return y, a_ext[:, -(CONV_W - 1):]


def _trunk(x, c, pos0, cache, p):
    new = [[] for _ in range(6)]
    B = x.shape[0]
    for l in range(DEPTH):
        mod = (jax.nn.silu(c) @ p["w_ada"][l] + p["b_ada"][l])[:, None, :]
        sh1, sc1, g1, sh2, sc2, g2 = jnp.split(mod, 6, axis=-1)
        lc = None if cache is None else {k: v[l] for k, v in cache.items()}
        h = rmsnorm(x, p["g_norm1"][l]) * (1.0 + sc1) + sh1
        mix, st = _mixers(h, l, pos0, lc, p)
        x = x + g1 * mix
        h = rmsnorm(x, p["g_norm2"][l]) * (1.0 + sc2) + sh2
        prev = jnp.zeros((B, CONV_W - 1, D_FF), x.dtype) if lc is None else lc["conv"]
        f, buf = _conv_ffn(h, prev, l, p)
        x = x + g2 * f
        for lst, s in zip(new, st + (buf,)):
            lst.append(s)
    y = rmsnorm(x, p["g_final"])
    return y, [jnp.stack(s, axis=0) for s in new]


def setup_inputs(seed: int = 0) -> dict:
    key = jax.random.key(seed)
    ks = jax.random.split(key, 32)
    nrm = jax.random.normal
    f32 = jnp.float32
    return {
        "x_prompt": nrm(ks[0], (BATCH, SEQ, D_MODEL), f32),
        "x_sample": nrm(ks[1], (DEC_BATCH, DEC_SEQ, D_MODEL), f32),
        "c_prompt": nrm(ks[2], (BATCH, D_MODEL), f32),
        "c_sample": nrm(ks[3], (DEC_BATCH, D_MODEL), f32),
        "cache_mla_latent": nrm(ks[4], (DEPTH, DEC_BATCH, PAST_LEN, MLA_KV_RANK), f32),
        "cache_mla_krope": nrm(ks[5], (DEPTH, DEC_BATCH, PAST_LEN, MLA_ROPE), f32),
        "cache_sb_k": nrm(ks[6], (DEPTH, DEC_BATCH, PAST_LEN, SB_HEADS, SB_DK), f32),
        "cache_sb_v": nrm(ks[7], (DEPTH, DEC_BATCH, PAST_LEN, SB_HEADS, SB_DV), f32),
        "state_ret": 2.0 * nrm(ks[8], (DEPTH, DEC_BATCH, RET_HEADS, RET_DK, RET_DV), f32),
        "state_ffn_conv": nrm(ks[9], (DEPTH, DEC_BATCH, CONV_W - 1, D_FF), f32),
        "w_in": nrm(ks[10], (DEPTH, D_MODEL, IN_COLS), f32) * D_MODEL ** -0.5,
        "g_q_norm": 1.0 + 0.02 * nrm(ks[11], (DEPTH, MLA_Q_RANK), f32),
        "w_uq": nrm(ks[12], (DEPTH, MLA_Q_RANK, MLA_HEADS * (MLA_NOPE + MLA_ROPE)), f32) * MLA_Q_RANK ** -0.5,
        "g_kv_norm": 1.0 + 0.02 * nrm(ks[13], (DEPTH, MLA_KV_RANK), f32),
        "w_ukv": nrm(ks[14], (DEPTH, MLA_KV_RANK, MLA_HEADS * (MLA_NOPE + MLA_V)), f32) * MLA_KV_RANK ** -0.5,
        "g_ret_norm": 1.0 + 0.02 * nrm(ks[15], (DEPTH, RET_HEADS * RET_DV), f32),
        "w_o": nrm(ks[16], (DEPTH, MIX_WIDTH, D_MODEL), f32) * MIX_WIDTH ** -0.5,
        "w_up": nrm(ks[17], (DEPTH, D_MODEL, 2 * D_FF), f32) * D_MODEL ** -0.5,
        "conv_w": nrm(ks[18], (DEPTH, CONV_W, D_FF), f32) * CONV_W ** -0.5,
        "conv_b": 0.02 * nrm(ks[19], (DEPTH, D_FF), f32),
        "w_down": nrm(ks[20], (DEPTH, D_FF, D_MODEL), f32) * D_FF ** -0.5,
        "g_norm1": 1.0 + 0.02 * nrm(ks[21], (DEPTH, D_MODEL), f32),
        "g_norm2": 1.0 + 0.02 * nrm(ks[22], (DEPTH, D_MODEL), f32),
        "w_ada": nrm(ks[23], (DEPTH, D_MODEL, 6 * D_MODEL), f32) * (0.5 * D_MODEL ** -0.5),
        "b_ada": 0.02 * nrm(ks[24], (DEPTH, 6 * D_MODEL), f32),
        "g_final": 1.0 + 0.02 * nrm(ks[25], (D_MODEL,), f32),
    }


def reference(x_prompt, x_sample, c_prompt, c_sample, cache_mla_latent, cache_mla_krope, cache_sb_k, cache_sb_v,
              state_ret, state_ffn_conv, w_in, g_q_norm, w_uq, g_kv_norm, w_ukv, g_ret_norm, w_o, w_up, conv_w,
              conv_b, w_down, g_norm1, g_norm2, w_ada, b_ada, g_final):
    p = dict(w_in=w_in, g_q_norm=g_q_norm, w_uq=w_uq, g_kv_norm=g_kv_norm, w_ukv=w_ukv, g_ret_norm=g_ret_norm,
             w_o=w_o, w_up=w_up, conv_w=conv_w, conv_b=conv_b, w_down=w_down, g_norm1=g_norm1, g_norm2=g_norm2,
             w_ada=w_ada, b_ada=b_ada, g_final=g_final)
    y_prompt, (p_lat, p_kr, p_sk, p_sv, p_S, p_conv) = _trunk(x_prompt, c_prompt, 0, None, p)
    cache = dict(lat=cache_mla_latent, kr=cache_mla_krope, sk=cache_sb_k, sv=cache_sb_v, S=state_ret,
                 conv=state_ffn_conv)
    y_sample, (s_lat, s_kr, s_sk, s_sv, s_S, s_conv) = _trunk(x_sample, c_sample, PAST_LEN, cache, p)
    return (y_prompt, y_sample, p_lat, p_kr, p_sk, p_sv, p_S, p_conv, s_lat, s_kr, s_sk, s_sv, s_S, s_conv)
```

```python
import functools

import numpy as np
import jax
import jax.numpy as jnp
from jax import lax
from jax.experimental import pallas as pl
from jax.experimental.pallas import tpu as pltpu

D_MODEL = 1024
CHUNK = 64
MLA_HEADS = 8
MLA_NOPE = 64
MLA_ROPE = 32
MLA_V = 64
MLA_Q_RANK = 384
MLA_KV_RANK = 256
RET_HEADS = 4
RET_DK = 64
RET_DV = 64
SB_HEADS = 4
SB_DK = 64
SB_DV = 64
D_FF = 2816
CONV_W = 3
ROPE_BASE = 10000.0
EPS = 1e-6

HEAD_W = 256
ROW_TILE = 512
ATT_TILE = 256
RET_CHUNK = 256
FF_CHUNK = 1408
VMEM_LIMIT = 56 * 1024 * 1024

F32 = jnp.float32
BF16 = jnp.bfloat16

_ZQ, _ZKV, _RQ, _RK, _RV, _RG, _SQ, _SK, _SV, _KRT, _IN_COLS_P = (
    0, 384, 640, 896, 1152, 1408, 1664, 1920, 2176, 2432, 2688)


def _dot(a, b):
    return jnp.dot(a, b, preferred_element_type=F32)


def _dot_nt(a, b):
    return lax.dot_general(a, b, (((1,), (1,)), ((), ())), preferred_element_type=F32)


def _dot_tn(a, b):
    return lax.dot_general(a, b, (((0,), (0,)), ((), ())), preferred_element_type=F32)


def _rms(x):
    return x * lax.rsqrt(jnp.mean(x * x, axis=-1, keepdims=True) + EPS)


def _silu(x):
    return x / (1.0 + jnp.exp(-x))


def _params(*sem):
    return pltpu.CompilerParams(dimension_semantics=sem, vmem_limit_bytes=VMEM_LIMIT)


def _const_spec(shape):
    nd = len(shape)
    return pl.BlockSpec(shape, lambda *_: (0,) * nd)


def _ada_kernel(c_ref, w_ref, b_ref, o_ref):
    a = _silu(c_ref[...]).astype(BF16)
    o_ref[0] = _dot(a, w_ref[0].astype(BF16)) + b_ref[0]


def _ada(c_all, w_ada, b_ada):
    depth, d, n = w_ada.shape
    rows = c_all.shape[0]
    tn = 1536
    return pl.pallas_call(
        _ada_kernel,
        out_shape=jax.ShapeDtypeStruct((depth, rows, n), F32),
        grid=(depth, n // tn),
        in_specs=[pl.BlockSpec((rows, d), lambda l, j: (0, 0)),
                  pl.BlockSpec((1, d, tn), lambda l, j: (l, 0, j)),
                  pl.BlockSpec((1, 1, tn), lambda l, j: (l, 0, j))],
        out_specs=pl.BlockSpec((1, rows, tn), lambda l, j: (l, 0, j)),
        compiler_params=_params("parallel", "parallel"),
        name="ada_mod",
    )(c_all, w_ada, b_ada.reshape(depth, 1, n))


def _inproj_kernel(x_ref, mod_ref, g1_ref, win_ref, gq_ref, wuq_ref, gkv_ref, cm_ref, sm_ref, cr_ref, sr_ref,
                   qn_ref, qr_ref, lat_ref, krt_ref, kr_ref, rq_ref, rk_ref, rv_ref, rg_ref, sq_ref, sk_ref,
                   sv_ref):
    bb, tl, d = x_ref.shape
    rows = bb * tl
    mod = mod_ref[...]
    h = _rms(x_ref[...]) * g1_ref[...] * (1.0 + mod[:, 1:2, :]) + mod[:, 0:1, :]
    z = _dot(h.reshape(rows, d).astype(BF16), win_ref[...])

    def put(ref, val):
        ref[...] = val.reshape(bb, tl, val.shape[-1]).astype(ref.dtype)

    def rope(x1, x2, c_ref, s_ref):
        c, s = c_ref[...][None], s_ref[...][None]
        x1 = x1.reshape(bb, tl, x1.shape[-1])
        x2 = x2.reshape(bb, tl, x2.shape[-1])
        return x1 * c - x2 * s, x1 * s + x2 * c

    q = _dot((_rms(z[:, _ZQ:_ZKV]) * gq_ref[...]).astype(BF16), wuq_ref[...])
    put(qn_ref, q[:, :512])
    q1, q2 = rope(q[:, 512:640], q[:, 640:768], cm_ref, sm_ref)
    qr_ref[:, :, 0:128] = q1.astype(qr_ref.dtype)
    qr_ref[:, :, 128:256] = q2.astype(qr_ref.dtype)
    put(lat_ref, _rms(z[:, _ZKV:_RQ]) * gkv_ref[...])
    k1, k2 = rope(z[:, _KRT:_KRT + 128], z[:, _KRT + 128:_KRT + 256], cm_ref, sm_ref)
    krt_ref[:, :, 0:128] = k1.astype(krt_ref.dtype)
    krt_ref[:, :, 128:256] = k2.astype(krt_ref.dtype)
    lane = lax.broadcasted_iota(jnp.int32, (1, 1, MLA_ROPE), 2)
    kr_ref[...] = jnp.where(lane < MLA_ROPE // 2, k1[:, :, :MLA_ROPE], k2[:, :, :MLA_ROPE])
    a1, a2 = rope(z[:, _RQ:_RQ + 128], z[:, _RQ + 128:_RK], cr_ref, sr_ref)
    rq_ref[:, :, 0:128] = (a1 * (RET_DK ** -0.5)).astype(rq_ref.dtype)
    rq_ref[:, :, 128:256] = (a2 * (RET_DK ** -0.5)).astype(rq_ref.dtype)
    b1, b2 = rope(z[:, _RK:_RK + 128], z[:, _RK + 128:_RV], cr_ref, sr_ref)
    rk_ref[:, :, 0:128] = b1
    rk_ref[:, :, 128:256] = b2
    put(rv_ref, z[:, _RV:_RG])
    put(rg_ref, z[:, _RG:_SQ])
    put(sq_ref, z[:, _SQ:_SK] * (SB_DK ** -0.5))
    put(sk_ref, z[:, _SK:_SV])
    put(sv_ref, z[:, _SV:_KRT])


def _inproj(x, mod, g1, win, gq, wuq, gkv, cm, sm, cr, sr):
    b, l, d = x.shape
    tl = min(l, ROW_TILE)
    bb = ROW_TILE // tl
    grid = (b // bb, l // tl)
    tok = lambda w: pl.BlockSpec((bb, tl, w), lambda i, t: (i, t, 0))
    pos = pl.BlockSpec((tl, 128), lambda i, t: (t, 0))
    widths = [(512, BF16), (256, BF16), (256, F32), (256, BF16), (MLA_ROPE, F32), (256, BF16), (256, F32),
              (256, BF16), (256, F32), (256, BF16), (256, F32), (256, F32)]
    return pl.pallas_call(
        _inproj_kernel,
        out_shape=[jax.ShapeDtypeStruct((b, l, w), dt) for w, dt in widths],
        grid=grid,
        in_specs=[tok(d), pl.BlockSpec((bb, 6, d), lambda i, t: (i, 0, 0)), _const_spec(g1.shape),
                  _const_spec(win.shape), _const_spec(gq.shape), _const_spec(wuq.shape), _const_spec(gkv.shape),
                  pos, pos, pos, pos],
        out_specs=[tok(w) for w, _ in widths],
        compiler_params=_params("parallel", "parallel"),
        name="in_proj",
    )(x, mod, g1, win, gq, wuq, gkv, cm, sm, cr, sr)


def _kvup_kernel(lat_ref, w_ref, kn_ref, v_ref):
    kv = _dot(lat_ref[0].astype(BF16), w_ref[...])
    kn_ref[0] = kv[:, :512].astype(kn_ref.dtype)
    v_ref[0] = kv[:, 512:].astype(v_ref.dtype)


def _kvup(lat_all, wukv):
    b, lk, r = lat_all.shape
    blk = lambda w: pl.BlockSpec((1, lk, w), lambda i: (i, 0, 0))
    return pl.pallas_call(
        _kvup_kernel,
        out_shape=[jax.ShapeDtypeStruct((b, lk, 512), BF16)] * 2,
        grid=(b,),
        in_specs=[blk(r), _const_spec(wukv.shape)],
        out_specs=[blk(512), blk(512)],
        compiler_params=_params("parallel"),
        name="kv_up",
    )(lat_all, wukv)


def _mla_kernel(qn_ref, qr_ref, kn_ref, v_ref, kr_ref, o_ref, *, tq, tk, off, scale):
    qi = pl.program_id(1)
    dstart = pl.multiple_of(off + qi * tq, CHUNK)
    nfull = (off + qi * tq) // tk
    lane = lax.broadcasted_iota(jnp.int32, (1, HEAD_W), 1)
    row_c = lax.broadcasted_iota(jnp.int32, (tq, 1), 0) // CHUNK
    col_c = lax.broadcasted_iota(jnp.int32, (1, tq), 1) // CHUNK
    dmask = col_c <= row_c
    qr = qr_ref[0]
    outs = []
    for g in range(MLA_HEADS // 4):
        gs = slice(HEAD_W * g, HEAD_W * (g + 1))
        qn = qn_ref[0, :, gs]
        out_g = jnp.zeros((tq, HEAD_W), F32)
        for hh in range(4):
            h = 4 * g + hh
            mq = (lane >= 64 * hh) & (lane < 64 * hh + 64)
            mr = ((lane & 127) >> 4) == h
            qh = jnp.where(mq, qn, jnp.zeros_like(qn))
            qrh = jnp.where(mr, qr, jnp.zeros_like(qr))

            def scores(ks):
                return (_dot_nt(qh, kn_ref[0, ks, gs]) + _dot_nt(qrh, kr_ref[0, ks, :])) * scale

            ds = pl.ds(dstart, tq)
            s = jnp.where(dmask, scores(ds), -1e30)
            m = jnp.max(s, axis=-1, keepdims=True)
            p = jnp.exp(s - m)
            l = jnp.sum(p, axis=-1, keepdims=True)
            acc = _dot(p.astype(BF16), v_ref[0, ds, gs])

            def body(j, carry):
                m, l, acc = carry
                ks = pl.ds(pl.multiple_of(j * tk, tk), tk)
                s = scores(ks)
                m_new = jnp.maximum(m, jnp.max(s, axis=-1, keepdims=True))
                alpha = jnp.exp(m - m_new)
                p = jnp.exp(s - m_new)
                l = alpha * l + jnp.sum(p, axis=-1, keepdims=True)
                acc = alpha * acc + _dot(p.astype(BF16), v_ref[0, ks, gs])
                return m_new, l, acc

            m, l, acc = lax.fori_loop(0, nfull, body, (m, l, acc))
            out_g = jnp.where(mq, acc / l, out_g)
        outs.append(out_g)
    o_ref[0] = jnp.concatenate(outs, axis=-1).astype(o_ref.dtype)


def _mla(qn, qr, kn, v, krt):
    b, l, _ = qn.shape
    lk = kn.shape[1]
    tq = min(l, ATT_TILE)
    qblk = lambda w: pl.BlockSpec((1, tq, w), lambda i, t: (i, t, 0))
    kblk = lambda w: pl.BlockSpec((1, lk, w), lambda i, t: (i, 0, 0))
    kern = functools.partial(_mla_kernel, tq=tq, tk=ATT_TILE, off=lk - l,
                             scale=(MLA_NOPE + MLA_ROPE) ** -0.5)
    return pl.pallas_call(
        kern,
        out_shape=jax.ShapeDtypeStruct((b, l, 512), BF16),
        grid=(b, l // tq),
        in_specs=[qblk(512), qblk(256), kblk(512), kblk(512), kblk(256)],
        out_specs=qblk(512),
        compiler_params=_params("parallel", "parallel"),
        name="mla_attn",
    )(qn, qr, kn, v, krt)


def _suffix_matrix(n):
    j = lax.broadcasted_iota(jnp.int32, (n, n), 0)
    s = lax.broadcasted_iota(jnp.int32, (n, n), 1)
    return jnp.where(j > s, 1.0, 0.0).astype(BF16)


def _sb_kernel(q_ref, k_ref, v_ref, o_ref, kb_ref, vb_ref, *, tq, tk, off):
    qi = pl.program_id(1)

    @pl.when(qi == 0)
    def _():
        kb_ref[...] = k_ref[0].astype(BF16)
        vb_ref[...] = v_ref[0].astype(BF16)

    dstart = pl.multiple_of(off + qi * tq, CHUNK)
    nfull = (off + qi * tq) // tk
    lane = lax.broadcasted_iota(jnp.int32, (1, HEAD_W), 1)
    dmask = lax.broadcasted_iota(jnp.int32, (1, tq), 1) < lax.broadcasted_iota(jnp.int32, (tq, 1), 0)
    t_diag = _suffix_matrix(tq)
    t_full = t_diag if tk == tq else _suffix_matrix(tk)
    q = q_ref[0]

    def tile(qh, ks, run, mask, tmat):
        zz = _dot_nt(qh, kb_ref[ks, :])
        t = jnp.log1p(jnp.exp(-jnp.abs(zz)))
        log_1mb = -jnp.maximum(zz, 0.0) - t
        log_beta = jnp.minimum(zz, 0.0) - t
        if mask is not None:
            log_1mb = jnp.where(mask, log_1mb, 0.0)
        hi = log_1mb.astype(BF16)
        lo = (log_1mb - hi.astype(F32)).astype(BF16)
        after = _dot(hi, tmat) + _dot(lo, tmat)
        w = jnp.exp(log_beta + after + run)
        if mask is not None:
            w = jnp.where(mask, w, 0.0)
        pv = _dot(w.astype(BF16), vb_ref[ks, :])
        return pv, run + jnp.sum(log_1mb, axis=-1, keepdims=True)

    out = jnp.zeros((tq, HEAD_W), F32)
    for h in range(SB_HEADS):
        mq = (lane >= 64 * h) & (lane < 64 * h + 64)
        qh = jnp.where(mq, q, jnp.zeros_like(q))
        acc, run = tile(qh, pl.ds(dstart, tq), jnp.zeros((tq, 1), F32), dmask, t_diag)

        def body(i, carry):
            acc, run = carry
            j = nfull - 1 - i
            pv, run = tile(qh, pl.ds(pl.multiple_of(j * tk, tk), tk), run, None, t_full)
            return acc + pv, run

        acc, run = lax.fori_loop(0, nfull, body, (acc, run))
        out = jnp.where(mq, acc, out)
    o_ref[0] = out.astype(o_ref.dtype)


def _sb(sq, sk_all, sv_all):
    b, l, _ = sq.shape
    lk = sk_all.shape[1]
    tq = min(l, ATT_TILE)
    qblk = pl.BlockSpec((1, tq, HEAD_W), lambda i, t: (i, t, 0))
    kblk = pl.BlockSpec((1, lk, HEAD_W), lambda i, t: (i, 0, 0))
    kern = functools.partial(_sb_kernel, tq=tq, tk=ATT_TILE, off=lk - l)
    return pl.pallas_call(
        kern,
        out_shape=jax.ShapeDtypeStruct((b, l, HEAD_W), BF16),
        grid=(b, l // tq),
        in_specs=[qblk, kblk, kblk],
        out_specs=qblk,
        scratch_shapes=[pltpu.VMEM((lk, HEAD_W), BF16), pltpu.VMEM((lk, HEAD_W), BF16)],
        compiler_params=_params("parallel", "arbitrary"),
        name="sb_attn",
    )(sq, sk_all, sv_all)


def _ret_kernel(q_ref, k_ref, v_ref, g_ref, s0_ref, gn_ref, dec_ref, qd_ref, kd_ref, sd_ref,
                o_ref, s_ref, *, c):
    n = q_ref.shape[1] // c
    lane = lax.broadcasted_iota(jnp.int32, (1, HEAD_W), 1)
    krow = lax.broadcasted_iota(jnp.int32, (HEAD_W, 1), 0)
    bd_mask = ((krow & 127) >> 5) == (lane >> 6)
    s_ref[0] = s0_ref[0]

    def step(i, carry):
        rs = pl.ds(pl.multiple_of(i * c, c), c)
        q = q_ref[0, rs, :]
        kf = k_ref[0, rs, :]
        k = kf.astype(BF16)
        v = v_ref[0, rs, :]
        state = s_ref[0]
        o = _dot(q, state.astype(BF16)) * qd_ref[...]
        for h in range(RET_HEADS):
            mk = ((lane & 127) >> 5) == h
            mv = (lane >> 6) == h
            att = _dot_nt(jnp.where(mk, q, jnp.zeros_like(q)), k) * dec_ref[h]
            o = o + jnp.where(mv, _dot(att.astype(BF16), v), 0.0)
        upd = _dot_tn((kf * kd_ref[...]).astype(BF16), v)
        s_ref[0] = sd_ref[...] * state + jnp.where(bd_mask, upd, 0.0)
        mu = jnp.zeros_like(o)
        for h in range(RET_HEADS):
            mv = (lane >> 6) == h
            mu = mu + jnp.where(mv, jnp.sum(jnp.where(mv, o, 0.0), axis=-1, keepdims=True), 0.0)
        dlt = o - mu * (1.0 / RET_DV)
        var = jnp.zeros_like(o)
        for h in range(RET_HEADS):
            mv = (lane >> 6) == h
            var = var + jnp.where(mv, jnp.sum(jnp.where(mv, dlt * dlt, 0.0), axis=-1, keepdims=True), 0.0)
        ro = dlt * lax.rsqrt(var * (1.0 / RET_DV) + EPS) * gn_ref[...]
        o_ref[0, rs, :] = (_silu(g_ref[0, rs, :]) * ro).astype(o_ref.dtype)
        return carry

    lax.fori_loop(0, n, step, 0)


def _ret_consts(c):
    lg = jnp.log(1.0 - 2.0 ** (-5.0 - jnp.arange(RET_HEADS, dtype=F32)))
    i = jnp.arange(c, dtype=F32)
    rel = i[:, None] - i[None, :]
    dec = jnp.where(rel >= 0, jnp.exp(lg[:, None, None] * jnp.maximum(rel, 0.0)), 0.0)
    v_head = jnp.arange(HEAD_W) // RET_DV
    k_head = (jnp.arange(HEAD_W) % 128) // (RET_DK // 2)
    qd = jnp.exp(lg[None, v_head] * (i[:, None] + 1.0))
    kd = jnp.exp(lg[None, k_head] * (c - 1.0 - i[:, None]))
    sd = jnp.exp(lg * c)[v_head][None, :]
    return dec, qd, kd, sd


def _ret(rq, rk, rv, rg, s0, gn):
    b, l, _ = rq.shape
    c = min(l, RET_CHUNK)
    dec, qd, kd, sd = _ret_consts(c)
    blk = pl.BlockSpec((1, l, HEAD_W), lambda i: (i, 0, 0))
    sblk = pl.BlockSpec((1, HEAD_W, HEAD_W), lambda i: (i, 0, 0))
    return pl.pallas_call(
        functools.partial(_ret_kernel, c=c),
        out_shape=[jax.ShapeDtypeStruct((b, l, HEAD_W), BF16), jax.ShapeDtypeStruct((b, HEAD_W, HEAD_W), F32)],
        grid=(b,),
        in_specs=[blk, blk, blk, blk, sblk, _const_spec(gn.shape), _const_spec(dec.shape), _const_spec(qd.shape),
                  _const_spec(kd.shape), _const_spec(sd.shape)],
        out_specs=[blk, sblk],
        compiler_params=_params("parallel"),
        name="retention",
    )(rq, rk, rv, rg, s0, gn, dec, qd, kd, sd)


def _oproj_kernel(mla_ref, ret_ref, sb_ref, x_ref, mod_ref, wo_ref, g2_ref, x1_ref, h2_ref):
    bb, tl, d = x_ref.shape
    rows = bb * tl
    cat = jnp.concatenate([mla_ref[...].reshape(rows, 512), ret_ref[...].reshape(rows, HEAD_W),
                           sb_ref[...].reshape(rows, HEAD_W)], axis=-1)
    mix = _dot(cat, wo_ref[...]).reshape(bb, tl, d)
    mod = mod_ref[...]
    x1 = x_ref[...] + mod[:, 2:3, :] * mix
    x1_ref[...] = x1
    h2_ref[...] = (_rms(x1) * g2_ref[...] * (1.0 + mod[:, 4:5, :]) + mod[:, 3:4, :]).astype(h2_ref.dtype)


def _oproj(mla, ret, sb, x, mod, wo, g2):
    b, l, d = x.shape
    tl = min(l, ROW_TILE)
    bb = ROW_TILE // tl
    tok = lambda w: pl.BlockSpec((bb, tl, w), lambda i, t: (i, t, 0))
    return pl.pallas_call(
        _oproj_kernel,
        out_shape=[jax.ShapeDtypeStruct((b, l, d), F32), jax.ShapeDtypeStruct((b, l, d), BF16)],
        grid=(b // bb, l // tl),
        in_specs=[tok(512), tok(HEAD_W), tok(HEAD_W), tok(d), pl.BlockSpec((bb, 6, d), lambda i, t: (i, 0, 0)),
                  _const_spec(wo.shape), _const_spec(g2.shape)],
        out_specs=[tok(d), tok(d)],
        compiler_params=_params("parallel", "parallel"),
        name="out_proj",
    )(mla, ret, sb, x, mod, wo, g2)


def _ffn_kernel(h_ref, x_ref, mod_ref, prev_ref, wup_ref, cw_ref, cb_ref, wdn_ref, gf_ref,
                o_ref, st_ref, a_scr, carry, *, final_norm):
    bb, tl, d = x_ref.shape
    rows = bb * tl
    t = pl.program_id(1)
    nt = pl.num_programs(1)

    @pl.when(t == 0)
    def _():
        carry[:, 6:8, :] = prev_ref[...]

    h = h_ref[...].reshape(rows, d)
    f = jnp.zeros((rows, d), F32)
    for c0 in range(0, D_FF, FF_CHUNK):
        cs = slice(c0, c0 + FF_CHUNK)
        a = _dot(h, wup_ref[:, cs]).reshape(bb, tl, FF_CHUNK)
        b = _dot(h, wup_ref[:, D_FF + c0:D_FF + c0 + FF_CHUNK])
        a_scr[:, 8:, :] = a
        a_scr[:, 6:8, :] = carry[:, 6:8, cs]
        cw = cw_ref[...]
        conv = (cb_ref[:, cs] + cw[0:1, cs] * a_scr[:, 6:6 + tl, :] + cw[1:2, cs] * a_scr[:, 7:7 + tl, :]
                + cw[2:3, cs] * a)
        carry[:, 6:8, cs] = a_scr[:, tl + 6:tl + 8, :]
        y = (_silu(conv).reshape(rows, FF_CHUNK) * b).astype(BF16)
        f = f + _dot(y, wdn_ref[cs, :])
    x2 = x_ref[...] + mod_ref[...][:, 5:6, :] * f.reshape(bb, tl, d)
    if final_norm:
        x2 = _rms(x2) * gf_ref[...]
    o_ref[...] = x2

    @pl.when(t == nt - 1)
    def _():
        st_ref[...] = carry[:, 6:8, :]


def _ffn(h2, x1, mod, prev, wup, cw, cb, wdn, gf, final_norm):
    b, l, d = x1.shape
    tl = min(l, ROW_TILE)
    bb = ROW_TILE // tl
    tok = lambda: pl.BlockSpec((bb, tl, d), lambda i, t: (i, t, 0))
    st = pl.BlockSpec((bb, CONV_W - 1, D_FF), lambda i, t: (i, 0, 0))
    once = dict(pipeline_mode=pl.Buffered(1))
    return pl.pallas_call(
        functools.partial(_ffn_kernel, final_norm=final_norm),
        out_shape=[jax.ShapeDtypeStruct((b, l, d), F32), jax.ShapeDtypeStruct((b, CONV_W - 1, D_FF), F32)],
        grid=(b // bb, l // tl),
        in_specs=[tok(), tok(), pl.BlockSpec((bb, 6, d), lambda i, t: (i, 0, 0)), st,
                  pl.BlockSpec(wup.shape, lambda i, t: (0, 0), **once), _const_spec(cw.shape),
                  _const_spec(cb.shape), pl.BlockSpec(wdn.shape, lambda i, t: (0, 0), **once),
                  _const_spec(gf.shape)],
        out_specs=[tok(), st],
        scratch_shapes=[pltpu.VMEM((bb, tl + 8, FF_CHUNK), F32), pltpu.VMEM((bb, 8, D_FF), F32)],
        compiler_params=_params("parallel", "arbitrary"),
        name="conv_ffn",
    )(h2, x1, mod, prev, wup, cw, cb, wdn, gf)


def _half_split(base, heads, dim):
    h = np.arange(heads)[:, None]
    i = np.arange(dim // 2)[None, :]
    first = (base + dim * h + i).reshape(-1)
    return np.concatenate([first, first + dim // 2])


def _win_columns():
    nat = lambda a, b: np.arange(a, b)
    kr = 640
    kr1 = np.tile(np.arange(kr, kr + 16), MLA_HEADS)
    return np.concatenate([nat(0, 640), _half_split(672, RET_HEADS, RET_DK), _half_split(928, RET_HEADS, RET_DK),
                           nat(1184, 2464), kr1, kr1 + 16])


def _wuq_columns():
    hd = MLA_NOPE + MLA_ROPE
    h = np.arange(MLA_HEADS)[:, None]
    nope = (hd * h + np.arange(MLA_NOPE)[None, :]).reshape(-1)
    r1 = (hd * h + MLA_NOPE + np.arange(MLA_ROPE // 2)[None, :]).reshape(-1)
    return np.concatenate([nope, r1, r1 + MLA_ROPE // 2])


def _wukv_columns():
    hd = MLA_NOPE + MLA_V
    h = np.arange(MLA_HEADS)[:, None]
    kn = (hd * h + np.arange(MLA_NOPE)[None, :]).reshape(-1)
    return np.concatenate([kn, kn + MLA_NOPE])


def _ret_state_rows():
    h = np.arange(RET_HEADS)[:, None]
    dd = np.arange(RET_DK)[None, :]
    return (dd // 32) * 128 + 32 * h + dd % 32


def _rope_tables(pos, dim, reps):
    inv = ROPE_BASE ** (-jnp.arange(0, dim, 2, dtype=F32) / dim)
    ang = pos.astype(F32)[:, None] * inv[None, :]
    return jnp.tile(jnp.cos(ang), (1, reps)), jnp.tile(jnp.sin(ang), (1, reps))


def _tile_krope(kr):
    half = MLA_ROPE // 2
    return jnp.concatenate([jnp.tile(kr[..., :half], (1, 1, MLA_HEADS)), jnp.tile(kr[..., half:], (1, 1, MLA_HEADS))],
                           axis=-1)


def _trunk(x, mods, pos0, cache, w):
    b, l, _ = x.shape
    depth = len(mods)
    pos = pos0 + jnp.arange(l)
    cm, sm = _rope_tables(pos, MLA_ROPE, MLA_HEADS)
    cr, sr = _rope_tables(pos, RET_DK, RET_HEADS)
    rows = _ret_state_rows()
    new = [[] for _ in range(6)]
    for layer in range(depth):
        mod = mods[layer]
        (qn, qr, lat, krt, kr, rq, rk, rv, rg, sq, sk, sv) = _inproj(
            x, mod, w["g_norm1"][layer], w["w_in"][layer], w["g_q_norm"][layer], w["w_uq"][layer],
            w["g_kv_norm"][layer], cm, sm, cr, sr)
        if cache is None:
            lat_all, krt_all, sk_all, sv_all = lat, krt, sk, sv
            s0 = jnp.zeros((b, HEAD_W, HEAD_W), F32)
            prev = jnp.zeros((b, CONV_W - 1, D_FF), F32)
        else:
            lat_all = jnp.concatenate([cache["lat"][layer], lat], axis=1)
            krt_all = jnp.concatenate([_tile_krope(cache["kr"][layer]).astype(BF16), krt], axis=1)
            sk_all = jnp.concatenate([cache["sk"][layer].reshape(b, -1, HEAD_W), sk], axis=1)
            sv_all = jnp.concatenate([cache["sv"][layer].reshape(b, -1, HEAD_W), sv], axis=1)
            st = cache["S"][layer]
            s0 = jnp.zeros((b, HEAD_W, HEAD_W), F32)
            for h in range(RET_HEADS):
                s0 = s0.at[:, rows[h], RET_DV * h:RET_DV * (h + 1)].set(st[:, h])
            prev = cache["conv"][layer]
        kn, v = _kvup(lat_all, w["w_ukv"][layer])
        mla = _mla(qn, qr, kn, v, krt_all)
        sb = _sb(sq, sk_all, sv_all)
        ret, s_new = _ret(rq, rk, rv, rg, s0, w["g_ret_norm"][layer])
        x1, h2 = _oproj(mla, ret, sb, x, mod, w["w_o"][layer], w["g_norm2"][layer])
        x, conv_state = _ffn(h2, x1, mod, prev, w["w_up"][layer], w["conv_w"][layer], w["conv_b"][layer],
                             w["w_down"][layer], w["g_final"], layer == depth - 1)
        s_nat = jnp.stack([s_new[:, rows[h], RET_DV * h:RET_DV * (h + 1)] for h in range(RET_HEADS)], axis=1)
        for lst, val in zip(new, (lat, kr, sk.reshape(b, l, SB_HEADS, SB_DK), sv.reshape(b, l, SB_HEADS, SB_DV),
                                  s_nat, conv_state)):
            lst.append(val)
    return x, [jnp.stack(s, axis=0) for s in new]


def kernel(x_prompt, x_sample, c_prompt, c_sample, cache_mla_latent, cache_mla_krope, cache_sb_k, cache_sb_v, state_ret, state_ffn_conv, w_in, g_q_norm, w_uq, g_kv_norm, w_ukv, g_ret_norm, w_o, w_up, conv_w, conv_b, w_down, g_norm1, g_norm2, w_ada, b_ada, g_final):
    depth = w_in.shape[0]
    bp = x_prompt.shape[0]
    row = lambda g: g.reshape(g.shape[0], 1, g.shape[-1])
    w = dict(
        w_in=w_in[:, :, _win_columns()].astype(BF16),
        w_uq=w_uq[:, :, _wuq_columns()].astype(BF16),
        w_ukv=w_ukv[:, :, _wukv_columns()].astype(BF16),
        w_o=w_o.astype(BF16), w_up=w_up.astype(BF16), w_down=w_down.astype(BF16),
        g_q_norm=row(g_q_norm), g_kv_norm=row(g_kv_norm), g_ret_norm=row(g_ret_norm),
        g_norm1=row(g_norm1), g_norm2=row(g_norm2), conv_w=conv_w, conv_b=row(conv_b),
        g_final=g_final.reshape(1, -1))
    mod = _ada(jnp.concatenate([c_prompt, c_sample], axis=0), w_ada, b_ada)
    mod = mod.reshape(depth, mod.shape[1], 6, D_MODEL)
    y_p, st_p = _trunk(x_prompt, [mod[l, :bp] for l in range(depth)], 0, None, w)
    cache = dict(lat=cache_mla_latent, kr=cache_mla_krope, sk=cache_sb_k, sv=cache_sb_v, S=state_ret,
                 conv=state_ffn_conv)
    y_s, st_s = _trunk(x_sample, [mod[l, bp:] for l in range(depth)], cache_mla_latent.shape[2], cache, w)
    return (y_p, y_s, *st_p, *st_s)
```

```python
import functools

import numpy as np
import jax
import jax.numpy as jnp
from jax import lax
from jax.experimental import pallas as pl
from jax.experimental.pallas import tpu as pltpu

D_MODEL = 1024
CHUNK = 64
MLA_HEADS = 8
MLA_NOPE = 64
MLA_ROPE = 32
MLA_V = 64
MLA_Q_RANK = 384
MLA_KV_RANK = 256
RET_HEADS = 4
RET_DK = 64
RET_DV = 64
SB_HEADS = 4
SB_DK = 64
SB_DV = 64
D_FF = 2816
CONV_W = 3
ROPE_BASE = 10000.0
EPS = 1e-6

HEAD_W = 256
ROW_TILE = 512
ATT_TILE = 256
RET_CHUNK = 256
FF_CHUNK = 1408
VMEM_LIMIT = 56 * 1024 * 1024

F32 = jnp.float32
BF16 = jnp.bfloat16

_ZQ, _ZKV, _RQ, _RK, _RV, _RG, _SQ, _SK, _SV, _KRT, _IN_COLS_P = (
    0, 384, 640, 896, 1152, 1408, 1664, 1920, 2176, 2432, 2688)


def _dot(a, b):
    return jnp.dot(a, b, preferred_element_type=F32)


def _dot_nt(a, b):
    return lax.dot_general(a, b, (((1,), (1,)), ((), ())), preferred_element_type=F32)


def _dot_tn(a, b):
    return lax.dot_general(a, b, (((0,), (0,)), ((), ())), preferred_element_type=F32)


def _rms(x):
    return x * lax.rsqrt(jnp.mean(x * x, axis=-1, keepdims=True) + EPS)


def _silu(x):
    return x / (1.0 + jnp.exp(-x))


def _params(*sem):
    return pltpu.CompilerParams(dimension_semantics=sem, vmem_limit_bytes=VMEM_LIMIT)


def _const_spec(shape):
    nd = len(shape)
    return pl.BlockSpec(shape, lambda *_: (0,) * nd)


def _ada_kernel(c_ref, w_ref, b_ref, o_ref):
    a = _silu(c_ref[...]).astype(BF16)
    o_ref[0] = _dot(a, w_ref[0].astype(BF16)) + b_ref[0]


def _ada(c_all, w_ada, b_ada):
    depth, d, n = w_ada.shape
    rows = c_all.shape[0]
    tn = 1536
    return pl.pallas_call(
        _ada_kernel,
        out_shape=jax.ShapeDtypeStruct((depth, rows, n), F32),
        grid=(depth, n // tn),
        in_specs=[pl.BlockSpec((rows, d), lambda l, j: (0, 0)),
                  pl.BlockSpec((1, d, tn), lambda l, j: (l, 0, j)),
                  pl.BlockSpec((1, 1, tn), lambda l, j: (l, 0, j))],
        out_specs=pl.BlockSpec((1, rows, tn), lambda l, j: (l, 0, j)),
        compiler_params=_params("parallel", "parallel"),
        name="ada_mod",
    )(c_all, w_ada, b_ada.reshape(depth, 1, n))


def _inproj_kernel(x_ref, mod_ref, g1_ref, win_ref, gq_ref, wuq_ref, gkv_ref, cm_ref, sm_ref, cr_ref, sr_ref,
                   qn_ref, qr_ref, lat_ref, krt_ref, kr_ref, rq_ref, rk_ref, rv_ref, rg_ref, sq_ref, sk_ref,
                   sv_ref):
    bb, tl, d = x_ref.shape
    rows = bb * tl
    mod = mod_ref[...]
    h = _rms(x_ref[...]) * g1_ref[...] * (1.0 + mod[:, 1:2, :]) + mod[:, 0:1, :]
    z = _dot(h.reshape(rows, d).astype(BF16), win_ref[...])

    def put(ref, val):
        ref[...] = val.reshape(bb, tl, val.shape[-1]).astype(ref.dtype)

    def rope(x1, x2, c_ref, s_ref):
        c, s = c_ref[...][None], s_ref[...][None]
        x1 = x1.reshape(bb, tl, x1.shape[-1])
        x2 = x2.reshape(bb, tl, x2.shape[-1])
        return x1 * c - x2 * s, x1 * s + x2 * c

    q = _dot((_rms(z[:, _ZQ:_ZKV]) * gq_ref[...]).astype(BF16), wuq_ref[...])
    put(qn_ref, q[:, :512])
    q1, q2 = rope(q[:, 512:640], q[:, 640:768], cm_ref, sm_ref)
    qr_ref[:, :, 0:128] = q1.astype(qr_ref.dtype)
    qr_ref[:, :, 128:256] = q2.astype(qr_ref.dtype)
    put(lat_ref, _rms(z[:, _ZKV:_RQ]) * gkv_ref[...])
    k1, k2 = rope(z[:, _KRT:_KRT + 128], z[:, _KRT + 128:_KRT + 256], cm_ref, sm_ref)
    krt_ref[:, :, 0:128] = k1.astype(krt_ref.dtype)
    krt_ref[:, :, 128:256] = k2.astype(krt_ref.dtype)
    lane = lax.broadcasted_iota(jnp.int32, (1, 1, MLA_ROPE), 2)
    kr_ref[...] = jnp.where(lane < MLA_ROPE // 2, k1[:, :, :MLA_ROPE], k2[:, :, :MLA_ROPE])
    a1, a2 = rope(z[:, _RQ:_RQ + 128], z[:, _RQ + 128:_RK], cr_ref, sr_ref)
    rq_ref[:, :, 0:128] = (a1 * (RET_DK ** -0.5)).astype(rq_ref.dtype)
    rq_ref[:, :, 128:256] = (a2 * (RET_DK ** -0.5)).astype(rq_ref.dtype)
    b1, b2 = rope(z[:, _RK:_RK + 128], z[:, _RK + 128:_RV], cr_ref, sr_ref)
    rk_ref[:, :, 0:128] = b1
    rk_ref[:, :, 128:256] = b2
    put(rv_ref, z[:, _RV:_RG])
    put(rg_ref, z[:, _RG:_SQ])
    put(sq_ref, z[:, _SQ:_SK] * (SB_DK ** -0.5))
    put(sk_ref, z[:, _SK:_SV])
    put(sv_ref, z[:, _SV:_KRT])


def _inproj(x, mod, g1, win, gq, wuq, gkv, cm, sm, cr, sr):
    b, l, d = x.shape
    tl = min(l, ROW_TILE)
    bb = ROW_TILE // tl
    grid = (b // bb, l // tl)
    tok = lambda w: pl.BlockSpec((bb, tl, w), lambda i, t: (i, t, 0))
    pos = pl.BlockSpec((tl, 128), lambda i, t: (t, 0))
    widths = [(512, BF16), (256, BF16), (256, F32), (256, BF16), (MLA_ROPE, F32), (256, BF16), (256, F32),
              (256, BF16), (256, F32), (256, BF16), (256, F32), (256, F32)]
    return pl.pallas_call(
        _inproj_kernel,
        out_shape=[jax.ShapeDtypeStruct((b, l, w), dt) for w, dt in widths],
        grid=grid,
        in_specs=[tok(d), pl.BlockSpec((bb, 6, d), lambda i, t: (i, 0, 0)), _const_spec(g1.shape),
                  _const_spec(win.shape), _const_spec(gq.shape), _const_spec(wuq.shape), _const_spec(gkv.shape),
                  pos, pos, pos, pos],
        out_specs=[tok(w) for w, _ in widths],
        compiler_params=_params("parallel", "parallel"),
        name="in_proj",
    )(x, mod, g1, win, gq, wuq, gkv, cm, sm, cr, sr)


def _kvup_kernel(lat_ref, w_ref, kn_ref, v_ref):
    kv = _dot(lat_ref[0].astype(BF16), w_ref[...])
    kn_ref[0] = kv[:, :512].astype(kn_ref.dtype)
    v_ref[0] = kv[:, 512:].astype(v_ref.dtype)


def _kvup(lat_all, wukv):
    b, lk, r = lat_all.shape
    blk = lambda w: pl.BlockSpec((1, lk, w), lambda i: (i, 0, 0))
    return pl.pallas_call(
        _kvup_kernel,
        out_shape=[jax.ShapeDtypeStruct((b, lk, 512), BF16)] * 2,
        grid=(b,),
        in_specs=[blk(r), _const_spec(wukv.shape)],
        out_specs=[blk(512), blk(512)],
        compiler_params=_params("parallel"),
        name="kv_up",
    )(lat_all, wukv)


def _mla_kernel(qn_ref, qr_ref, kn_ref, v_ref, kr_ref, o_ref, *, tq, tk, off, scale):
    qi = pl.program_id(1)
    dstart = pl.multiple_of(off + qi * tq, CHUNK)
    nfull = (off + qi * tq) // tk
    lane = lax.broadcasted_iota(jnp.int32, (1, HEAD_W), 1)
    row_c = (lax.broadcasted_iota(jnp.int32, (4 * tq, 1), 0) % tq) // CHUNK
    col_c = lax.broadcasted_iota(jnp.int32, (1, tq), 1) // CHUNK
    dmask = col_c <= row_c
    qr = qr_ref[0]
    masks = [(lane >= 64 * hh) & (lane < 64 * hh + 64) for hh in range(4)]
    outs = []
    for g in range(MLA_HEADS // 4):
        gs = slice(HEAD_W * g, HEAD_W * (g + 1))
        qn = qn_ref[0, :, gs]
        q = jnp.concatenate(
            [jnp.concatenate([jnp.where(masks[hh], qn, jnp.zeros_like(qn)),
                              jnp.where(((lane & 127) >> 4) == 4 * g + hh, qr, jnp.zeros_like(qr))], axis=-1)
             for hh in range(4)], axis=0)

        def load(ks):
            return jnp.concatenate([kn_ref[0, ks, gs], kr_ref[0, ks, :]], axis=-1), v_ref[0, ks, gs]

        kcat, vt = load(pl.ds(dstart, tq))
        s = jnp.where(dmask, _dot_nt(q, kcat) * scale, -1e30)
        m = jnp.max(s, axis=-1, keepdims=True)
        p = jnp.exp(s - m)
        state = (m, jnp.sum(p, axis=-1, keepdims=True), _dot(p.astype(BF16), vt))

        def body(j, state):
            m, l, acc = state
            kcat, vt = load(pl.ds(pl.multiple_of(j * tk, tk), tk))
            s = _dot_nt(q, kcat) * scale
            m_new = jnp.maximum(m, jnp.max(s, axis=-1, keepdims=True))
            alpha = jnp.exp(m - m_new)
            p = jnp.exp(s - m_new)
            return (m_new, alpha * l + jnp.sum(p, axis=-1, keepdims=True),
                    alpha * acc + _dot(p.astype(BF16), vt))

        m, l, acc = lax.fori_loop(0, nfull, body, state)
        o = acc / l
        out_g = jnp.zeros((tq, HEAD_W), F32)
        for hh in range(4):
            out_g = jnp.where(masks[hh], o[hh * tq:(hh + 1) * tq], out_g)
        outs.append(out_g)
    o_ref[0] = jnp.concatenate(outs, axis=-1).astype(o_ref.dtype)


def _mla(qn, qr, kn, v, krt):
    b, l, _ = qn.shape
    lk = kn.shape[1]
    tq = min(l, ATT_TILE)
    qblk = lambda w: pl.BlockSpec((1, tq, w), lambda i, t: (i, t, 0))
    kblk = lambda w: pl.BlockSpec((1, lk, w), lambda i, t: (i, 0, 0))
    kern = functools.partial(_mla_kernel, tq=tq, tk=ATT_TILE, off=lk - l,
                             scale=(MLA_NOPE + MLA_ROPE) ** -0.5)
    return pl.pallas_call(
        kern,
        out_shape=jax.ShapeDtypeStruct((b, l, 512), BF16),
        grid=(b, l // tq),
        in_specs=[qblk(512), qblk(256), kblk(512), kblk(512), kblk(256)],
        out_specs=qblk(512),
        compiler_params=_params("parallel", "parallel"),
        name="mla_attn",
    )(qn, qr, kn, v, krt)


def _suffix_matrix(n):
    j = lax.broadcasted_iota(jnp.int32, (n, n), 0)
    s = lax.broadcasted_iota(jnp.int32, (n, n), 1)
    return jnp.where(j > s, 1.0, 0.0).astype(BF16)


def _sb_kernel(q_ref, k_ref, v_ref, o_ref, kb_ref, vb_ref, *, tq, tk, off):
    qi = pl.program_id(1)

    @pl.when(qi == 0)
    def _():
        kb_ref[...] = k_ref[0].astype(BF16)
        vb_ref[...] = v_ref[0].astype(BF16)

    dstart = pl.multiple_of(off + qi * tq, CHUNK)
    nfull = (off + qi * tq) // tk
    lane = lax.broadcasted_iota(jnp.int32, (1, HEAD_W), 1)
    nh = SB_HEADS
    dmask = (lax.broadcasted_iota(jnp.int32, (1, tq), 1)
             < lax.broadcasted_iota(jnp.int32, (nh * tq, 1), 0) % tq)
    t_diag = _suffix_matrix(tq)
    t_full = t_diag if tk == tq else _suffix_matrix(tk)
    q = q_ref[0]

    def tile(qh, ks, run, mask, tmat):
        zz = _dot_nt(qh, kb_ref[ks, :])
        t = jnp.log1p(jnp.exp(-jnp.abs(zz)))
        log_1mb = -jnp.maximum(zz, 0.0) - t
        log_beta = jnp.minimum(zz, 0.0) - t
        if mask is not None:
            log_1mb = jnp.where(mask, log_1mb, 0.0)
        hi = log_1mb.astype(BF16)
        lo = (log_1mb - hi.astype(F32)).astype(BF16)
        after = _dot(hi, tmat) + _dot(lo, tmat)
        w = jnp.exp(log_beta + after + run)
        if mask is not None:
            w = jnp.where(mask, w, 0.0)
        pv = _dot(w.astype(BF16), vb_ref[ks, :])
        return pv, run + jnp.sum(log_1mb, axis=-1, keepdims=True)

    masks = [(lane >= 64 * h) & (lane < 64 * h + 64) for h in range(nh)]
    qs = jnp.concatenate([jnp.where(mk, q, jnp.zeros_like(q)) for mk in masks], axis=0)
    state = tile(qs, pl.ds(dstart, tq), jnp.zeros((nh * tq, 1), F32), dmask, t_diag)

    def body(i, state):
        acc, run = state
        pv, run = tile(qs, pl.ds(pl.multiple_of((nfull - 1 - i) * tk, tk), tk), run, None, t_full)
        return acc + pv, run

    acc, _ = lax.fori_loop(0, nfull, body, state)
    out = jnp.zeros((tq, HEAD_W), F32)
    for h in range(nh):
        out = jnp.where(masks[h], acc[h * tq:(h + 1) * tq], out)
    o_ref[0] = out.astype(o_ref.dtype)


def _sb(sq, sk_all, sv_all):
    b, l, _ = sq.shape
    lk = sk_all.shape[1]
    tq = min(l, ATT_TILE)
    qblk = pl.BlockSpec((1, tq, HEAD_W), lambda i, t: (i, t, 0))
    kblk = pl.BlockSpec((1, lk, HEAD_W), lambda i, t: (i, 0, 0))
    kern = functools.partial(_sb_kernel, tq=tq, tk=ATT_TILE, off=lk - l)
    return pl.pallas_call(
        kern,
        out_shape=jax.ShapeDtypeStruct((b, l, HEAD_W), BF16),
        grid=(b, l // tq),
        in_specs=[qblk, kblk, kblk],
        out_specs=qblk,
        scratch_shapes=[pltpu.VMEM((lk, HEAD_W), BF16), pltpu.VMEM((lk, HEAD_W), BF16)],
        compiler_params=_params("parallel", "arbitrary"),
        name="sb_attn",
    )(sq, sk_all, sv_all)


def _ret_kernel(q_ref, k_ref, v_ref, g_ref, s0_ref, gn_ref, dec_ref, qd_ref, kd_ref, sd_ref,
                o_ref, s_ref, *, c):
    n = q_ref.shape[1] // c
    lane = lax.broadcasted_iota(jnp.int32, (1, HEAD_W), 1)
    krow = lax.broadcasted_iota(jnp.int32, (HEAD_W, 1), 0)
    bd_mask = ((krow & 127) >> 5) == (lane >> 6)
    s_ref[0] = s0_ref[0]

    def step(i, carry):
        rs = pl.ds(pl.multiple_of(i * c, c), c)
        q = q_ref[0, rs, :]
        kf = k_ref[0, rs, :]
        k = kf.astype(BF16)
        v = v_ref[0, rs, :]
        state = s_ref[0]
        o = _dot(q, state.astype(BF16)) * qd_ref[...]
        for h in range(RET_HEADS):
            mk = ((lane & 127) >> 5) == h
            mv = (lane >> 6) == h
            att = _dot_nt(jnp.where(mk, q, jnp.zeros_like(q)), k) * dec_ref[h]
            o = o + jnp.where(mv, _dot(att.astype(BF16), v), 0.0)
        upd = _dot_tn((kf * kd_ref[...]).astype(BF16), v)
        s_ref[0] = sd_ref[...] * state + jnp.where(bd_mask, upd, 0.0)
        mu = jnp.zeros_like(o)
        for h in range(RET_HEADS):
            mv = (lane >> 6) == h
            mu = mu + jnp.where(mv, jnp.sum(jnp.where(mv, o, 0.0), axis=-1, keepdims=True), 0.0)
        dlt = o - mu * (1.0 / RET_DV)
        var = jnp.zeros_like(o)
        for h in range(RET_HEADS):
            mv = (lane >> 6) == h
            var = var + jnp.where(mv, jnp.sum(jnp.where(mv, dlt * dlt, 0.0), axis=-1, keepdims=True), 0.0)
        ro = dlt * lax.rsqrt(var * (1.0 / RET_DV) + EPS) * gn_ref[...]
        o_ref[0, rs, :] = (_silu(g_ref[0, rs, :]) * ro).astype(o_ref.dtype)
        return carry

    lax.fori_loop(0, n, step, 0)


def _ret_consts(c):
    lg = jnp.log(1.0 - 2.0 ** (-5.0 - jnp.arange(RET_HEADS, dtype=F32)))
    i = jnp.arange(c, dtype=F32)
    rel = i[:, None] - i[None, :]
    dec = jnp.where(rel >= 0, jnp.exp(lg[:, None, None] * jnp.maximum(rel, 0.0)), 0.0)
    v_head = jnp.arange(HEAD_W) // RET_DV
    k_head = (jnp.arange(HEAD_W) % 128) // (RET_DK // 2)
    qd = jnp.exp(lg[None, v_head] * (i[:, None] + 1.0))
    kd = jnp.exp(lg[None, k_head] * (c - 1.0 - i[:, None]))
    sd = jnp.exp(lg * c)[v_head][None, :]
    return dec, qd, kd, sd


def _ret(rq, rk, rv, rg, s0, gn):
    b, l, _ = rq.shape
    c = min(l, RET_CHUNK)
    dec, qd, kd, sd = _ret_consts(c)
    blk = pl.BlockSpec((1, l, HEAD_W), lambda i: (i, 0, 0))
    sblk = pl.BlockSpec((1, HEAD_W, HEAD_W), lambda i: (i, 0, 0))
    return pl.pallas_call(
        functools.partial(_ret_kernel, c=c),
        out_shape=[jax.ShapeDtypeStruct((b, l, HEAD_W), BF16), jax.ShapeDtypeStruct((b, HEAD_W, HEAD_W), F32)],
        grid=(b,),
        in_specs=[blk, blk, blk, blk, sblk, _const_spec(gn.shape), _const_spec(dec.shape), _const_spec(qd.shape),
                  _const_spec(kd.shape), _const_spec(sd.shape)],
        out_specs=[blk, sblk],
        compiler_params=_params("parallel"),
        name="retention",
    )(rq, rk, rv, rg, s0, gn, dec, qd, kd, sd)


def _oproj_kernel(mla_ref, ret_ref, sb_ref, x_ref, mod_ref, wo_ref, g2_ref, x1_ref, h2_ref):
    bb, tl, d = x_ref.shape
    rows = bb * tl
    cat = jnp.concatenate([mla_ref[...].reshape(rows, 512), ret_ref[...].reshape(rows, HEAD_W),
                           sb_ref[...].reshape(rows, HEAD_W)], axis=-1)
    mix = _dot(cat, wo_ref[...]).reshape(bb, tl, d)
    mod = mod_ref[...]
    x1 = x_ref[...] + mod[:, 2:3, :] * mix
    x1_ref[...] = x1
    h2_ref[...] = (_rms(x1) * g2_ref[...] * (1.0 + mod[:, 4:5, :]) + mod[:, 3:4, :]).astype(h2_ref.dtype)


def _oproj(mla, ret, sb, x, mod, wo, g2):
    b, l, d = x.shape
    tl = min(l, ROW_TILE)
    bb = ROW_TILE // tl
    tok = lambda w: pl.BlockSpec((bb, tl, w), lambda i, t: (i, t, 0))
    return pl.pallas_call(
        _oproj_kernel,
        out_shape=[jax.ShapeDtypeStruct((b, l, d), F32), jax.ShapeDtypeStruct((b, l, d), BF16)],
        grid=(b // bb, l // tl),
        in_specs=[tok(512), tok(HEAD_W), tok(HEAD_W), tok(d), pl.BlockSpec((bb, 6, d), lambda i, t: (i, 0, 0)),
                  _const_spec(wo.shape), _const_spec(g2.shape)],
        out_specs=[tok(d), tok(d)],
        compiler_params=_params("parallel", "parallel"),
        name="out_proj",
    )(mla, ret, sb, x, mod, wo, g2)


def _ffn_kernel(h_ref, x_ref, mod_ref, prev_ref, wup_ref, cw_ref, cb_ref, wdn_ref, gf_ref,
                o_ref, st_ref, a_scr, carry, *, final_norm):
    bb, tl, d = x_ref.shape
    rows = bb * tl
    t = pl.program_id(1)
    nt = pl.num_programs(1)

    @pl.when(t == 0)
    def _():
        carry[:, 6:8, :] = prev_ref[...]

    h = h_ref[...].reshape(rows, d)
    f = jnp.zeros((rows, d), F32)
    for c0 in range(0, D_FF, FF_CHUNK):
        cs = slice(c0, c0 + FF_CHUNK)
        a = _dot(h, wup_ref[:, cs]).reshape(bb, tl, FF_CHUNK)
        b = _dot(h, wup_ref[:, D_FF + c0:D_FF + c0 + FF_CHUNK])
        a_scr[:, 8:, :] = a
        a_scr[:, 6:8, :] = carry[:, 6:8, cs]
        cw = cw_ref[...]
        conv = (cb_ref[:, cs] + cw[0:1, cs] * a_scr[:, 6:6 + tl, :] + cw[1:2, cs] * a_scr[:, 7:7 + tl, :]
                + cw[2:3, cs] * a)
        carry[:, 6:8, cs] = a_scr[:, tl + 6:tl + 8, :]
        y = (_silu(conv).reshape(rows, FF_CHUNK) * b).astype(BF16)
        f = f + _dot(y, wdn_ref[cs, :])
    x2 = x_ref[...] + mod_ref[...][:, 5:6, :] * f.reshape(bb, tl, d)
    if final_norm:
        x2 = _rms(x2) * gf_ref[...]
    o_ref[...] = x2

    @pl.when(t == nt - 1)
    def _():
        st_ref[...] = carry[:, 6:8, :]


def _ffn(h2, x1, mod, prev, wup, cw, cb, wdn, gf, final_norm):
    b, l, d = x1.shape
    tl = min(l, ROW_TILE)
    bb = ROW_TILE // tl
    tok = lambda: pl.BlockSpec((bb, tl, d), lambda i, t: (i, t, 0))
    st = pl.BlockSpec((bb, CONV_W - 1, D_FF), lambda i, t: (i, 0, 0))
    once = dict(pipeline_mode=pl.Buffered(1))
    return pl.pallas_call(
        functools.partial(_ffn_kernel, final_norm=final_norm),
        out_shape=[jax.ShapeDtypeStruct((b, l, d), F32), jax.ShapeDtypeStruct((b, CONV_W - 1, D_FF), F32)],
        grid=(b // bb, l // tl),
        in_specs=[tok(), tok(), pl.BlockSpec((bb, 6, d), lambda i, t: (i, 0, 0)), st,
                  pl.BlockSpec(wup.shape, lambda i, t: (0, 0), **once), _const_spec(cw.shape),
                  _const_spec(cb.shape), pl.BlockSpec(wdn.shape, lambda i, t: (0, 0), **once),
                  _const_spec(gf.shape)],
        out_specs=[tok(), st],
        scratch_shapes=[pltpu.VMEM((bb, tl + 8, FF_CHUNK), F32), pltpu.VMEM((bb, 8, D_FF), F32)],
        compiler_params=_params("parallel", "arbitrary"),
        name="conv_ffn",
    )(h2, x1, mod, prev, wup, cw, cb, wdn, gf)


def _half_split(w, heads):
    lead = w.shape[:-1]
    dim = w.shape[-1] // heads
    w = w.reshape(lead + (heads, 2, dim // 2))
    return jnp.swapaxes(w, -3, -2).reshape(lead + (heads * dim,))


def _permute_win(w):
    half = MLA_ROPE // 2
    tiled = lambda a: jnp.tile(a, (1, 1, MLA_HEADS))
    return jnp.concatenate([w[..., 0:640], _half_split(w[..., 672:928], RET_HEADS),
                            _half_split(w[..., 928:1184], RET_HEADS), w[..., 1184:2464],
                            tiled(w[..., 640:640 + half]), tiled(w[..., 640 + half:672])], axis=-1)


def _permute_wuq(w):
    half = MLA_ROPE // 2
    w = w.reshape(w.shape[:-1] + (MLA_HEADS, MLA_NOPE + MLA_ROPE))
    flat = lambda a: a.reshape(a.shape[:-2] + (-1,))
    return jnp.concatenate([flat(w[..., :MLA_NOPE]), flat(w[..., MLA_NOPE:MLA_NOPE + half]),
                            flat(w[..., MLA_NOPE + half:])], axis=-1)


def _permute_wukv(w):
    w = w.reshape(w.shape[:-1] + (MLA_HEADS, MLA_NOPE + MLA_V))
    flat = lambda a: a.reshape(a.shape[:-2] + (-1,))
    return jnp.concatenate([flat(w[..., :MLA_NOPE]), flat(w[..., MLA_NOPE:])], axis=-1)


def _ret_state_in(st):
    b = st.shape[0]
    st = st.reshape(b, RET_HEADS, 2, RET_DK // 2, RET_DV).transpose(0, 2, 1, 3, 4)
    eye = jnp.eye(RET_HEADS, dtype=st.dtype)
    full = st[:, :, :, :, None, :] * eye[None, None, :, None, :, None]
    return full.reshape(b, HEAD_W, HEAD_W)


def _ret_state_out(s):
    b = s.shape[0]
    s = s.reshape(b, 2, RET_HEADS, RET_DK // 2, RET_HEADS, RET_DV)
    blocks = [s[:, :, h, :, h, :].reshape(b, RET_DK, RET_DV) for h in range(RET_HEADS)]
    return jnp.stack(blocks, axis=1)


def _rope_tables(pos, dim, reps):
    inv = ROPE_BASE ** (-jnp.arange(0, dim, 2, dtype=F32) / dim)
    ang = pos.astype(F32)[:, None] * inv[None, :]
    return jnp.tile(jnp.cos(ang), (1, reps)), jnp.tile(jnp.sin(ang), (1, reps))


def _tile_krope(kr):
    half = MLA_ROPE // 2
    return jnp.concatenate([jnp.tile(kr[..., :half], (1, 1, MLA_HEADS)), jnp.tile(kr[..., half:], (1, 1, MLA_HEADS))],
                           axis=-1)


def _trunk(x, mods, pos0, cache, w):
    b, l, _ = x.shape
    depth = len(mods)
    pos = pos0 + jnp.arange(l)
    cm, sm = _rope_tables(pos, MLA_ROPE, MLA_HEADS)
    cr, sr = _rope_tables(pos, RET_DK, RET_HEADS)
    new = [[] for _ in range(6)]
    for layer in range(depth):
        mod = mods[layer]
        (qn, qr, lat, krt, kr, rq, rk, rv, rg, sq, sk, sv) = _inproj(
            x, mod, w["g_norm1"][layer], w["w_in"][layer], w["g_q_norm"][layer], w["w_uq"][layer],
            w["g_kv_norm"][layer], cm, sm, cr, sr)
        if cache is None:
            lat_all, krt_all, sk_all, sv_all = lat, krt, sk, sv
            s0 = jnp.zeros((b, HEAD_W, HEAD_W), F32)
            prev = jnp.zeros((b, CONV_W - 1, D_FF), F32)
        else:
            lat_all = jnp.concatenate([cache["lat"][layer], lat], axis=1)
            krt_all = jnp.concatenate([_tile_krope(cache["kr"][layer]).astype(BF16), krt], axis=1)
            sk_all = jnp.concatenate([cache["sk"][layer].reshape(b, -1, HEAD_W), sk], axis=1)
            sv_all = jnp.concatenate([cache["sv"][layer].reshape(b, -1, HEAD_W), sv], axis=1)
            s0 = _ret_state_in(cache["S"][layer])
            prev = cache["conv"][layer]
        kn, v = _kvup(lat_all, w["w_ukv"][layer])
        mla = _mla(qn, qr, kn, v, krt_all)
        sb = _sb(sq, sk_all, sv_all)
        ret, s_new = _ret(rq, rk, rv, rg, s0, w["g_ret_norm"][layer])
        x1, h2 = _oproj(mla, ret, sb, x, mod, w["w_o"][layer], w["g_norm2"][layer])
        x, conv_state = _ffn(h2, x1, mod, prev, w["w_up"][layer], w["conv_w"][layer], w["conv_b"][layer],
                             w["w_down"][layer], w["g_final"], layer == depth - 1)
        for lst, val in zip(new, (lat, kr, sk.reshape(b, l, SB_HEADS, SB_DK), sv.reshape(b, l, SB_HEADS, SB_DV),
                                  _ret_state_out(s_new), conv_state)):
            lst.append(val)
    return x, [jnp.stack(s, axis=0) for s in new]


def kernel(x_prompt, x_sample, c_prompt, c_sample, cache_mla_latent, cache_mla_krope, cache_sb_k, cache_sb_v, state_ret, state_ffn_conv, w_in, g_q_norm, w_uq, g_kv_norm, w_ukv, g_ret_norm, w_o, w_up, conv_w, conv_b, w_down, g_norm1, g_norm2, w_ada, b_ada, g_final):
    depth = w_in.shape[0]
    bp = x_prompt.shape[0]
    row = lambda g: g.reshape(g.shape[0], 1, g.shape[-1])
    w = dict(
        w_in=_permute_win(w_in).astype(BF16),
        w_uq=_permute_wuq(w_uq).astype(BF16),
        w_ukv=_permute_wukv(w_ukv).astype(BF16),
        w_o=w_o.astype(BF16), w_up=w_up.astype(BF16), w_down=w_down.astype(BF16),
        g_q_norm=row(g_q_norm), g_kv_norm=row(g_kv_norm), g_ret_norm=row(g_ret_norm),
        g_norm1=row(g_norm1), g_norm2=row(g_norm2), conv_w=conv_w, conv_b=row(conv_b),
        g_final=g_final.reshape(1, -1))
    mod = _ada(jnp.concatenate([c_prompt, c_sample], axis=0), w_ada, b_ada)
    mod = mod.reshape(depth, mod.shape[1], 6, D_MODEL)
    y_p, st_p = _trunk(x_prompt, [mod[l, :bp] for l in range(depth)], 0, None, w)
    cache = dict(lat=cache_mla_latent, kr=cache_mla_krope, sk=cache_sb_k, sv=cache_sb_v, S=state_ret,
                 conv=state_ffn_conv)
    y_s, st_s = _trunk(x_sample, [mod[l, bp:] for l in range(depth)], cache_mla_latent.shape[2], cache, w)
    return (y_p, y_s, *st_p, *st_s)
```

```python
import functools

import numpy as np
import jax
import jax.numpy as jnp
from jax import lax
from jax.experimental import pallas as pl
from jax.experimental.pallas import tpu as pltpu

D_MODEL = 1024
CHUNK = 64
MLA_HEADS = 8
MLA_NOPE = 64
MLA_ROPE = 32
MLA_V = 64
MLA_Q_RANK = 384
MLA_KV_RANK = 256
RET_HEADS = 4
RET_DK = 64
RET_DV = 64
SB_HEADS = 4
SB_DK = 64
SB_DV = 64
D_FF = 2816
CONV_W = 3
ROPE_BASE = 10000.0
EPS = 1e-6

HEAD_W = 256
ROW_TILE = 512
ATT_TILE = 256
RET_CHUNK = 256
FF_CHUNK = 1408
SB_LOG_ZERO = -104.0
VMEM_LIMIT = 56 * 1024 * 1024

F32 = jnp.float32
BF16 = jnp.bfloat16

_ZQ, _ZKV, _RQ, _RK, _RV, _RG, _SQ, _SK, _SV, _KRT, _IN_COLS_P = (
    0, 384, 640, 896, 1152, 1408, 1664, 1920, 2176, 2432, 2688)


def _dot(a, b):
    return jnp.dot(a, b, preferred_element_type=F32)


def _dot_nt(a, b):
    return lax.dot_general(a, b, (((1,), (1,)), ((), ())), preferred_element_type=F32)


def _dot_tn(a, b):
    return lax.dot_general(a, b, (((0,), (0,)), ((), ())), preferred_element_type=F32)


def _rms(x):
    return x * lax.rsqrt(jnp.mean(x * x, axis=-1, keepdims=True) + EPS)


def _silu(x):
    return x / (1.0 + jnp.exp(-x))


def _params(*sem):
    return pltpu.CompilerParams(dimension_semantics=sem, vmem_limit_bytes=VMEM_LIMIT)


def _const_spec(shape):
    nd = len(shape)
    return pl.BlockSpec(shape, lambda *_: (0,) * nd)


def _ada_kernel(c_ref, w_ref, b_ref, o_ref):
    a = _silu(c_ref[...]).astype(BF16)
    o_ref[0] = _dot(a, w_ref[0].astype(BF16)) + b_ref[0]


def _ada(c_all, w_ada, b_ada):
    depth, d, n = w_ada.shape
    rows = c_all.shape[0]
    tn = 1536
    return pl.pallas_call(
        _ada_kernel,
        out_shape=jax.ShapeDtypeStruct((depth, rows, n), F32),
        grid=(depth, n // tn),
        in_specs=[pl.BlockSpec((rows, d), lambda l, j: (0, 0)),
                  pl.BlockSpec((1, d, tn), lambda l, j: (l, 0, j)),
                  pl.BlockSpec((1, 1, tn), lambda l, j: (l, 0, j))],
        out_specs=pl.BlockSpec((1, rows, tn), lambda l, j: (l, 0, j)),
        compiler_params=_params("parallel", "parallel"),
        name="ada_mod",
    )(c_all, w_ada, b_ada.reshape(depth, 1, n))


def _inproj_kernel(x_ref, mod_ref, g1_ref, win_ref, gq_ref, wuq_ref, gkv_ref, cm_ref, sm_ref, cr_ref, sr_ref,
                   qn_ref, qr_ref, lat_ref, krt_ref, kr_ref, rq_ref, rk_ref, rv_ref, rg_ref, sq_ref, sk_ref,
                   sv_ref):
    bb, tl, d = x_ref.shape
    rows = bb * tl
    mod = mod_ref[...]
    h = _rms(x_ref[...]) * g1_ref[...] * (1.0 + mod[:, 1:2, :]) + mod[:, 0:1, :]
    z = _dot(h.reshape(rows, d).astype(BF16), win_ref[...])

    def put(ref, val):
        ref[...] = val.reshape(bb, tl, val.shape[-1]).astype(ref.dtype)

    def rope(x1, x2, c_ref, s_ref):
        c, s = c_ref[...][None], s_ref[...][None]
        x1 = x1.reshape(bb, tl, x1.shape[-1])
        x2 = x2.reshape(bb, tl, x2.shape[-1])
        return x1 * c - x2 * s, x1 * s + x2 * c

    q = _dot((_rms(z[:, _ZQ:_ZKV]) * gq_ref[...]).astype(BF16), wuq_ref[...])
    put(qn_ref, q[:, :512])
    q1, q2 = rope(q[:, 512:640], q[:, 640:768], cm_ref, sm_ref)
    qr_ref[:, :, 0:128] = q1.astype(qr_ref.dtype)
    qr_ref[:, :, 128:256] = q2.astype(qr_ref.dtype)
    put(lat_ref, _rms(z[:, _ZKV:_RQ]) * gkv_ref[...])
    k1, k2 = rope(z[:, _KRT:_KRT + 128], z[:, _KRT + 128:_KRT + 256], cm_ref, sm_ref)
    krt_ref[:, :, 0:128] = k1.astype(krt_ref.dtype)
    krt_ref[:, :, 128:256] = k2.astype(krt_ref.dtype)
    lane = lax.broadcasted_iota(jnp.int32, (1, 1, MLA_ROPE), 2)
    kr_ref[...] = jnp.where(lane < MLA_ROPE // 2, k1[:, :, :MLA_ROPE], k2[:, :, :MLA_ROPE])
    a1, a2 = rope(z[:, _RQ:_RQ + 128], z[:, _RQ + 128:_RK], cr_ref, sr_ref)
    rq_ref[:, :, 0:128] = (a1 * (RET_DK ** -0.5)).astype(rq_ref.dtype)
    rq_ref[:, :, 128:256] = (a2 * (RET_DK ** -0.5)).astype(rq_ref.dtype)
    b1, b2 = rope(z[:, _RK:_RK + 128], z[:, _RK + 128:_RV], cr_ref, sr_ref)
    rk_ref[:, :, 0:128] = b1
    rk_ref[:, :, 128:256] = b2
    put(rv_ref, z[:, _RV:_RG])
    put(rg_ref, z[:, _RG:_SQ])
    put(sq_ref, z[:, _SQ:_SK] * (SB_DK ** -0.5))
    put(sk_ref, z[:, _SK:_SV])
    put(sv_ref, z[:, _SV:_KRT])


def _inproj(x, mod, g1, win, gq, wuq, gkv, cm, sm, cr, sr):
    b, l, d = x.shape
    tl = min(l, ROW_TILE)
    bb = ROW_TILE // tl
    grid = (b // bb, l // tl)
    tok = lambda w: pl.BlockSpec((bb, tl, w), lambda i, t: (i, t, 0))
    pos = pl.BlockSpec((tl, 128), lambda i, t: (t, 0))
    widths = [(512, BF16), (256, BF16), (256, F32), (256, BF16), (MLA_ROPE, F32), (256, BF16), (256, F32),
              (256, BF16), (256, F32), (256, BF16), (256, F32), (256, F32)]
    return pl.pallas_call(
        _inproj_kernel,
        out_shape=[jax.ShapeDtypeStruct((b, l, w), dt) for w, dt in widths],
        grid=grid,
        in_specs=[tok(d), pl.BlockSpec((bb, 6, d), lambda i, t: (i, 0, 0)), _const_spec(g1.shape),
                  _const_spec(win.shape), _const_spec(gq.shape), _const_spec(wuq.shape), _const_spec(gkv.shape),
                  pos, pos, pos, pos],
        out_specs=[tok(w) for w, _ in widths],
        compiler_params=_params("parallel", "parallel"),
        name="in_proj",
    )(x, mod, g1, win, gq, wuq, gkv, cm, sm, cr, sr)


def _kvup_kernel(lat_ref, w_ref, kn_ref, v_ref):
    kv = _dot(lat_ref[0].astype(BF16), w_ref[...])
    kn_ref[0] = kv[:, :512].astype(kn_ref.dtype)
    v_ref[0] = kv[:, 512:].astype(v_ref.dtype)


def _kvup(lat_all, wukv):
    b, lk, r = lat_all.shape
    blk = lambda w: pl.BlockSpec((1, lk, w), lambda i: (i, 0, 0))
    return pl.pallas_call(
        _kvup_kernel,
        out_shape=[jax.ShapeDtypeStruct((b, lk, 512), BF16)] * 2,
        grid=(b,),
        in_specs=[blk(r), _const_spec(wukv.shape)],
        out_specs=[blk(512), blk(512)],
        compiler_params=_params("parallel"),
        name="kv_up",
    )(lat_all, wukv)


def _fori_pairs(n, body, init):
    carry = lax.fori_loop(0, n // 2, lambda i, c: body(2 * i + 1, body(2 * i, c)), init)
    return lax.fori_loop(0, n % 2, lambda i, c: body(n - 1, c), carry)


def _fold(op, x):
    w = x.shape[1]
    parts = [x[:, i:i + 128] for i in range(0, w, 128)]
    out = parts[0]
    for p in parts[1:]:
        out = op(out, p)
    return out


def _mla_kernel(qn_ref, qr_ref, knp_ref, vp_ref, krp_ref, knd_ref, vd_ref, krd_ref, o_ref, s_scr,
                *, tq, tk, off, scale):
    qi = pl.program_id(1)
    nfull = (off + qi * tq) // tk
    rows = 4 * tq
    lane = lax.broadcasted_iota(jnp.int32, (1, HEAD_W), 1)
    row_c = (lax.broadcasted_iota(jnp.int32, (rows, 1), 0) % tq) // CHUNK
    col_c = lax.broadcasted_iota(jnp.int32, (1, tq), 1) // CHUNK
    dmask = col_c <= row_c
    qr = qr_ref[0]
    masks = [(lane >= 64 * hh) & (lane < 64 * hh + 64) for hh in range(4)]
    c = scale * 1.4426950408889634
    outs = []
    for g in range(MLA_HEADS // 4):
        gs = slice(HEAD_W * g, HEAD_W * (g + 1))
        qn = qn_ref[0, :, gs]
        q = jnp.concatenate(
            [jnp.concatenate([jnp.where(masks[hh], qn, jnp.zeros_like(qn)),
                              jnp.where(((lane & 127) >> 4) == 4 * g + hh, qr, jnp.zeros_like(qr))], axis=-1)
             for hh in range(4)], axis=0)

        s_d = jnp.where(dmask, _dot_nt(q, jnp.concatenate([knd_ref[0, :, gs], krd_ref[0]], axis=-1)), -1e30)
        if tq % 128 == 0:
            mx = _fold(jnp.maximum, s_d)
        else:
            mx = jnp.broadcast_to(jnp.max(s_d, axis=-1, keepdims=True), (rows, 128))

        def scores(j, mx):
            ks = pl.ds(pl.multiple_of(j * tk, tk), tk)
            s = _dot_nt(q, jnp.concatenate([knp_ref[0, ks, gs], krp_ref[0, ks, :]], axis=-1))
            s_scr[j] = s
            return jnp.maximum(mx, _fold(jnp.maximum, s))

        mx = _fori_pairs(nfull, scores, mx)
        m = jnp.max(mx, axis=-1, keepdims=True)
        m_rep = jnp.broadcast_to(m, (rows, 128))

        p_d = jnp.exp2((s_d - m) * c)
        if tq % 128 == 0:
            lsum = _fold(jnp.add, p_d)
        else:
            lsum = jnp.broadcast_to(jnp.sum(p_d, axis=-1, keepdims=True), (rows, 128)) * (1.0 / 128)
        acc = _dot(p_d.astype(BF16), vd_ref[0, :, gs])

        def values(j, carry):
            lsum, acc = carry
            ks = pl.ds(pl.multiple_of(j * tk, tk), tk)
            s = s_scr[j]
            ps = [jnp.exp2((s[:, i:i + 128] - m_rep) * c) for i in range(0, tk, 128)]
            for p in ps:
                lsum = lsum + p
            p = jnp.concatenate(ps, axis=-1).astype(BF16)
            return lsum, acc + _dot(p, vp_ref[0, ks, gs])

        lsum, acc = _fori_pairs(nfull, values, (lsum, acc))
        o = acc * (1.0 / jnp.sum(lsum, axis=-1, keepdims=True))
        out_g = jnp.zeros((tq, HEAD_W), F32)
        for hh in range(4):
            out_g = jnp.where(masks[hh], o[hh * tq:(hh + 1) * tq], out_g)
        outs.append(out_g)
    o_ref[0] = jnp.concatenate(outs, axis=-1).astype(o_ref.dtype)


def _mla(qn, qr, past, diag, off):
    b, l, _ = qn.shape
    lp = past[0].shape[1]
    tq = min(l, ATT_TILE)
    tk = ATT_TILE
    qblk = lambda w: pl.BlockSpec((1, tq, w), lambda i, t: (i, t, 0))
    kblk = lambda w: pl.BlockSpec((1, lp, w), lambda i, t: (i, 0, 0))
    nmax = max((off + l - tq) // tk, 1)
    kern = functools.partial(_mla_kernel, tq=tq, tk=tk, off=off, scale=(MLA_NOPE + MLA_ROPE) ** -0.5)
    return pl.pallas_call(
        kern,
        out_shape=jax.ShapeDtypeStruct((b, l, 512), BF16),
        grid=(b, l // tq),
        in_specs=[qblk(512), qblk(256), kblk(512), kblk(512), kblk(256), qblk(512), qblk(512), qblk(256)],
        out_specs=qblk(512),
        scratch_shapes=[pltpu.VMEM((nmax, 4 * tq, tk), F32)],
        compiler_params=_params("parallel", "parallel"),
        name="mla_attn",
    )(qn, qr, *past, *diag)


def _neg_suffix_matrix(n):
    j = lax.broadcasted_iota(jnp.int32, (n, n), 0)
    s = lax.broadcasted_iota(jnp.int32, (n, n), 1)
    return jnp.where(j >= s, -1.0, 0.0).astype(BF16)


def _sb_kernel(q_ref, kp_ref, vp_ref, kd_ref, vd_ref, o_ref, kb_ref, vb_ref, *, tq, tk, off):
    qi = pl.program_id(1)

    @pl.when(qi == 0)
    def _():
        kb_ref[...] = kp_ref[0].astype(BF16)
        vb_ref[...] = vp_ref[0].astype(BF16)

    nfull = (off + qi * tq) // tk
    lane = lax.broadcasted_iota(jnp.int32, (1, HEAD_W), 1)
    nh = SB_HEADS
    dmask = (lax.broadcasted_iota(jnp.int32, (1, tq), 1)
             < lax.broadcasted_iota(jnp.int32, (nh * tq, 1), 0) % tq)
    t_diag = _neg_suffix_matrix(tq)
    t_full = t_diag if tk == tq else _neg_suffix_matrix(tk)
    q = q_ref[0]

    def tile(qh, k, v, run, mask, tmat):
        zz = _dot_nt(qh, k)
        sp = jnp.maximum(zz, 0.0) + jnp.log(1.0 + jnp.exp(-jnp.abs(zz)))
        if mask is not None:
            sp = jnp.where(mask, sp, 0.0)
        hi = sp.astype(BF16)
        lo = (sp - hi.astype(F32)).astype(BF16)
        if sp.shape[1] % 128 == 0:
            incl = _dot(jnp.concatenate([hi, lo], axis=-1), jnp.concatenate([tmat, tmat], axis=0))
        else:
            incl = _dot(hi, tmat) + _dot(lo, tmat)
        w = jnp.exp(zz + incl + run)
        if mask is not None:
            w = jnp.where(mask, w, 0.0)
        return _dot(w.astype(BF16), v), run - jnp.sum(sp, axis=-1, keepdims=True)

    masks = [(lane >= 64 * h) & (lane < 64 * h + 64) for h in range(nh)]
    qs = jnp.concatenate([jnp.where(mk, q, jnp.zeros_like(q)) for mk in masks], axis=0)
    state = tile(qs, kd_ref[0].astype(BF16), vd_ref[0].astype(BF16), jnp.zeros((nh * tq, 1), F32), dmask, t_diag)

    def more(state):
        j, _, run = state
        return jnp.logical_and(j >= 0, jnp.max(run) > SB_LOG_ZERO)

    def body(state):
        j, acc, run = state
        ks = pl.ds(pl.multiple_of(j * tk, tk), tk)
        pv, run = tile(qs, kb_ref[ks, :], vb_ref[ks, :], run, None, t_full)
        return j - 1, acc + pv, run

    _, acc, _ = lax.while_loop(more, body, (nfull - 1,) + state)
    out = jnp.zeros((tq, HEAD_W), F32)
    for h in range(nh):
        out = jnp.where(masks[h], acc[h * tq:(h + 1) * tq], out)
    o_ref[0] = out.astype(o_ref.dtype)


def _sb(sq, past, diag, off):
    b, l, _ = sq.shape
    lp = past[0].shape[1]
    tq = min(l, ATT_TILE)
    qblk = pl.BlockSpec((1, tq, HEAD_W), lambda i, t: (i, t, 0))
    kblk = pl.BlockSpec((1, lp, HEAD_W), lambda i, t: (i, 0, 0))
    kern = functools.partial(_sb_kernel, tq=tq, tk=ATT_TILE, off=off)
    return pl.pallas_call(
        kern,
        out_shape=jax.ShapeDtypeStruct((b, l, HEAD_W), BF16),
        grid=(b, l // tq),
        in_specs=[qblk, kblk, kblk, qblk, qblk],
        out_specs=qblk,
        scratch_shapes=[pltpu.VMEM((lp, HEAD_W), BF16), pltpu.VMEM((lp, HEAD_W), BF16)],
        compiler_params=_params("parallel", "arbitrary"),
        name="sb_attn",
    )(sq, *past, *diag)


def _ret_kernel(q_ref, k_ref, v_ref, g_ref, s0_ref, gn_ref, dec_ref, qd_ref, kd_ref, sd_ref,
                o_ref, s_ref, *, c):
    n = q_ref.shape[1] // c
    lane = lax.broadcasted_iota(jnp.int32, (1, HEAD_W), 1)
    krow = lax.broadcasted_iota(jnp.int32, (HEAD_W, 1), 0)
    bd_mask = ((krow & 127) >> 5) == (lane >> 6)
    s_ref[0] = s0_ref[0]

    def step(i, carry):
        rs = pl.ds(pl.multiple_of(i * c, c), c)
        q = q_ref[0, rs, :]
        kf = k_ref[0, rs, :]
        k = kf.astype(BF16)
        v = v_ref[0, rs, :]
        state = s_ref[0]
        o = _dot(q, state.astype(BF16)) * qd_ref[...]
        for h in range(RET_HEADS):
            mk = ((lane & 127) >> 5) == h
            mv = (lane >> 6) == h
            att = _dot_nt(jnp.where(mk, q, jnp.zeros_like(q)), k) * dec_ref[h]
            o = o + jnp.where(mv, _dot(att.astype(BF16), v), 0.0)
        upd = _dot_tn((kf * kd_ref[...]).astype(BF16), v)
        s_ref[0] = sd_ref[...] * state + jnp.where(bd_mask, upd, 0.0)
        mu = jnp.zeros_like(o)
        for h in range(RET_HEADS):
            mv = (lane >> 6) == h
            mu = mu + jnp.where(mv, jnp.sum(jnp.where(mv, o, 0.0), axis=-1, keepdims=True), 0.0)
        dlt = o - mu * (1.0 / RET_DV)
        var = jnp.zeros_like(o)
        for h in range(RET_HEADS):
            mv = (lane >> 6) == h
            var = var + jnp.where(mv, jnp.sum(jnp.where(mv, dlt * dlt, 0.0), axis=-1, keepdims=True), 0.0)
        ro = dlt * lax.rsqrt(var * (1.0 / RET_DV) + EPS) * gn_ref[...]
        o_ref[0, rs, :] = (_silu(g_ref[0, rs, :]) * ro).astype(o_ref.dtype)
        return carry

    lax.fori_loop(0, n, step, 0)


def _ret_consts(c):
    lg = jnp.log(1.0 - 2.0 ** (-5.0 - jnp.arange(RET_HEADS, dtype=F32)))
    i = jnp.arange(c, dtype=F32)
    rel = i[:, None] - i[None, :]
    dec = jnp.where(rel >= 0, jnp.exp(lg[:, None, None] * jnp.maximum(rel, 0.0)), 0.0)
    v_head = jnp.arange(HEAD_W) // RET_DV
    k_head = (jnp.arange(HEAD_W) % 128) // (RET_DK // 2)
    qd = jnp.exp(lg[None, v_head] * (i[:, None] + 1.0))
    kd = jnp.exp(lg[None, k_head] * (c - 1.0 - i[:, None]))
    sd = jnp.exp(lg * c)[v_head][None, :]
    return dec, qd, kd, sd


def _ret(rq, rk, rv, rg, s0, gn):
    b, l, _ = rq.shape
    c = min(l, RET_CHUNK)
    dec, qd, kd, sd = _ret_consts(c)
    blk = pl.BlockSpec((1, l, HEAD_W), lambda i: (i, 0, 0))
    sblk = pl.BlockSpec((1, HEAD_W, HEAD_W), lambda i: (i, 0, 0))
    return pl.pallas_call(
        functools.partial(_ret_kernel, c=c),
        out_shape=[jax.ShapeDtypeStruct((b, l, HEAD_W), BF16), jax.ShapeDtypeStruct((b, HEAD_W, HEAD_W), F32)],
        grid=(b,),
        in_specs=[blk, blk, blk, blk, sblk, _const_spec(gn.shape), _const_spec(dec.shape), _const_spec(qd.shape),
                  _const_spec(kd.shape), _const_spec(sd.shape)],
        out_specs=[blk, sblk],
        compiler_params=_params("parallel"),
        name="retention",
    )(rq, rk, rv, rg, s0, gn, dec, qd, kd, sd)


def _oproj_kernel(mla_ref, ret_ref, sb_ref, x_ref, mod_ref, wo_ref, g2_ref, x1_ref, h2_ref):
    bb, tl, d = x_ref.shape
    rows = bb * tl
    cat = jnp.concatenate([mla_ref[...].reshape(rows, 512), ret_ref[...].reshape(rows, HEAD_W),
                           sb_ref[...].reshape(rows, HEAD_W)], axis=-1)
    mix = _dot(cat, wo_ref[...]).reshape(bb, tl, d)
    mod = mod_ref[...]
    x1 = x_ref[...] + mod[:, 2:3, :] * mix
    x1_ref[...] = x1
    h2_ref[...] = (_rms(x1) * g2_ref[...] * (1.0 + mod[:, 4:5, :]) + mod[:, 3:4, :]).astype(h2_ref.dtype)


def _oproj(mla, ret, sb, x, mod, wo, g2):
    b, l, d = x.shape
    tl = min(l, ROW_TILE)
    bb = ROW_TILE // tl
    tok = lambda w: pl.BlockSpec((bb, tl, w), lambda i, t: (i, t, 0))
    return pl.pallas_call(
        _oproj_kernel,
        out_shape=[jax.ShapeDtypeStruct((b, l, d), F32), jax.ShapeDtypeStruct((b, l, d), BF16)],
        grid=(b // bb, l // tl),
        in_specs=[tok(512), tok(HEAD_W), tok(HEAD_W), tok(d), pl.BlockSpec((bb, 6, d), lambda i, t: (i, 0, 0)),
                  _const_spec(wo.shape), _const_spec(g2.shape)],
        out_specs=[tok(d), tok(d)],
        compiler_params=_params("parallel", "parallel"),
        name="out_proj",
    )(mla, ret, sb, x, mod, wo, g2)


def _ffn_kernel(h_ref, x_ref, mod_ref, prev_ref, wup_ref, cw_ref, cb_ref, wdn_ref, gf_ref,
                o_ref, st_ref, a_scr, carry, *, final_norm):
    bb, tl, d = x_ref.shape
    rows = bb * tl
    t = pl.program_id(1)
    nt = pl.num_programs(1)

    @pl.when(t == 0)
    def _():
        carry[:, 6:8, :] = prev_ref[...]

    h = h_ref[...].reshape(rows, d)
    f = jnp.zeros((rows, d), F32)
    for c0 in range(0, D_FF, FF_CHUNK):
        cs = slice(c0, c0 + FF_CHUNK)
        a = _dot(h, wup_ref[:, cs]).reshape(bb, tl, FF_CHUNK)
        b = _dot(h, wup_ref[:, D_FF + c0:D_FF + c0 + FF_CHUNK])
        a_scr[:, 8:, :] = a
        a_scr[:, 6:8, :] = carry[:, 6:8, cs]
        cw = cw_ref[...]
        conv = (cb_ref[:, cs] + cw[0:1, cs] * a_scr[:, 6:6 + tl, :] + cw[1:2, cs] * a_scr[:, 7:7 + tl, :]
                + cw[2:3, cs] * a)
        carry[:, 6:8, cs] = a_scr[:, tl + 6:tl + 8, :]
        y = (_silu(conv).reshape(rows, FF_CHUNK) * b).astype(BF16)
        f = f + _dot(y, wdn_ref[cs, :])
    x2 = x_ref[...] + mod_ref[...][:, 5:6, :] * f.reshape(bb, tl, d)
    if final_norm:
        x2 = _rms(x2) * gf_ref[...]
    o_ref[...] = x2

    @pl.when(t == nt - 1)
    def _():
        st_ref[...] = carry[:, 6:8, :]


def _ffn(h2, x1, mod, prev, wup, cw, cb, wdn, gf, final_norm):
    b, l, d = x1.shape
    tl = min(l, ROW_TILE)
    bb = ROW_TILE // tl
    tok = lambda: pl.BlockSpec((bb, tl, d), lambda i, t: (i, t, 0))
    st = pl.BlockSpec((bb, CONV_W - 1, D_FF), lambda i, t: (i, 0, 0))
    once = dict(pipeline_mode=pl.Buffered(1))
    return pl.pallas_call(
        functools.partial(_ffn_kernel, final_norm=final_norm),
        out_shape=[jax.ShapeDtypeStruct((b, l, d), F32), jax.ShapeDtypeStruct((b, CONV_W - 1, D_FF), F32)],
        grid=(b // bb, l // tl),
        in_specs=[tok(), tok(), pl.BlockSpec((bb, 6, d), lambda i, t: (i, 0, 0)), st,
                  pl.BlockSpec(wup.shape, lambda i, t: (0, 0), **once), _const_spec(cw.shape),
                  _const_spec(cb.shape), pl.BlockSpec(wdn.shape, lambda i, t: (0, 0), **once),
                  _const_spec(gf.shape)],
        out_specs=[tok(), st],
        scratch_shapes=[pltpu.VMEM((bb, tl + 8, FF_CHUNK), F32), pltpu.VMEM((bb, 8, D_FF), F32)],
        compiler_params=_params("parallel", "arbitrary"),
        name="conv_ffn",
    )(h2, x1, mod, prev, wup, cw, cb, wdn, gf)


def _half_split(w, heads):
    lead = w.shape[:-1]
    dim = w.shape[-1] // heads
    w = w.reshape(lead + (heads, 2, dim // 2))
    return jnp.swapaxes(w, -3, -2).reshape(lead + (heads * dim,))


def _permute_win(w):
    half = MLA_ROPE // 2
    tiled = lambda a: jnp.tile(a, (1, 1, MLA_HEADS))
    return jnp.concatenate([w[..., 0:640], _half_split(w[..., 672:928], RET_HEADS),
                            _half_split(w[..., 928:1184], RET_HEADS), w[..., 1184:2464],
                            tiled(w[..., 640:640 + half]), tiled(w[..., 640 + half:672])], axis=-1)


def _permute_wuq(w):
    half = MLA_ROPE // 2
    w = w.reshape(w.shape[:-1] + (MLA_HEADS, MLA_NOPE + MLA_ROPE))
    flat = lambda a: a.reshape(a.shape[:-2] + (-1,))
    return jnp.concatenate([flat(w[..., :MLA_NOPE]), flat(w[..., MLA_NOPE:MLA_NOPE + half]),
                            flat(w[..., MLA_NOPE + half:])], axis=-1)


def _permute_wukv(w):
    w = w.reshape(w.shape[:-1] + (MLA_HEADS, MLA_NOPE + MLA_V))
    flat = lambda a: a.reshape(a.shape[:-2] + (-1,))
    return jnp.concatenate([flat(w[..., :MLA_NOPE]), flat(w[..., MLA_NOPE:])], axis=-1)


def _ret_state_in(st):
    b = st.shape[0]
    st = st.reshape(b, RET_HEADS, 2, RET_DK // 2, RET_DV).transpose(0, 2, 1, 3, 4)
    eye = jnp.eye(RET_HEADS, dtype=st.dtype)
    full = st[:, :, :, :, None, :] * eye[None, None, :, None, :, None]
    return full.reshape(b, HEAD_W, HEAD_W)


def _ret_state_out(s):
    b = s.shape[0]
    s = s.reshape(b, 2, RET_HEADS, RET_DK // 2, RET_HEADS, RET_DV)
    blocks = [s[:, :, h, :, h, :].reshape(b, RET_DK, RET_DV) for h in range(RET_HEADS)]
    return jnp.stack(blocks, axis=1)


def _rope_tables(pos, dim, reps):
    inv = ROPE_BASE ** (-jnp.arange(0, dim, 2, dtype=F32) / dim)
    ang = pos.astype(F32)[:, None] * inv[None, :]
    return jnp.tile(jnp.cos(ang), (1, reps)), jnp.tile(jnp.sin(ang), (1, reps))


def _tile_krope(kr):
    half = MLA_ROPE // 2
    return jnp.concatenate([jnp.tile(kr[..., :half], (1, 1, MLA_HEADS)), jnp.tile(kr[..., half:], (1, 1, MLA_HEADS))],
                           axis=-1)


def _trunk(x, mods, pos0, cache, w):
    b, l, _ = x.shape
    depth = len(mods)
    pos = pos0 + jnp.arange(l)
    cm, sm = _rope_tables(pos, MLA_ROPE, MLA_HEADS)
    cr, sr = _rope_tables(pos, RET_DK, RET_HEADS)
    new = [[] for _ in range(6)]
    for layer in range(depth):
        mod = mods[layer]
        (qn, qr, lat, krt, kr, rq, rk, rv, rg, sq, sk, sv) = _inproj(
            x, mod, w["g_norm1"][layer], w["w_in"][layer], w["g_q_norm"][layer], w["w_uq"][layer],
            w["g_kv_norm"][layer], cm, sm, cr, sr)
        kvup = lambda a: tuple(_kvup(a, w["w_ukv"][layer]))
        if cache is None:
            off = 0
            mla_past = mla_diag = kvup(lat) + (krt,)
            sb_past = sb_diag = (sk, sv)
            s0 = jnp.zeros((b, HEAD_W, HEAD_W), F32)
            prev = jnp.zeros((b, CONV_W - 1, D_FF), F32)
        else:
            off = cache["lat"].shape[2]
            mla_past = kvup(cache["lat"][layer]) + (_tile_krope(cache["kr"][layer]).astype(BF16),)
            mla_diag = kvup(lat) + (krt,)
            sb_past = (cache["sk"][layer].reshape(b, off, HEAD_W), cache["sv"][layer].reshape(b, off, HEAD_W))
            sb_diag = (sk, sv)
            s0 = _ret_state_in(cache["S"][layer])
            prev = cache["conv"][layer]
        mla = _mla(qn, qr, mla_past, mla_diag, off)
        sb = _sb(sq, sb_past, sb_diag, off)
        ret, s_new = _ret(rq, rk, rv, rg, s0, w["g_ret_norm"][layer])
        x1, h2 = _oproj(mla, ret, sb, x, mod, w["w_o"][layer], w["g_norm2"][layer])
        x, conv_state = _ffn(h2, x1, mod, prev, w["w_up"][layer], w["conv_w"][layer], w["conv_b"][layer],
                             w["w_down"][layer], w["g_final"], layer == depth - 1)
        for lst, val in zip(new, (lat, kr, sk.reshape(b, l, SB_HEADS, SB_DK), sv.reshape(b, l, SB_HEADS, SB_DV),
                                  _ret_state_out(s_new), conv_state)):
            lst.append(val)
    return x, [jnp.stack(s, axis=0) for s in new]


def kernel(x_prompt, x_sample, c_prompt, c_sample, cache_mla_latent, cache_mla_krope, cache_sb_k, cache_sb_v, state_ret, state_ffn_conv, w_in, g_q_norm, w_uq, g_kv_norm, w_ukv, g_ret_norm, w_o, w_up, conv_w, conv_b, w_down, g_norm1, g_norm2, w_ada, b_ada, g_final):
    depth = w_in.shape[0]
    bp = x_prompt.shape[0]
    row = lambda g: g.reshape(g.shape[0], 1, g.shape[-1])
    w = dict(
        w_in=_permute_win(w_in).astype(BF16),
        w_uq=_permute_wuq(w_uq).astype(BF16),
        w_ukv=_permute_wukv(w_ukv).astype(BF16),
        w_o=w_o.astype(BF16), w_up=w_up.astype(BF16), w_down=w_down.astype(BF16),
        g_q_norm=row(g_q_norm), g_kv_norm=row(g_kv_norm), g_ret_norm=row(g_ret_norm),
        g_norm1=row(g_norm1), g_norm2=row(g_norm2), conv_w=conv_w, conv_b=row(conv_b),
        g_final=g_final.reshape(1, -1))
    mod = _ada(jnp.concatenate([c_prompt, c_sample], axis=0), w_ada, b_ada)
    mod = mod.reshape(depth, mod.shape[1], 6, D_MODEL)
    y_p, st_p = _trunk(x_prompt, [mod[l, :bp] for l in range(depth)], 0, None, w)
    cache = dict(lat=cache_mla_latent, kr=cache_mla_krope, sk=cache_sb_k, sv=cache_sb_v, S=state_ret,
                 conv=state_ffn_conv)
    y_s, st_s = _trunk(x_sample, [mod[l, bp:] for l in range(depth)], cache_mla_latent.shape[2], cache, w)
    return (y_p, y_s, *st_p, *st_s)
```

```python
import functools

import numpy as np
import jax
import jax.numpy as jnp
from jax import lax
from jax.experimental import pallas as pl
from jax.experimental.pallas import tpu as pltpu

D_MODEL = 1024
CHUNK = 64
MLA_HEADS = 8
MLA_NOPE = 64
MLA_ROPE = 32
MLA_V = 64
MLA_Q_RANK = 384
MLA_KV_RANK = 256
RET_HEADS = 4
RET_DK = 64
RET_DV = 64
SB_HEADS = 4
SB_DK = 64
SB_DV = 64
D_FF = 2816
CONV_W = 3
ROPE_BASE = 10000.0
EPS = 1e-6

HEAD_W = 256
ROW_TILE = 512
ATT_TILE = 256
RET_CHUNK = 256
FF_CHUNK = 1408
SB_LOG_ZERO = -104.0
VMEM_LIMIT = 56 * 1024 * 1024

F32 = jnp.float32
BF16 = jnp.bfloat16

_ZQ, _ZKV, _RQ, _RK, _RV, _RG, _SQ, _SK, _SV, _KRT, _IN_COLS_P = (
    0, 384, 640, 896, 1152, 1408, 1664, 1920, 2176, 2432, 2688)


def _dot(a, b):
    return jnp.dot(a, b, preferred_element_type=F32)


def _dot_nt(a, b):
    return lax.dot_general(a, b, (((1,), (1,)), ((), ())), preferred_element_type=F32)


def _dot_tn(a, b):
    return lax.dot_general(a, b, (((0,), (0,)), ((), ())), preferred_element_type=F32)


def _rms(x):
    return x * lax.rsqrt(jnp.mean(x * x, axis=-1, keepdims=True) + EPS)


def _silu(x):
    return x / (1.0 + jnp.exp(-x))


def _params(*sem):
    return pltpu.CompilerParams(dimension_semantics=sem, vmem_limit_bytes=VMEM_LIMIT)


def _const_spec(shape):
    nd = len(shape)
    return pl.BlockSpec(shape, lambda *_: (0,) * nd)


def _ada_kernel(c_ref, w_ref, b_ref, o_ref):
    a = _silu(c_ref[...]).astype(BF16)
    o_ref[0] = _dot(a, w_ref[0].astype(BF16)) + b_ref[0]


def _ada(c_all, w_ada, b_ada):
    depth, d, n = w_ada.shape
    rows = c_all.shape[0]
    tn = 1536
    return pl.pallas_call(
        _ada_kernel,
        out_shape=jax.ShapeDtypeStruct((depth, rows, n), F32),
        grid=(depth, n // tn),
        in_specs=[pl.BlockSpec((rows, d), lambda l, j: (0, 0)),
                  pl.BlockSpec((1, d, tn), lambda l, j: (l, 0, j)),
                  pl.BlockSpec((1, 1, tn), lambda l, j: (l, 0, j))],
        out_specs=pl.BlockSpec((1, rows, tn), lambda l, j: (l, 0, j)),
        compiler_params=_params("parallel", "parallel"),
        name="ada_mod",
    )(c_all, w_ada, b_ada.reshape(depth, 1, n))


def _inproj_kernel(x_ref, mod_ref, g1_ref, win_ref, gq_ref, wuq_ref, gkv_ref, cm_ref, sm_ref, cr_ref, sr_ref,
                   qn_ref, qr_ref, lat_ref, krt_ref, kr_ref, rq_ref, rk_ref, rv_ref, rg_ref, sq_ref, sk_ref,
                   sv_ref):
    bb, tl, d = x_ref.shape
    rows = bb * tl
    mod = mod_ref[...]
    h = _rms(x_ref[...]) * g1_ref[...] * (1.0 + mod[:, 1:2, :]) + mod[:, 0:1, :]
    z = _dot(h.reshape(rows, d).astype(BF16), win_ref[...])

    def put(ref, val):
        ref[...] = val.reshape(bb, tl, val.shape[-1]).astype(ref.dtype)

    def rope(x1, x2, c_ref, s_ref):
        c, s = c_ref[...][None], s_ref[...][None]
        x1 = x1.reshape(bb, tl, x1.shape[-1])
        x2 = x2.reshape(bb, tl, x2.shape[-1])
        return x1 * c - x2 * s, x1 * s + x2 * c

    q = _dot((_rms(z[:, _ZQ:_ZKV]) * gq_ref[...]).astype(BF16), wuq_ref[...])
    put(qn_ref, q[:, :512])
    q1, q2 = rope(q[:, 512:640], q[:, 640:768], cm_ref, sm_ref)
    qr_ref[:, :, 0:128] = q1.astype(qr_ref.dtype)
    qr_ref[:, :, 128:256] = q2.astype(qr_ref.dtype)
    put(lat_ref, _rms(z[:, _ZKV:_RQ]) * gkv_ref[...])
    k1, k2 = rope(z[:, _KRT:_KRT + 128], z[:, _KRT + 128:_KRT + 256], cm_ref, sm_ref)
    krt_ref[:, :, 0:128] = k1.astype(krt_ref.dtype)
    krt_ref[:, :, 128:256] = k2.astype(krt_ref.dtype)
    lane = lax.broadcasted_iota(jnp.int32, (1, 1, MLA_ROPE), 2)
    kr_ref[...] = jnp.where(lane < MLA_ROPE // 2, k1[:, :, :MLA_ROPE], k2[:, :, :MLA_ROPE])
    a1, a2 = rope(z[:, _RQ:_RQ + 128], z[:, _RQ + 128:_RK], cr_ref, sr_ref)
    rq_ref[:, :, 0:128] = (a1 * (RET_DK ** -0.5)).astype(rq_ref.dtype)
    rq_ref[:, :, 128:256] = (a2 * (RET_DK ** -0.5)).astype(rq_ref.dtype)
    b1, b2 = rope(z[:, _RK:_RK + 128], z[:, _RK + 128:_RV], cr_ref, sr_ref)
    rk_ref[:, :, 0:128] = b1
    rk_ref[:, :, 128:256] = b2
    put(rv_ref, z[:, _RV:_RG])
    put(rg_ref, z[:, _RG:_SQ])
    put(sq_ref, z[:, _SQ:_SK] * (SB_DK ** -0.5))
    put(sk_ref, z[:, _SK:_SV])
    put(sv_ref, z[:, _SV:_KRT])


def _inproj(x, mod, g1, win, gq, wuq, gkv, cm, sm, cr, sr):
    b, l, d = x.shape
    tl = min(l, ROW_TILE)
    bb = ROW_TILE // tl
    grid = (b // bb, l // tl)
    tok = lambda w: pl.BlockSpec((bb, tl, w), lambda i, t: (i, t, 0))
    pos = pl.BlockSpec((tl, 128), lambda i, t: (t, 0))
    widths = [(512, BF16), (256, BF16), (256, F32), (256, BF16), (MLA_ROPE, F32), (256, BF16), (256, F32),
              (256, BF16), (256, F32), (256, BF16), (256, F32), (256, F32)]
    return pl.pallas_call(
        _inproj_kernel,
        out_shape=[jax.ShapeDtypeStruct((b, l, w), dt) for w, dt in widths],
        grid=grid,
        in_specs=[tok(d), pl.BlockSpec((bb, 6, d), lambda i, t: (i, 0, 0)), _const_spec(g1.shape),
                  _const_spec(win.shape), _const_spec(gq.shape), _const_spec(wuq.shape), _const_spec(gkv.shape),
                  pos, pos, pos, pos],
        out_specs=[tok(w) for w, _ in widths],
        compiler_params=_params("parallel", "parallel"),
        name="in_proj",
    )(x, mod, g1, win, gq, wuq, gkv, cm, sm, cr, sr)


def _kvup_kernel(lat_ref, w_ref, kn_ref, v_ref):
    kv = _dot(lat_ref[0].astype(BF16), w_ref[...])
    kn_ref[0] = kv[:, :512].astype(kn_ref.dtype)
    v_ref[0] = kv[:, 512:].astype(v_ref.dtype)


def _kvup(lat_all, wukv):
    b, lk, r = lat_all.shape
    blk = lambda w: pl.BlockSpec((1, lk, w), lambda i: (i, 0, 0))
    return pl.pallas_call(
        _kvup_kernel,
        out_shape=[jax.ShapeDtypeStruct((b, lk, 512), BF16)] * 2,
        grid=(b,),
        in_specs=[blk(r), _const_spec(wukv.shape)],
        out_specs=[blk(512), blk(512)],
        compiler_params=_params("parallel"),
        name="kv_up",
    )(lat_all, wukv)


def _fori_pairs(n, body, init):
    carry = lax.fori_loop(0, n // 2, lambda i, c: body(2 * i + 1, body(2 * i, c)), init)
    return lax.fori_loop(0, n % 2, lambda i, c: body(n - 1, c), carry)


def _fold(op, x):
    w = x.shape[1]
    parts = [x[:, i:i + 128] for i in range(0, w, 128)]
    out = parts[0]
    for p in parts[1:]:
        out = op(out, p)
    return out


def _mla_kernel(qn_ref, qr_ref, knp_ref, vp_ref, krp_ref, knd_ref, vd_ref, krd_ref, o_ref, s_scr,
                *, tq, tk, off, scale):
    qi = pl.program_id(1)
    nfull = (off + qi * tq) // tk
    rows = 4 * tq
    lane = lax.broadcasted_iota(jnp.int32, (1, HEAD_W), 1)
    row_c = (lax.broadcasted_iota(jnp.int32, (rows, 1), 0) % tq) // CHUNK
    col_c = lax.broadcasted_iota(jnp.int32, (1, tq), 1) // CHUNK
    dmask = col_c <= row_c
    qr = qr_ref[0]
    masks = [(lane >= 64 * hh) & (lane < 64 * hh + 64) for hh in range(4)]
    c = scale * 1.4426950408889634
    outs = []
    for g in range(MLA_HEADS // 4):
        gs = slice(HEAD_W * g, HEAD_W * (g + 1))
        qn = qn_ref[0, :, gs]
        q = jnp.concatenate(
            [jnp.concatenate([jnp.where(masks[hh], qn, jnp.zeros_like(qn)),
                              jnp.where(((lane & 127) >> 4) == 4 * g + hh, qr, jnp.zeros_like(qr))], axis=-1)
             for hh in range(4)], axis=0)

        s_d = jnp.where(dmask, _dot_nt(q, jnp.concatenate([knd_ref[0, :, gs], krd_ref[0]], axis=-1)), -1e30)
        if tq % 128 == 0:
            mx = _fold(jnp.maximum, s_d)
        else:
            mx = jnp.broadcast_to(jnp.max(s_d, axis=-1, keepdims=True), (rows, 128))

        def scores(j, mx):
            ks = pl.ds(pl.multiple_of(j * tk, tk), tk)
            s = _dot_nt(q, jnp.concatenate([knp_ref[0, ks, gs], krp_ref[0, ks, :]], axis=-1))
            s_scr[j] = s
            return jnp.maximum(mx, _fold(jnp.maximum, s))

        mx = _fori_pairs(nfull, scores, mx)
        m = jnp.max(mx, axis=-1, keepdims=True)
        m_rep = jnp.broadcast_to(m, (rows, 128))

        p_d = jnp.exp2((s_d - m) * c)
        if tq % 128 == 0:
            lsum = _fold(jnp.add, p_d)
        else:
            lsum = jnp.broadcast_to(jnp.sum(p_d, axis=-1, keepdims=True), (rows, 128)) * (1.0 / 128)
        acc = _dot(p_d.astype(BF16), vd_ref[0, :, gs])

        def values(j, carry):
            lsum, acc = carry
            ks = pl.ds(pl.multiple_of(j * tk, tk), tk)
            s = s_scr[j]
            ps = [jnp.exp2((s[:, i:i + 128] - m_rep) * c) for i in range(0, tk, 128)]
            for p in ps:
                lsum = lsum + p
            p = jnp.concatenate(ps, axis=-1).astype(BF16)
            return lsum, acc + _dot(p, vp_ref[0, ks, gs])

        lsum, acc = _fori_pairs(nfull, values, (lsum, acc))
        o = acc * (1.0 / jnp.sum(lsum, axis=-1, keepdims=True))
        out_g = jnp.zeros((tq, HEAD_W), F32)
        for hh in range(4):
            out_g = jnp.where(masks[hh], o[hh * tq:(hh + 1) * tq], out_g)
        outs.append(out_g)
    o_ref[0] = jnp.concatenate(outs, axis=-1).astype(o_ref.dtype)


def _mla(qn, qr, past, diag, off):
    b, l, _ = qn.shape
    lp = past[0].shape[1]
    tq = min(l, ATT_TILE)
    tk = ATT_TILE
    qblk = lambda w: pl.BlockSpec((1, tq, w), lambda i, t: (i, t, 0))
    kblk = lambda w: pl.BlockSpec((1, lp, w), lambda i, t: (i, 0, 0))
    nmax = max((off + l - tq) // tk, 1)
    kern = functools.partial(_mla_kernel, tq=tq, tk=tk, off=off, scale=(MLA_NOPE + MLA_ROPE) ** -0.5)
    return pl.pallas_call(
        kern,
        out_shape=jax.ShapeDtypeStruct((b, l, 512), BF16),
        grid=(b, l // tq),
        in_specs=[qblk(512), qblk(256), kblk(512), kblk(512), kblk(256), qblk(512), qblk(512), qblk(256)],
        out_specs=qblk(512),
        scratch_shapes=[pltpu.VMEM((nmax, 4 * tq, tk), F32)],
        compiler_params=_params("parallel", "parallel"),
        name="mla_attn",
    )(qn, qr, *past, *diag)


def _neg_suffix_matrix(n):
    j = lax.broadcasted_iota(jnp.int32, (n, n), 0)
    s = lax.broadcasted_iota(jnp.int32, (n, n), 1)
    return jnp.where(j >= s, -1.0, 0.0).astype(BF16)


def _sb_kernel(q_ref, kp_ref, vp_ref, kd_ref, vd_ref, o_ref, kb_ref, vb_ref, *, tq, tk, off):
    qi = pl.program_id(1)

    @pl.when(qi == 0)
    def _():
        kb_ref[...] = kp_ref[0].astype(BF16)
        vb_ref[...] = vp_ref[0].astype(BF16)

    nfull = (off + qi * tq) // tk
    lane = lax.broadcasted_iota(jnp.int32, (1, HEAD_W), 1)
    nh = SB_HEADS
    dmask = (lax.broadcasted_iota(jnp.int32, (1, tq), 1)
             < lax.broadcasted_iota(jnp.int32, (nh * tq, 1), 0) % tq)
    t_diag = _neg_suffix_matrix(tq)
    t_full = t_diag if tk == tq else _neg_suffix_matrix(tk)
    q = q_ref[0]

    def tile(qh, k, v, run, mask, tmat):
        zz = _dot_nt(qh, k)
        sp = jnp.maximum(zz, 0.0) + jnp.log(1.0 + jnp.exp(-jnp.abs(zz)))
        if mask is not None:
            sp = jnp.where(mask, sp, 0.0)
        hi = sp.astype(BF16)
        lo = (sp - hi.astype(F32)).astype(BF16)
        if sp.shape[1] % 128 == 0:
            incl = _dot(jnp.concatenate([hi, lo], axis=-1), jnp.concatenate([tmat, tmat], axis=0))
        else:
            incl = _dot(hi, tmat) + _dot(lo, tmat)
        w = jnp.exp(zz + incl + run)
        if mask is not None:
            w = jnp.where(mask, w, 0.0)
        return _dot(w.astype(BF16), v), run - jnp.sum(sp, axis=-1, keepdims=True)

    masks = [(lane >= 64 * h) & (lane < 64 * h + 64) for h in range(nh)]
    qs = jnp.concatenate([jnp.where(mk, q, jnp.zeros_like(q)) for mk in masks], axis=0)
    state = tile(qs, kd_ref[0].astype(BF16), vd_ref[0].astype(BF16), jnp.zeros((nh * tq, 1), F32), dmask, t_diag)

    def more(state):
        j, _, run = state
        return jnp.logical_and(j >= 0, jnp.max(run) > SB_LOG_ZERO)

    def body(state):
        j, acc, run = state
        ks = pl.ds(pl.multiple_of(j * tk, tk), tk)
        pv, run = tile(qs, kb_ref[ks, :], vb_ref[ks, :], run, None, t_full)
        return j - 1, acc + pv, run

    _, acc, _ = lax.while_loop(more, body, (nfull - 1,) + state)
    out = jnp.zeros((tq, HEAD_W), F32)
    for h in range(nh):
        out = jnp.where(masks[h], acc[h * tq:(h + 1) * tq], out)
    o_ref[0] = out.astype(o_ref.dtype)


def _sb(sq, past, diag, off):
    b, l, _ = sq.shape
    lp = past[0].shape[1]
    tq = min(l, ATT_TILE)
    qblk = pl.BlockSpec((1, tq, HEAD_W), lambda i, t: (i, t, 0))
    kblk = pl.BlockSpec((1, lp, HEAD_W), lambda i, t: (i, 0, 0))
    kern = functools.partial(_sb_kernel, tq=tq, tk=ATT_TILE, off=off)
    return pl.pallas_call(
        kern,
        out_shape=jax.ShapeDtypeStruct((b, l, HEAD_W), BF16),
        grid=(b, l // tq),
        in_specs=[qblk, kblk, kblk, qblk, qblk],
        out_specs=qblk,
        scratch_shapes=[pltpu.VMEM((lp, HEAD_W), BF16), pltpu.VMEM((lp, HEAD_W), BF16)],
        compiler_params=_params("parallel", "arbitrary"),
        name="sb_attn",
    )(sq, *past, *diag)


def _ret_kernel(q_ref, k_ref, v_ref, g_ref, s0_ref, gn_ref, dec_ref, qd_ref, kd_ref, sd_ref,
                o_ref, s_ref, *, c):
    n = q_ref.shape[1] // c
    lane = lax.broadcasted_iota(jnp.int32, (1, HEAD_W), 1)
    krow = lax.broadcasted_iota(jnp.int32, (HEAD_W, 1), 0)
    bd_mask = ((krow & 127) >> 5) == (lane >> 6)
    s_ref[0] = s0_ref[0]

    def step(i, carry):
        rs = pl.ds(pl.multiple_of(i * c, c), c)
        q = q_ref[0, rs, :]
        kf = k_ref[0, rs, :]
        k = kf.astype(BF16)
        v = v_ref[0, rs, :]
        state = s_ref[0]
        o = _dot(q, state.astype(BF16)) * qd_ref[...]
        for h in range(RET_HEADS):
            mk = ((lane & 127) >> 5) == h
            mv = (lane >> 6) == h
            att = _dot_nt(jnp.where(mk, q, jnp.zeros_like(q)), k) * dec_ref[h]
            o = o + jnp.where(mv, _dot(att.astype(BF16), v), 0.0)
        upd = _dot_tn((kf * kd_ref[...]).astype(BF16), v)
        s_ref[0] = sd_ref[...] * state + jnp.where(bd_mask, upd, 0.0)
        mu = jnp.zeros_like(o)
        for h in range(RET_HEADS):
            mv = (lane >> 6) == h
            mu = mu + jnp.where(mv, jnp.sum(jnp.where(mv, o, 0.0), axis=-1, keepdims=True), 0.0)
        dlt = o - mu * (1.0 / RET_DV)
        var = jnp.zeros_like(o)
        for h in range(RET_HEADS):
            mv = (lane >> 6) == h
            var = var + jnp.where(mv, jnp.sum(jnp.where(mv, dlt * dlt, 0.0), axis=-1, keepdims=True), 0.0)
        ro = dlt * lax.rsqrt(var * (1.0 / RET_DV) + EPS) * gn_ref[...]
        o_ref[0, rs, :] = (_silu(g_ref[0, rs, :]) * ro).astype(o_ref.dtype)
        return carry

    lax.fori_loop(0, n, step, 0)


def _ret_consts(c):
    lg = jnp.log(1.0 - 2.0 ** (-5.0 - jnp.arange(RET_HEADS, dtype=F32)))
    i = jnp.arange(c, dtype=F32)
    rel = i[:, None] - i[None, :]
    dec = jnp.where(rel >= 0, jnp.exp(lg[:, None, None] * jnp.maximum(rel, 0.0)), 0.0)
    v_head = jnp.arange(HEAD_W) // RET_DV
    k_head = (jnp.arange(HEAD_W) % 128) // (RET_DK // 2)
    qd = jnp.exp(lg[None, v_head] * (i[:, None] + 1.0))
    kd = jnp.exp(lg[None, k_head] * (c - 1.0 - i[:, None]))
    sd = jnp.exp(lg * c)[v_head][None, :]
    return dec, qd, kd, sd


def _ret(rq, rk, rv, rg, s0, gn):
    b, l, _ = rq.shape
    c = min(l, RET_CHUNK)
    dec, qd, kd, sd = _ret_consts(c)
    blk = pl.BlockSpec((1, l, HEAD_W), lambda i: (i, 0, 0))
    sblk = pl.BlockSpec((1, HEAD_W, HEAD_W), lambda i: (i, 0, 0))
    return pl.pallas_call(
        functools.partial(_ret_kernel, c=c),
        out_shape=[jax.ShapeDtypeStruct((b, l, HEAD_W), BF16), jax.ShapeDtypeStruct((b, HEAD_W, HEAD_W), F32)],
        grid=(b,),
        in_specs=[blk, blk, blk, blk, sblk, _const_spec(gn.shape), _const_spec(dec.shape), _const_spec(qd.shape),
                  _const_spec(kd.shape), _const_spec(sd.shape)],
        out_specs=[blk, sblk],
        compiler_params=_params("parallel"),
        name="retention",
    )(rq, rk, rv, rg, s0, gn, dec, qd, kd, sd)


def _ffn_kernel(mla_ref, ret_ref, sb_ref, x_ref, mod_ref, prev_ref, wo_ref, g2_ref, wup_ref, cw_ref, cb_ref,
                wdn_ref, gf_ref, o_ref, st_ref, a_scr, carry, *, final_norm):
    bb, tl, d = x_ref.shape
    rows = bb * tl
    t = pl.program_id(1)
    nt = pl.num_programs(1)

    @pl.when(t == 0)
    def _():
        carry[:, 6:8, :] = prev_ref[...]

    mod = mod_ref[...]
    cat = jnp.concatenate([mla_ref[...].reshape(rows, 512), ret_ref[...].reshape(rows, HEAD_W),
                           sb_ref[...].reshape(rows, HEAD_W)], axis=-1)
    x1 = x_ref[...] + mod[:, 2:3, :] * _dot(cat, wo_ref[...]).reshape(bb, tl, d)
    h = (_rms(x1) * g2_ref[...] * (1.0 + mod[:, 4:5, :]) + mod[:, 3:4, :]).reshape(rows, d).astype(BF16)
    f = jnp.zeros((rows, d), F32)
    for c0 in range(0, D_FF, FF_CHUNK):
        cs = slice(c0, c0 + FF_CHUNK)
        a = _dot(h, wup_ref[:, cs]).reshape(bb, tl, FF_CHUNK)
        b = _dot(h, wup_ref[:, D_FF + c0:D_FF + c0 + FF_CHUNK])
        a_scr[:, 8:, :] = a
        a_scr[:, 6:8, :] = carry[:, 6:8, cs]
        cw = cw_ref[...]
        conv = (cb_ref[:, cs] + cw[0:1, cs] * a_scr[:, 6:6 + tl, :] + cw[1:2, cs] * a_scr[:, 7:7 + tl, :]
                + cw[2:3, cs] * a)
        carry[:, 6:8, cs] = a_scr[:, tl + 6:tl + 8, :]
        y = (_silu(conv).reshape(rows, FF_CHUNK) * b).astype(BF16)
        f = f + _dot(y, wdn_ref[cs, :])
    x2 = x1 + mod[:, 5:6, :] * f.reshape(bb, tl, d)
    if final_norm:
        x2 = _rms(x2) * gf_ref[...]
    o_ref[...] = x2

    @pl.when(t == nt - 1)
    def _():
        st_ref[...] = carry[:, 6:8, :]


def _ffn(mla, ret, sb, x, mod, prev, wo, g2, wup, cw, cb, wdn, gf, final_norm):
    b, l, d = x.shape
    tl = min(l, ROW_TILE)
    bb = ROW_TILE // tl
    tok = lambda w: pl.BlockSpec((bb, tl, w), lambda i, t: (i, t, 0))
    st = pl.BlockSpec((bb, CONV_W - 1, D_FF), lambda i, t: (i, 0, 0))
    once = lambda a: pl.BlockSpec(a.shape, lambda i, t: (0, 0), pipeline_mode=pl.Buffered(1))
    return pl.pallas_call(
        functools.partial(_ffn_kernel, final_norm=final_norm),
        out_shape=[jax.ShapeDtypeStruct((b, l, d), F32), jax.ShapeDtypeStruct((b, CONV_W - 1, D_FF), F32)],
        grid=(b // bb, l // tl),
        in_specs=[tok(512), tok(HEAD_W), tok(HEAD_W), tok(d), pl.BlockSpec((bb, 6, d), lambda i, t: (i, 0, 0)), st,
                  once(wo), _const_spec(g2.shape), once(wup), _const_spec(cw.shape), _const_spec(cb.shape),
                  once(wdn), _const_spec(gf.shape)],
        out_specs=[tok(d), st],
        scratch_shapes=[pltpu.VMEM((bb, tl + 8, FF_CHUNK), F32), pltpu.VMEM((bb, 8, D_FF), F32)],
        compiler_params=_params("parallel", "arbitrary"),
        name="conv_ffn",
    )(mla, ret, sb, x, mod, prev, wo, g2, wup, cw, cb, wdn, gf)


def _half_split(w, heads):
    lead = w.shape[:-1]
    dim = w.shape[-1] // heads
    w = w.reshape(lead + (heads, 2, dim // 2))
    return jnp.swapaxes(w, -3, -2).reshape(lead + (heads * dim,))


def _permute_win(w):
    half = MLA_ROPE // 2
    tiled = lambda a: jnp.tile(a, (1, 1, MLA_HEADS))
    return jnp.concatenate([w[..., 0:640], _half_split(w[..., 672:928], RET_HEADS),
                            _half_split(w[..., 928:1184], RET_HEADS), w[..., 1184:2464],
                            tiled(w[..., 640:640 + half]), tiled(w[..., 640 + half:672])], axis=-1)


def _permute_wuq(w):
    half = MLA_ROPE // 2
    w = w.reshape(w.shape[:-1] + (MLA_HEADS, MLA_NOPE + MLA_ROPE))
    flat = lambda a: a.reshape(a.shape[:-2] + (-1,))
    return jnp.concatenate([flat(w[..., :MLA_NOPE]), flat(w[..., MLA_NOPE:MLA_NOPE + half]),
                            flat(w[..., MLA_NOPE + half:])], axis=-1)


def _permute_wukv(w):
    w = w.reshape(w.shape[:-1] + (MLA_HEADS, MLA_NOPE + MLA_V))
    flat = lambda a: a.reshape(a.shape[:-2] + (-1,))
    return jnp.concatenate([flat(w[..., :MLA_NOPE]), flat(w[..., MLA_NOPE:])], axis=-1)


def _ret_state_in(st):
    b = st.shape[0]
    st = st.reshape(b, RET_HEADS, 2, RET_DK // 2, RET_DV).transpose(0, 2, 1, 3, 4)
    eye = jnp.eye(RET_HEADS, dtype=st.dtype)
    full = st[:, :, :, :, None, :] * eye[None, None, :, None, :, None]
    return full.reshape(b, HEAD_W, HEAD_W)


def _ret_state_out(s):
    b = s.shape[0]
    s = s.reshape(b, 2, RET_HEADS, RET_DK // 2, RET_HEADS, RET_DV)
    blocks = [s[:, :, h, :, h, :].reshape(b, RET_DK, RET_DV) for h in range(RET_HEADS)]
    return jnp.stack(blocks, axis=1)


def _rope_tables(pos, dim, reps):
    inv = ROPE_BASE ** (-jnp.arange(0, dim, 2, dtype=F32) / dim)
    ang = pos.astype(F32)[:, None] * inv[None, :]
    return jnp.tile(jnp.cos(ang), (1, reps)), jnp.tile(jnp.sin(ang), (1, reps))


def _tile_krope(kr):
    half = MLA_ROPE // 2
    return jnp.concatenate([jnp.tile(kr[..., :half], (1, 1, MLA_HEADS)), jnp.tile(kr[..., half:], (1, 1, MLA_HEADS))],
                           axis=-1)


def _trunk(x, mods, pos0, cache, w):
    b, l, _ = x.shape
    depth = len(mods)
    pos = pos0 + jnp.arange(l)
    cm, sm = _rope_tables(pos, MLA_ROPE, MLA_HEADS)
    cr, sr = _rope_tables(pos, RET_DK, RET_HEADS)
    new = [[] for _ in range(6)]
    for layer in range(depth):
        mod = mods[layer]
        (qn, qr, lat, krt, kr, rq, rk, rv, rg, sq, sk, sv) = _inproj(
            x, mod, w["g_norm1"][layer], w["w_in"][layer], w["g_q_norm"][layer], w["w_uq"][layer],
            w["g_kv_norm"][layer], cm, sm, cr, sr)
        kvup = lambda a: tuple(_kvup(a, w["w_ukv"][layer]))
        if cache is None:
            off = 0
            mla_past = mla_diag = kvup(lat) + (krt,)
            sb_past = sb_diag = (sk, sv)
            s0 = jnp.zeros((b, HEAD_W, HEAD_W), F32)
            prev = jnp.zeros((b, CONV_W - 1, D_FF), F32)
        else:
            off = cache["lat"].shape[2]
            mla_past = kvup(cache["lat"][layer]) + (_tile_krope(cache["kr"][layer]).astype(BF16),)
            mla_diag = kvup(lat) + (krt,)
            sb_past = (cache["sk"][layer].reshape(b, off, HEAD_W), cache["sv"][layer].reshape(b, off, HEAD_W))
            sb_diag = (sk, sv)
            s0 = _ret_state_in(cache["S"][layer])
            prev = cache["conv"][layer]
        mla = _mla(qn, qr, mla_past, mla_diag, off)
        sb = _sb(sq, sb_past, sb_diag, off)
        ret, s_new = _ret(rq, rk, rv, rg, s0, w["g_ret_norm"][layer])
        x, conv_state = _ffn(mla, ret, sb, x, mod, prev, w["w_o"][layer], w["g_norm2"][layer], w["w_up"][layer],
                             w["conv_w"][layer], w["conv_b"][layer], w["w_down"][layer], w["g_final"],
                             layer == depth - 1)
        for lst, val in zip(new, (lat, kr, sk.reshape(b, l, SB_HEADS, SB_DK), sv.reshape(b, l, SB_HEADS, SB_DV),
                                  _ret_state_out(s_new), conv_state)):
            lst.append(val)
    return x, [jnp.stack(s, axis=0) for s in new]


def kernel(x_prompt, x_sample, c_prompt, c_sample, cache_mla_latent, cache_mla_krope, cache_sb_k, cache_sb_v, state_ret, state_ffn_conv, w_in, g_q_norm, w_uq, g_kv_norm, w_ukv, g_ret_norm, w_o, w_up, conv_w, conv_b, w_down, g_norm1, g_norm2, w_ada, b_ada, g_final):
    depth = w_in.shape[0]
    bp = x_prompt.shape[0]
    row = lambda g: g.reshape(g.shape[0], 1, g.shape[-1])
    w = dict(
        w_in=_permute_win(w_in).astype(BF16),
        w_uq=_permute_wuq(w_uq).astype(BF16),
        w_ukv=_permute_wukv(w_ukv).astype(BF16),
        w_o=w_o.astype(BF16), w_up=w_up.astype(BF16), w_down=w_down.astype(BF16),
        g_q_norm=row(g_q_norm), g_kv_norm=row(g_kv_norm), g_ret_norm=row(g_ret_norm),
        g_norm1=row(g_norm1), g_norm2=row(g_norm2), conv_w=conv_w, conv_b=row(conv_b),
        g_final=g_final.reshape(1, -1))
    mod = _ada(jnp.concatenate([c_prompt, c_sample], axis=0), w_ada, b_ada)
    mod = mod.reshape(depth, mod.shape[1], 6, D_MODEL)
    y_p, st_p = _trunk(x_prompt, [mod[l, :bp] for l in range(depth)], 0, None, w)
    cache = dict(lat=cache_mla_latent, kr=cache_mla_krope, sk=cache_sb_k, sv=cache_sb_v, S=state_ret,
                 conv=state_ffn_conv)
    y_s, st_s = _trunk(x_sample, [mod[l, bp:] for l in range(depth)], cache_mla_latent.shape[2], cache, w)
    return (y_p, y_s, *st_p, *st_s)
```

```python
import functools

import numpy as np
import jax
import jax.numpy as jnp
from jax import lax
from jax.experimental import pallas as pl
from jax.experimental.pallas import tpu as pltpu

D_MODEL = 1024
CHUNK = 64
MLA_HEADS = 8
MLA_NOPE = 64
MLA_ROPE = 32
MLA_V = 64
MLA_Q_RANK = 384
MLA_KV_RANK = 256
RET_HEADS = 4
RET_DK = 64
RET_DV = 64
SB_HEADS = 4
SB_DK = 64
SB_DV = 64
D_FF = 2816
CONV_W = 3
ROPE_BASE = 10000.0
EPS = 1e-6

HEAD_W = 256
ROW_TILE = 512
ATT_TILE = 256
RET_CHUNK = 256
FF_CHUNK = 1408
SB_LOG_ZERO = -104.0
VMEM_LIMIT = 56 * 1024 * 1024

F32 = jnp.float32
BF16 = jnp.bfloat16

_ZQ, _ZKV, _RQ, _RK, _RV, _RG, _SQ, _SK, _SV, _KRT, _IN_COLS_P = (
    0, 384, 640, 896, 1152, 1408, 1664, 1920, 2176, 2432, 2688)


def _dot(a, b):
    return jnp.dot(a, b, preferred_element_type=F32)


def _dot_nt(a, b):
    return lax.dot_general(a, b, (((1,), (1,)), ((), ())), preferred_element_type=F32)


def _dot_tn(a, b):
    return lax.dot_general(a, b, (((0,), (0,)), ((), ())), preferred_element_type=F32)


def _rms(x):
    return x * lax.rsqrt(jnp.mean(x * x, axis=-1, keepdims=True) + EPS)


def _silu(x):
    return x / (1.0 + jnp.exp(-x))


def _params(*sem):
    return pltpu.CompilerParams(dimension_semantics=sem, vmem_limit_bytes=VMEM_LIMIT)


def _const_spec(shape):
    nd = len(shape)
    return pl.BlockSpec(shape, lambda *_: (0,) * nd)


def _ada_kernel(c_ref, w_ref, b_ref, o_ref):
    a = _silu(c_ref[...]).astype(BF16)
    o_ref[0] = _dot(a, w_ref[0].astype(BF16)) + b_ref[0]


def _ada(c_all, w_ada, b_ada):
    depth, d, n = w_ada.shape
    rows = c_all.shape[0]
    tn = 1536
    return pl.pallas_call(
        _ada_kernel,
        out_shape=jax.ShapeDtypeStruct((depth, rows, n), F32),
        grid=(depth, n // tn),
        in_specs=[pl.BlockSpec((rows, d), lambda l, j: (0, 0)),
                  pl.BlockSpec((1, d, tn), lambda l, j: (l, 0, j)),
                  pl.BlockSpec((1, 1, tn), lambda l, j: (l, 0, j))],
        out_specs=pl.BlockSpec((1, rows, tn), lambda l, j: (l, 0, j)),
        compiler_params=_params("parallel", "parallel"),
        name="ada_mod",
    )(c_all, w_ada, b_ada.reshape(depth, 1, n))


def _inproj_kernel(x_ref, mod_ref, g1_ref, win_ref, gq_ref, wuq_ref, gkv_ref, cm_ref, sm_ref, cr_ref, sr_ref,
                   qn_ref, qr_ref, lat_ref, krt_ref, kr_ref, rq_ref, rk_ref, rv_ref, rg_ref, sq_ref, sk_ref,
                   sv_ref):
    bb, tl, d = x_ref.shape
    rows = bb * tl
    mod = mod_ref[...]
    h = _rms(x_ref[...]) * g1_ref[...] * (1.0 + mod[:, 1:2, :]) + mod[:, 0:1, :]
    z = _dot(h.reshape(rows, d).astype(BF16), win_ref[...])

    def put(ref, val):
        ref[...] = val.reshape(bb, tl, val.shape[-1]).astype(ref.dtype)

    def rope(x1, x2, c_ref, s_ref):
        c, s = c_ref[...][None], s_ref[...][None]
        x1 = x1.reshape(bb, tl, x1.shape[-1])
        x2 = x2.reshape(bb, tl, x2.shape[-1])
        return x1 * c - x2 * s, x1 * s + x2 * c

    q = _dot((_rms(z[:, _ZQ:_ZKV]) * gq_ref[...]).astype(BF16), wuq_ref[...])
    put(qn_ref, q[:, :512])

    def rope_swapped(a, b, width):
        a = a.reshape(bb, tl, width)
        b = b.reshape(bb, tl, width)
        return a * cm_ref[:, :width][None] + b * sm_ref[:, :width][None]

    put(qr_ref, rope_swapped(q[:, 512:768], q[:, 768:1024], 256))
    put(lat_ref, _rms(z[:, _ZKV:_RQ]) * gkv_ref[...])
    kr = rope_swapped(z[:, _KRT:_KRT + 128], z[:, _KRT + 128:_KRT + 256], 128)
    put(krt_ref, kr)
    kr_ref[...] = kr[:, :, :MLA_ROPE]
    a1, a2 = rope(z[:, _RQ:_RQ + 128], z[:, _RQ + 128:_RK], cr_ref, sr_ref)
    rq_ref[:, :, 0:128] = (a1 * (RET_DK ** -0.5)).astype(rq_ref.dtype)
    rq_ref[:, :, 128:256] = (a2 * (RET_DK ** -0.5)).astype(rq_ref.dtype)
    b1, b2 = rope(z[:, _RK:_RK + 128], z[:, _RK + 128:_RV], cr_ref, sr_ref)
    rk_ref[:, :, 0:128] = b1
    rk_ref[:, :, 128:256] = b2
    put(rv_ref, z[:, _RV:_RG])
    put(rg_ref, z[:, _RG:_SQ])
    put(sq_ref, z[:, _SQ:_SK] * (SB_DK ** -0.5))
    put(sk_ref, z[:, _SK:_SV])
    put(sv_ref, z[:, _SV:_KRT])


def _inproj(x, mod, g1, win, gq, wuq, gkv, cm, sm, cr, sr):
    b, l, d = x.shape
    tl = min(l, ROW_TILE)
    bb = ROW_TILE // tl
    grid = (b // bb, l // tl)
    tok = lambda w: pl.BlockSpec((bb, tl, w), lambda i, t: (i, t, 0))
    pos = pl.BlockSpec((tl, 128), lambda i, t: (t, 0))
    pos2 = pl.BlockSpec((tl, 256), lambda i, t: (t, 0))
    widths = [(512, BF16), (256, BF16), (256, F32), (128, BF16), (MLA_ROPE, F32), (256, BF16), (256, F32),
              (256, BF16), (256, F32), (256, BF16), (256, F32), (256, F32)]
    return pl.pallas_call(
        _inproj_kernel,
        out_shape=[jax.ShapeDtypeStruct((b, l, w), dt) for w, dt in widths],
        grid=grid,
        in_specs=[tok(d), pl.BlockSpec((bb, 6, d), lambda i, t: (i, 0, 0)), _const_spec(g1.shape),
                  _const_spec(win.shape), _const_spec(gq.shape), _const_spec(wuq.shape), _const_spec(gkv.shape),
                  pos2, pos2, pos, pos],
        out_specs=[tok(w) for w, _ in widths],
        compiler_params=_params("parallel", "parallel"),
        name="in_proj",
    )(x, mod, g1, win, gq, wuq, gkv, cm, sm, cr, sr)


def _kvup_kernel(lat_ref, krt_ref, w_ref, kp_ref, v_ref):
    kv = _dot(lat_ref[0].astype(BF16), w_ref[...])
    for p in range(MLA_HEADS // 2):
        kp_ref[0, :, 256 * p:256 * p + 128] = kv[:, 128 * p:128 * (p + 1)].astype(kp_ref.dtype)
        kp_ref[0, :, 256 * p + 128:256 * (p + 1)] = krt_ref[0]
    v_ref[0] = kv[:, 512:].astype(v_ref.dtype)


def _kvup(lat, krt, wukv):
    b, lk, r = lat.shape
    blk = lambda w: pl.BlockSpec((1, lk, w), lambda i: (i, 0, 0))
    return pl.pallas_call(
        _kvup_kernel,
        out_shape=(jax.ShapeDtypeStruct((b, lk, 1024), BF16), jax.ShapeDtypeStruct((b, lk, 512), BF16)),
        grid=(b,),
        in_specs=[blk(r), blk(128), _const_spec(wukv.shape)],
        out_specs=[blk(1024), blk(512)],
        compiler_params=_params("parallel"),
        name="kv_up",
    )(lat, krt, wukv)


def _fori_pairs(n, body, init):
    carry = lax.fori_loop(0, n // 2, lambda i, c: body(2 * i + 1, body(2 * i, c)), init)
    return lax.fori_loop(0, n % 2, lambda i, c: body(n - 1, c), carry)


def _fold(op, x):
    w = x.shape[1]
    parts = [x[:, i:i + 128] for i in range(0, w, 128)]
    out = parts[0]
    for p in parts[1:]:
        out = op(out, p)
    return out


def _mla_kernel(qn_ref, qr_ref, kp_ref, vp_ref, kd_ref, vd_ref, o_ref, s_scr, q_scr, mx_scr, m_scr, l_scr, acc_scr,
                *, tq, tk, off, scale, fused):
    qi = pl.program_id(1)
    nfull = (off + qi * tq) // tk
    ntile = nfull + 1 if fused else nfull
    rows = 4 * tq
    lane = lax.broadcasted_iota(jnp.int32, (1, 128), 1)
    row_c = (lax.broadcasted_iota(jnp.int32, (rows, 1), 0) % tq) // CHUNK
    col_c = lax.broadcasted_iota(jnp.int32, (1, tq), 1) // CHUNK
    c = scale * 1.4426950408889634
    neg = -1e30
    for g in range(MLA_HEADS // 4):
        gs = slice(HEAD_W * g, HEAD_W * (g + 1))
        qr = qr_ref[0, :, 128 * g:128 * (g + 1)]
        for pp in range(2):
            qn = qn_ref[0, :, 128 * (2 * g + pp):128 * (2 * g + pp + 1)]
            for a in range(2):
                q_scr[pp, a * tq:(a + 1) * tq, 0:128] = jnp.where((lane >> 6) == a, qn, jnp.zeros_like(qn))
                q_scr[pp, a * tq:(a + 1) * tq, 128:256] = jnp.where((lane >> 5) == 2 * pp + a, qr, jnp.zeros_like(qr))

        def raw_scores(k_of_pair):
            return jnp.concatenate([_dot_nt(q_scr[pp], k_of_pair(2 * g + pp)) for pp in range(2)], axis=0)

        def scores(j, carry):
            ks = pl.ds(pl.multiple_of(j * tk, tk), tk)
            s = raw_scores(lambda p: kp_ref[0, ks, 256 * p:256 * (p + 1)])
            s_scr[j] = s
            mx_scr[...] = jnp.maximum(mx_scr[...], _fold(jnp.maximum, s))
            return carry

        if fused:
            ks_d = pl.ds(pl.multiple_of(nfull * tk, tk), tk)
            s_d = jnp.where(col_c <= row_c, raw_scores(lambda p: kp_ref[0, ks_d, 256 * p:256 * (p + 1)]), neg)
            s_scr[nfull] = s_d
        else:
            s_d = jnp.where(col_c <= row_c, raw_scores(lambda p: kd_ref[0, :, 256 * p:256 * (p + 1)]), neg)
        if tq % 128 == 0:
            mx_scr[...] = _fold(jnp.maximum, s_d)
        else:
            mx_scr[...] = jnp.broadcast_to(jnp.max(s_d, axis=-1, keepdims=True), (rows, 128))
        _fori_pairs(nfull, scores, 0)
        m = jnp.max(mx_scr[...], axis=-1, keepdims=True)
        m_scr[...] = jnp.broadcast_to(m, (rows, 128))

        def values(j, carry):
            ks = pl.ds(pl.multiple_of(j * tk, tk), tk)
            m_rep = m_scr[...]
            ps = [jnp.exp2((s_scr[j, :, i:i + 128] - m_rep) * c) for i in range(0, tk, 128)]
            lsum = l_scr[...]
            for p in ps:
                lsum = lsum + p
            l_scr[...] = lsum
            acc_scr[...] += _dot(jnp.concatenate(ps, axis=-1).astype(BF16), vp_ref[0, ks, gs])
            return carry

        if fused:
            l_scr[...] = jnp.zeros((rows, 128), F32)
            acc_scr[...] = jnp.zeros((rows, HEAD_W), F32)
        else:
            p_d = jnp.exp2((s_d - m) * c)
            if tq % 128 == 0:
                l_scr[...] = _fold(jnp.add, p_d)
            else:
                l_scr[...] = jnp.broadcast_to(jnp.sum(p_d, axis=-1, keepdims=True), (rows, 128)) * (1.0 / 128)
            acc_scr[...] = _dot(p_d.astype(BF16), vd_ref[0, :, gs])
        _fori_pairs(ntile, values, 0)
        inv = 1.0 / jnp.sum(l_scr[...], axis=-1, keepdims=True)
        lane_v = lax.broadcasted_iota(jnp.int32, (1, HEAD_W), 1)
        out_g = jnp.zeros((tq, HEAD_W), F32)
        for hh in range(4):
            rs = slice(hh * tq, (hh + 1) * tq)
            out_g = jnp.where((lane_v >> 6) == hh, acc_scr[rs, :] * inv[rs], out_g)
        o_ref[0, :, gs] = out_g.astype(o_ref.dtype)


def _mla(qn, qr, past, diag, off):
    b, l, _ = qn.shape
    lp = past[0].shape[1]
    tq = min(l, ATT_TILE)
    tk = ATT_TILE
    fused = off == 0 and tq == tk
    qblk = lambda w: pl.BlockSpec((1, tq, w), lambda i, t: (i, t, 0))
    kblk = lambda w: pl.BlockSpec((1, lp, w), lambda i, t: (i, 0, 0))
    nmax = (off + l - tq) // tk + (1 if fused else 0)
    kern = functools.partial(_mla_kernel, tq=tq, tk=tk, off=off, scale=(MLA_NOPE + MLA_ROPE) ** -0.5, fused=fused)
    return pl.pallas_call(
        kern,
        out_shape=jax.ShapeDtypeStruct((b, l, 512), BF16),
        grid=(b, l // tq),
        in_specs=[qblk(512), qblk(256), kblk(1024), kblk(512), qblk(1024), qblk(512)],
        out_specs=qblk(512),
        scratch_shapes=[pltpu.VMEM((nmax, 4 * tq, tk), F32), pltpu.VMEM((2, 2 * tq, 256), BF16),
                        pltpu.VMEM((4 * tq, 128), F32), pltpu.VMEM((4 * tq, 128), F32), pltpu.VMEM((4 * tq, 128), F32),
                        pltpu.VMEM((4 * tq, HEAD_W), F32)],
        compiler_params=_params("parallel", "parallel"),
        name="mla_attn",
    )(qn, qr, *past, *diag)


def _neg_suffix_matrix(n):
    j = lax.broadcasted_iota(jnp.int32, (n, n), 0)
    s = lax.broadcasted_iota(jnp.int32, (n, n), 1)
    return jnp.where(j >= s, -1.0, 0.0).astype(BF16)


def _sb_kernel(q_ref, kp_ref, vp_ref, kd_ref, vd_ref, o_ref, kb_ref, vb_ref, q_scr, run_scr, acc_scr, *, tq, tk, off):
    qi = pl.program_id(1)

    @pl.when(qi == 0)
    def _():
        kb_ref[...] = kp_ref[0].astype(BF16)
        vb_ref[...] = vp_ref[0].astype(BF16)

    nfull = (off + qi * tq) // tk
    lane = lax.broadcasted_iota(jnp.int32, (1, HEAD_W), 1)
    nh = SB_HEADS
    dmask = (lax.broadcasted_iota(jnp.int32, (1, tq), 1)
             < lax.broadcasted_iota(jnp.int32, (nh * tq, 1), 0) % tq)
    t_diag = _neg_suffix_matrix(tq)
    t_full = t_diag if tk == tq else _neg_suffix_matrix(tk)
    q = q_ref[0]

    def tile(k, v, mask, tmat, first):
        zz = _dot_nt(q_scr[...], k)
        sp = jnp.maximum(zz, 0.0) + jnp.log(1.0 + jnp.exp(-jnp.abs(zz)))
        if mask is not None:
            sp = jnp.where(mask, sp, 0.0)
        hi = sp.astype(BF16)
        lo = (sp - hi.astype(F32)).astype(BF16)
        incl = _dot(hi, tmat) + _dot(lo, tmat)
        if first:
            w = jnp.exp(zz + incl)
            run = jnp.zeros((nh * tq, 128), F32)
        else:
            run = run_scr[...]
            w = jnp.exp(zz + incl + (run if zz.shape[1] == 128 else jnp.concatenate([run] * (zz.shape[1] // 128), -1)))
        if mask is not None:
            w = jnp.where(mask, w, 0.0)
        pv = _dot(w.astype(BF16), v)
        run = run - jnp.sum(sp, axis=-1, keepdims=True)
        run_scr[...] = run
        if first:
            acc_scr[...] = pv
        else:
            acc_scr[...] += pv
        return jnp.max(run)

    for h in range(nh):
        q_scr[h * tq:(h + 1) * tq, :] = jnp.where((lane >> 6) == h, q, jnp.zeros_like(q))
    top = tile(kd_ref[0].astype(BF16), vd_ref[0].astype(BF16), dmask, t_diag, True)

    def more(state):
        j, top = state
        return jnp.logical_and(j >= 0, top > SB_LOG_ZERO)

    def body(state):
        j, _ = state
        ks = pl.ds(pl.multiple_of(j * tk, tk), tk)
        return j - 1, tile(kb_ref[ks, :], vb_ref[ks, :], None, t_full, False)

    lax.while_loop(more, body, (nfull - 1, top))
    out = jnp.zeros((tq, HEAD_W), F32)
    for h in range(nh):
        out = jnp.where((lane >> 6) == h, acc_scr[h * tq:(h + 1) * tq, :], out)
    o_ref[0] = out.astype(o_ref.dtype)


def _sb(sq, past, diag, off):
    b, l, _ = sq.shape
    lp = past[0].shape[1]
    tq = min(l, ATT_TILE)
    qblk = pl.BlockSpec((1, tq, HEAD_W), lambda i, t: (i, t, 0))
    kblk = pl.BlockSpec((1, lp, HEAD_W), lambda i, t: (i, 0, 0))
    kern = functools.partial(_sb_kernel, tq=tq, tk=ATT_TILE, off=off)
    return pl.pallas_call(
        kern,
        out_shape=jax.ShapeDtypeStruct((b, l, HEAD_W), BF16),
        grid=(b, l // tq),
        in_specs=[qblk, kblk, kblk, qblk, qblk],
        out_specs=qblk,
        scratch_shapes=[pltpu.VMEM((lp, HEAD_W), BF16), pltpu.VMEM((lp, HEAD_W), BF16),
                        pltpu.VMEM((SB_HEADS * tq, HEAD_W), BF16), pltpu.VMEM((SB_HEADS * tq, 128), F32),
                        pltpu.VMEM((SB_HEADS * tq, HEAD_W), F32)],
        compiler_params=_params("parallel", "arbitrary"),
        name="sb_attn",
    )(sq, *past, *diag)


def _ret_kernel(q_ref, k_ref, v_ref, g_ref, s0_ref, gn_ref, dec_ref, qd_ref, kd_ref, sd_ref,
                o_ref, s_ref, *, c):
    n = q_ref.shape[1] // c
    lane = lax.broadcasted_iota(jnp.int32, (1, HEAD_W), 1)
    krow = lax.broadcasted_iota(jnp.int32, (HEAD_W, 1), 0)
    bd_mask = ((krow & 127) >> 5) == (lane >> 6)
    s_ref[0] = s0_ref[0]

    def step(i, carry):
        rs = pl.ds(pl.multiple_of(i * c, c), c)
        q = q_ref[0, rs, :]
        kf = k_ref[0, rs, :]
        k = kf.astype(BF16)
        v = v_ref[0, rs, :]
        state = s_ref[0]
        o = _dot(q, state.astype(BF16)) * qd_ref[...]
        for h in range(RET_HEADS):
            mk = ((lane & 127) >> 5) == h
            mv = (lane >> 6) == h
            att = _dot_nt(jnp.where(mk, q, jnp.zeros_like(q)), k) * dec_ref[h]
            o = o + jnp.where(mv, _dot(att.astype(BF16), v), 0.0)
        upd = _dot_tn((kf * kd_ref[...]).astype(BF16), v)
        s_ref[0] = sd_ref[...] * state + jnp.where(bd_mask, upd, 0.0)
        mu = jnp.zeros_like(o)
        for h in range(RET_HEADS):
            mv = (lane >> 6) == h
            mu = mu + jnp.where(mv, jnp.sum(jnp.where(mv, o, 0.0), axis=-1, keepdims=True), 0.0)
        dlt = o - mu * (1.0 / RET_DV)
        var = jnp.zeros_like(o)
        for h in range(RET_HEADS):
            mv = (lane >> 6) == h
            var = var + jnp.where(mv, jnp.sum(jnp.where(mv, dlt * dlt, 0.0), axis=-1, keepdims=True), 0.0)
        ro = dlt * lax.rsqrt(var * (1.0 / RET_DV) + EPS) * gn_ref[...]
        o_ref[0, rs, :] = (_silu(g_ref[0, rs, :]) * ro).astype(o_ref.dtype)
        return carry

    lax.fori_loop(0, n, step, 0)


def _ret_consts(c):
    lg = jnp.log(1.0 - 2.0 ** (-5.0 - jnp.arange(RET_HEADS, dtype=F32)))
    i = jnp.arange(c, dtype=F32)
    rel = i[:, None] - i[None, :]
    dec = jnp.where(rel >= 0, jnp.exp(lg[:, None, None] * jnp.maximum(rel, 0.0)), 0.0)
    v_head = jnp.arange(HEAD_W) // RET_DV
    k_head = (jnp.arange(HEAD_W) % 128) // (RET_DK // 2)
    qd = jnp.exp(lg[None, v_head] * (i[:, None] + 1.0))
    kd = jnp.exp(lg[None, k_head] * (c - 1.0 - i[:, None]))
    sd = jnp.exp(lg * c)[v_head][None, :]
    return dec, qd, kd, sd


def _ret(rq, rk, rv, rg, s0, gn):
    b, l, _ = rq.shape
    c = min(l, RET_CHUNK)
    dec, qd, kd, sd = _ret_consts(c)
    blk = pl.BlockSpec((1, l, HEAD_W), lambda i: (i, 0, 0))
    sblk = pl.BlockSpec((1, HEAD_W, HEAD_W), lambda i: (i, 0, 0))
    return pl.pallas_call(
        functools.partial(_ret_kernel, c=c),
        out_shape=[jax.ShapeDtypeStruct((b, l, HEAD_W), BF16), jax.ShapeDtypeStruct((b, HEAD_W, HEAD_W), F32)],
        grid=(b,),
        in_specs=[blk, blk, blk, blk, sblk, _const_spec(gn.shape), _const_spec(dec.shape), _const_spec(qd.shape),
                  _const_spec(kd.shape), _const_spec(sd.shape)],
        out_specs=[blk, sblk],
        compiler_params=_params("parallel"),
        name="retention",
    )(rq, rk, rv, rg, s0, gn, dec, qd, kd, sd)


def _ffn_kernel(mla_ref, ret_ref, sb_ref, x_ref, mod_ref, prev_ref, wo_ref, g2_ref, wup_ref, cw_ref, cb_ref,
                wdn_ref, gf_ref, o_ref, st_ref, a_scr, carry, *, final_norm):
    bb, tl, d = x_ref.shape
    rows = bb * tl
    t = pl.program_id(1)
    nt = pl.num_programs(1)

    @pl.when(t == 0)
    def _():
        carry[:, 6:8, :] = prev_ref[...]

    mod = mod_ref[...]
    cat = jnp.concatenate([mla_ref[...].reshape(rows, 512), ret_ref[...].reshape(rows, HEAD_W),
                           sb_ref[...].reshape(rows, HEAD_W)], axis=-1)
    x1 = x_ref[...] + mod[:, 2:3, :] * _dot(cat, wo_ref[...]).reshape(bb, tl, d)
    h = (_rms(x1) * g2_ref[...] * (1.0 + mod[:, 4:5, :]) + mod[:, 3:4, :]).reshape(rows, d).astype(BF16)
    f = jnp.zeros((rows, d), F32)
    for c0 in range(0, D_FF, FF_CHUNK):
        cs = slice(c0, c0 + FF_CHUNK)
        a = _dot(h, wup_ref[:, cs]).reshape(bb, tl, FF_CHUNK)
        b = _dot(h, wup_ref[:, D_FF + c0:D_FF + c0 + FF_CHUNK])
        a_scr[:, 8:, :] = a
        a_scr[:, 6:8, :] = carry[:, 6:8, cs]
        cw = cw_ref[...]
        conv = (cb_ref[:, cs] + cw[0:1, cs] * a_scr[:, 6:6 + tl, :] + cw[1:2, cs] * a_scr[:, 7:7 + tl, :]
                + cw[2:3, cs] * a)
        carry[:, 6:8, cs] = a_scr[:, tl + 6:tl + 8, :]
        y = (_silu(conv).reshape(rows, FF_CHUNK) * b).astype(BF16)
        f = f + _dot(y, wdn_ref[cs, :])
    x2 = x1 + mod[:, 5:6, :] * f.reshape(bb, tl, d)
    if final_norm:
        x2 = _rms(x2) * gf_ref[...]
    o_ref[...] = x2

    @pl.when(t == nt - 1)
    def _():
        st_ref[...] = carry[:, 6:8, :]


def _ffn(mla, ret, sb, x, mod, prev, wo, g2, wup, cw, cb, wdn, gf, final_norm):
    b, l, d = x.shape
    tl = min(l, ROW_TILE)
    bb = ROW_TILE // tl
    tok = lambda w: pl.BlockSpec((bb, tl, w), lambda i, t: (i, t, 0))
    st = pl.BlockSpec((bb, CONV_W - 1, D_FF), lambda i, t: (i, 0, 0))
    once = lambda a: pl.BlockSpec(a.shape, lambda i, t: (0, 0), pipeline_mode=pl.Buffered(1))
    return pl.pallas_call(
        functools.partial(_ffn_kernel, final_norm=final_norm),
        out_shape=[jax.ShapeDtypeStruct((b, l, d), F32), jax.ShapeDtypeStruct((b, CONV_W - 1, D_FF), F32)],
        grid=(b // bb, l // tl),
        in_specs=[tok(512), tok(HEAD_W), tok(HEAD_W), tok(d), pl.BlockSpec((bb, 6, d), lambda i, t: (i, 0, 0)), st,
                  once(wo), _const_spec(g2.shape), once(wup), _const_spec(cw.shape), _const_spec(cb.shape),
                  once(wdn), _const_spec(gf.shape)],
        out_specs=[tok(d), st],
        scratch_shapes=[pltpu.VMEM((bb, tl + 8, FF_CHUNK), F32), pltpu.VMEM((bb, 8, D_FF), F32)],
        compiler_params=_params("parallel", "arbitrary"),
        name="conv_ffn",
    )(mla, ret, sb, x, mod, prev, wo, g2, wup, cw, cb, wdn, gf)


def _half_split(w, heads):
    lead = w.shape[:-1]
    dim = w.shape[-1] // heads
    w = w.reshape(lead + (heads, 2, dim // 2))
    return jnp.swapaxes(w, -3, -2).reshape(lead + (heads * dim,))


def _permute_win(w):
    half = MLA_ROPE // 2
    kr = w[..., 640:672]
    kr_swapped = jnp.concatenate([kr[..., half:], kr[..., :half]], axis=-1)
    tiled = lambda a: jnp.tile(a, (1, 1, 4))
    return jnp.concatenate([w[..., 0:640], _half_split(w[..., 672:928], RET_HEADS),
                            _half_split(w[..., 928:1184], RET_HEADS), w[..., 1184:2464],
                            tiled(kr), tiled(kr_swapped)], axis=-1)


def _permute_wuq(w):
    half = MLA_ROPE // 2
    w = w.reshape(w.shape[:-1] + (MLA_HEADS, MLA_NOPE + MLA_ROPE))
    flat = lambda a: a.reshape(a.shape[:-2] + (-1,))
    rope = w[..., MLA_NOPE:]
    swapped = jnp.concatenate([rope[..., half:], rope[..., :half]], axis=-1)
    return jnp.concatenate([flat(w[..., :MLA_NOPE]), flat(rope), flat(swapped)], axis=-1)


def _permute_wukv(w):
    w = w.reshape(w.shape[:-1] + (MLA_HEADS, MLA_NOPE + MLA_V))
    flat = lambda a: a.reshape(a.shape[:-2] + (-1,))
    return jnp.concatenate([flat(w[..., :MLA_NOPE]), flat(w[..., MLA_NOPE:])], axis=-1)


def _ret_state_in(st):
    b = st.shape[0]
    st = st.reshape(b, RET_HEADS, 2, RET_DK // 2, RET_DV).transpose(0, 2, 1, 3, 4)
    eye = jnp.eye(RET_HEADS, dtype=st.dtype)
    full = st[:, :, :, :, None, :] * eye[None, None, :, None, :, None]
    return full.reshape(b, HEAD_W, HEAD_W)


def _ret_state_out(s):
    b = s.shape[0]
    s = s.reshape(b, 2, RET_HEADS, RET_DK // 2, RET_HEADS, RET_DV)
    blocks = [s[:, :, h, :, h, :].reshape(b, RET_DK, RET_DV) for h in range(RET_HEADS)]
    return jnp.stack(blocks, axis=1)


def _rope_angles(pos, dim):
    inv = ROPE_BASE ** (-jnp.arange(0, dim, 2, dtype=F32) / dim)
    return pos.astype(F32)[:, None] * inv[None, :]


def _rope_tables(pos, dim, reps):
    ang = _rope_angles(pos, dim)
    return jnp.tile(jnp.cos(ang), (1, reps)), jnp.tile(jnp.sin(ang), (1, reps))


def _rope_tables_swapped(pos, dim, reps):
    ang = _rope_angles(pos, dim)
    c, s = jnp.cos(ang), jnp.sin(ang)
    return jnp.tile(jnp.concatenate([c, c], axis=1), (1, reps)), jnp.tile(jnp.concatenate([-s, s], axis=1), (1, reps))


def _trunk(x, mods, pos0, cache, w):
    b, l, _ = x.shape
    depth = len(mods)
    pos = pos0 + jnp.arange(l)
    cm, sm = _rope_tables_swapped(pos, MLA_ROPE, MLA_HEADS)
    cr, sr = _rope_tables(pos, RET_DK, RET_HEADS)
    new = [[] for _ in range(6)]
    for layer in range(depth):
        mod = mods[layer]
        (qn, qr, lat, krt, kr, rq, rk, rv, rg, sq, sk, sv) = _inproj(
            x, mod, w["g_norm1"][layer], w["w_in"][layer], w["g_q_norm"][layer], w["w_uq"][layer],
            w["g_kv_norm"][layer], cm, sm, cr, sr)
        if cache is None:
            off = 0
            mla_past = mla_diag = _kvup(lat, krt, w["w_ukv"][layer])
            sb_past = sb_diag = (sk, sv)
            s0 = jnp.zeros((b, HEAD_W, HEAD_W), F32)
            prev = jnp.zeros((b, CONV_W - 1, D_FF), F32)
        else:
            off = cache["lat"].shape[2]
            krt_cache = jnp.tile(cache["kr"][layer], (1, 1, 4)).astype(BF16)
            mla_past = _kvup(cache["lat"][layer], krt_cache, w["w_ukv"][layer])
            mla_diag = _kvup(lat, krt, w["w_ukv"][layer])
            sb_past = (cache["sk"][layer].reshape(b, off, HEAD_W), cache["sv"][layer].reshape(b, off, HEAD_W))
            sb_diag = (sk, sv)
            s0 = _ret_state_in(cache["S"][layer])
            prev = cache["conv"][layer]
        mla = _mla(qn, qr, mla_past, mla_diag, off)
        sb = _sb(sq, sb_past, sb_diag, off)
        ret, s_new = _ret(rq, rk, rv, rg, s0, w["g_ret_norm"][layer])
        x, conv_state = _ffn(mla, ret, sb, x, mod, prev, w["w_o"][layer], w["g_norm2"][layer], w["w_up"][layer],
                             w["conv_w"][layer], w["conv_b"][layer], w["w_down"][layer], w["g_final"],
                             layer == depth - 1)
        for lst, val in zip(new, (lat, kr, sk.reshape(b, l, SB_HEADS, SB_DK), sv.reshape(b, l, SB_HEADS, SB_DV),
                                  _ret_state_out(s_new), conv_state)):
            lst.append(val)
    return x, [jnp.stack(s, axis=0) for s in new]


def kernel(x_prompt, x_sample, c_prompt, c_sample, cache_mla_latent, cache_mla_krope, cache_sb_k, cache_sb_v, state_ret, state_ffn_conv, w_in, g_q_norm, w_uq, g_kv_norm, w_ukv, g_ret_norm, w_o, w_up, conv_w, conv_b, w_down, g_norm1, g_norm2, w_ada, b_ada, g_final):
    depth = w_in.shape[0]
    bp = x_prompt.shape[0]
    row = lambda g: g.reshape(g.shape[0], 1, g.shape[-1])
    w = dict(
        w_in=_permute_win(w_in).astype(BF16),
        w_uq=_permute_wuq(w_uq).astype(BF16),
        w_ukv=_permute_wukv(w_ukv).astype(BF16),
        w_o=w_o.astype(BF16), w_up=w_up.astype(BF16), w_down=w_down.astype(BF16),
        g_q_norm=row(g_q_norm), g_kv_norm=row(g_kv_norm), g_ret_norm=row(g_ret_norm),
        g_norm1=row(g_norm1), g_norm2=row(g_norm2), conv_w=conv_w, conv_b=row(conv_b),
        g_final=g_final.reshape(1, -1))
    mod = _ada(jnp.concatenate([c_prompt, c_sample], axis=0), w_ada, b_ada)
    mod = mod.reshape(depth, mod.shape[1], 6, D_MODEL)
    y_p, st_p = _trunk(x_prompt, [mod[l, :bp] for l in range(depth)], 0, None, w)
    cache = dict(lat=cache_mla_latent, kr=cache_mla_krope, sk=cache_sb_k, sv=cache_sb_v, S=state_ret,
                 conv=state_ffn_conv)
    y_s, st_s = _trunk(x_sample, [mod[l, bp:] for l in range(depth)], cache_mla_latent.shape[2], cache, w)
    return (y_p, y_s, *st_p, *st_s)
```

```python
import functools

import numpy as np
import jax
import jax.numpy as jnp
from jax import lax
from jax.experimental import pallas as pl
from jax.experimental.pallas import tpu as pltpu

D_MODEL = 1024
CHUNK = 64
MLA_HEADS = 8
MLA_NOPE = 64
MLA_ROPE = 32
MLA_V = 64
MLA_Q_RANK = 384
MLA_KV_RANK = 256
RET_HEADS = 4
RET_DK = 64
RET_DV = 64
SB_HEADS = 4
SB_DK = 64
SB_DV = 64
D_FF = 2816
CONV_W = 3
ROPE_BASE = 10000.0
EPS = 1e-6

HEAD_W = 256
ROW_TILE = 512
ATT_TILE = 256
RET_CHUNK = 256
FF_CHUNK = 1408
SB_LOG_ZERO = -104.0
VMEM_LIMIT = 56 * 1024 * 1024

F32 = jnp.float32
BF16 = jnp.bfloat16

_ZQ, _ZKV, _RQ, _RK, _RV, _RG, _SQ, _SK, _SV, _KRT, _IN_COLS_P = (
    0, 384, 640, 896, 1152, 1408, 1664, 1920, 2176, 2432, 2688)


def _dot(a, b):
    return jnp.dot(a, b, preferred_element_type=F32)


def _dot_nt(a, b):
    return lax.dot_general(a, b, (((1,), (1,)), ((), ())), preferred_element_type=F32)


def _dot_tn(a, b):
    return lax.dot_general(a, b, (((0,), (0,)), ((), ())), preferred_element_type=F32)


def _rms(x):
    return x * lax.rsqrt(jnp.mean(x * x, axis=-1, keepdims=True) + EPS)


def _silu(x):
    return x / (1.0 + jnp.exp(-x))


def _params(*sem):
    return pltpu.CompilerParams(dimension_semantics=sem, vmem_limit_bytes=VMEM_LIMIT)


def _const_spec(shape):
    nd = len(shape)
    return pl.BlockSpec(shape, lambda *_: (0,) * nd)


def _ada_kernel(c_ref, w_ref, b_ref, o_ref):
    a = _silu(c_ref[...]).astype(BF16)
    o_ref[0] = _dot(a, w_ref[0].astype(BF16)) + b_ref[0]


def _ada(c_all, w_ada, b_ada):
    depth, d, n = w_ada.shape
    rows = c_all.shape[0]
    tn = 1536
    return pl.pallas_call(
        _ada_kernel,
        out_shape=jax.ShapeDtypeStruct((depth, rows, n), F32),
        grid=(depth, n // tn),
        in_specs=[pl.BlockSpec((rows, d), lambda l, j: (0, 0)),
                  pl.BlockSpec((1, d, tn), lambda l, j: (l, 0, j)),
                  pl.BlockSpec((1, 1, tn), lambda l, j: (l, 0, j))],
        out_specs=pl.BlockSpec((1, rows, tn), lambda l, j: (l, 0, j)),
        compiler_params=_params("parallel", "parallel"),
        name="ada_mod",
    )(c_all, w_ada, b_ada.reshape(depth, 1, n))


_INPROJ_INPUTS = 12
_INPROJ_OUTPUTS = (("qn", 512, BF16, False), ("qr", 256, BF16, False), ("lat", MLA_KV_RANK, F32, True),
                   ("kp", 1024, BF16, False), ("v", 512, BF16, False), ("kr", MLA_ROPE, F32, True),
                   ("rq", HEAD_W, BF16, False), ("rk", HEAD_W, F32, False), ("rv", HEAD_W, BF16, False),
                   ("rg", HEAD_W, F32, False), ("sq", HEAD_W, BF16, False), ("sk", HEAD_W, F32, True),
                   ("sv", HEAD_W, F32, True))


def _inproj_kernel(*refs):
    (x_ref, mod_ref, g1_ref, win_ref, gq_ref, wuq_ref, gkv_ref, wukv_ref, cm_ref, sm_ref, cr_ref,
     sr_ref) = refs[:_INPROJ_INPUTS]
    out = dict(zip([o[0] for o in _INPROJ_OUTPUTS], refs[len(refs) - len(_INPROJ_OUTPUTS):]))
    bb, tl, d = x_ref.shape
    rows = bb * tl
    mod = mod_ref[...]
    h = _rms(x_ref[...]) * g1_ref[...] * (1.0 + mod[:, 1:2, :]) + mod[:, 0:1, :]
    z = _dot(h.reshape(rows, d).astype(BF16), win_ref[...])

    def put(name, val, lanes=slice(None)):
        ref = out[name]
        lead = (0,) if len(ref.shape) == 4 else ()
        ref[lead + (slice(None), slice(None), lanes)] = val.reshape(bb, tl, val.shape[-1]).astype(ref.dtype)

    def rope(x1, x2, c_ref, s_ref):
        c, s = c_ref[...][None], s_ref[...][None]
        x1 = x1.reshape(bb, tl, x1.shape[-1])
        x2 = x2.reshape(bb, tl, x2.shape[-1])
        return x1 * c - x2 * s, x1 * s + x2 * c

    def rope_swapped(a, b, width):
        a = a.reshape(bb, tl, width)
        b = b.reshape(bb, tl, width)
        return a * cm_ref[:, :width][None] + b * sm_ref[:, :width][None]

    q = _dot((_rms(z[:, _ZQ:_ZKV]) * gq_ref[...]).astype(BF16), wuq_ref[...])
    put("qn", q[:, :512])
    put("qr", rope_swapped(q[:, 512:768], q[:, 768:1024], 256))
    lat = _rms(z[:, _ZKV:_RQ]) * gkv_ref[...]
    put("lat", lat)
    kr = rope_swapped(z[:, _KRT:_KRT + 128], z[:, _KRT + 128:_KRT + 256], 128)
    put("kr", kr[:, :, :MLA_ROPE])
    kv = _dot(lat.astype(BF16), wukv_ref[...])
    for p in range(MLA_HEADS // 2):
        put("kp", kv[:, 128 * p:128 * (p + 1)], slice(256 * p, 256 * p + 128))
        put("kp", kr, slice(256 * p + 128, 256 * (p + 1)))
    put("v", kv[:, 512:])
    a1, a2 = rope(z[:, _RQ:_RQ + 128], z[:, _RQ + 128:_RK], cr_ref, sr_ref)
    put("rq", a1 * (RET_DK ** -0.5), slice(0, 128))
    put("rq", a2 * (RET_DK ** -0.5), slice(128, 256))
    b1, b2 = rope(z[:, _RK:_RK + 128], z[:, _RK + 128:_RV], cr_ref, sr_ref)
    put("rk", b1, slice(0, 128))
    put("rk", b2, slice(128, 256))
    put("rv", z[:, _RV:_RG])
    put("rg", z[:, _RG:_SQ])
    put("sq", z[:, _SQ:_SK] * (SB_DK ** -0.5))
    put("sk", z[:, _SK:_SV])
    put("sv", z[:, _SV:_KRT])


def _inproj(x, mod, g1, win, gq, wuq, gkv, wukv, cm, sm, cr, sr, layer, depth, stacked):
    b, l, d = x.shape
    tl = min(l, ROW_TILE)
    bb = ROW_TILE // tl
    grid = (b // bb, l // tl)
    tok = lambda w: pl.BlockSpec((bb, tl, w), lambda i, t: (i, t, 0))
    stk = lambda w: pl.BlockSpec((1, bb, tl, w), lambda i, t: (layer, i, t, 0))
    pos = pl.BlockSpec((tl, 128), lambda i, t: (t, 0))
    pos2 = pl.BlockSpec((tl, 256), lambda i, t: (t, 0))
    stacked_idx = [k for k, o in enumerate(_INPROJ_OUTPUTS) if o[3]]
    extra = list(stacked)
    res = pl.pallas_call(
        _inproj_kernel,
        out_shape=[jax.ShapeDtypeStruct((depth, b, l, w) if st else (b, l, w), dt) for _, w, dt, st in _INPROJ_OUTPUTS],
        grid=grid,
        in_specs=[tok(d), pl.BlockSpec((bb, 6, d), lambda i, t: (i, 0, 0)), _const_spec(g1.shape),
                  _const_spec(win.shape), _const_spec(gq.shape), _const_spec(wuq.shape), _const_spec(gkv.shape),
                  _const_spec(wukv.shape), pos2, pos2, pos, pos] + [pl.BlockSpec(memory_space=pl.ANY)] * len(extra),
        out_specs=[stk(w) if st else tok(w) for _, w, _, st in _INPROJ_OUTPUTS],
        input_output_aliases={_INPROJ_INPUTS + k: o for k, o in enumerate(stacked_idx)},
        compiler_params=_params("parallel", "parallel"),
        name="in_proj",
    )(x, mod, g1, win, gq, wuq, gkv, wukv, cm, sm, cr, sr, *extra)
    return dict(zip([o[0] for o in _INPROJ_OUTPUTS], res))


def _kvup_kernel(lat_ref, krt_ref, w_ref, kp_ref, v_ref):
    kv = _dot(lat_ref[0].astype(BF16), w_ref[...])
    for p in range(MLA_HEADS // 2):
        kp_ref[0, :, 256 * p:256 * p + 128] = kv[:, 128 * p:128 * (p + 1)].astype(kp_ref.dtype)
        kp_ref[0, :, 256 * p + 128:256 * (p + 1)] = krt_ref[0]
    v_ref[0] = kv[:, 512:].astype(v_ref.dtype)


def _kvup(lat, layer, krt, wukv):
    _, b, lk, r = lat.shape
    blk = lambda w: pl.BlockSpec((1, lk, w), lambda i: (i, 0, 0))
    return pl.pallas_call(
        _kvup_kernel,
        out_shape=(jax.ShapeDtypeStruct((b, lk, 1024), BF16), jax.ShapeDtypeStruct((b, lk, 512), BF16)),
        grid=(b,),
        in_specs=[pl.BlockSpec((None, 1, lk, r), lambda i: (layer, i, 0, 0)), blk(128), _const_spec(wukv.shape)],
        out_specs=[blk(1024), blk(512)],
        compiler_params=_params("parallel"),
        name="kv_up",
    )(lat, krt, wukv)


def _fori_pairs(n, body, init):
    carry = lax.fori_loop(0, n // 2, lambda i, c: body(2 * i + 1, body(2 * i, c)), init)
    return lax.fori_loop(0, n % 2, lambda i, c: body(n - 1, c), carry)


def _fold(op, x):
    w = x.shape[1]
    parts = [x[:, i:i + 128] for i in range(0, w, 128)]
    out = parts[0]
    for p in parts[1:]:
        out = op(out, p)
    return out


def _mla_kernel(qn_ref, qr_ref, kp_ref, vp_ref, kd_ref, vd_ref, o_ref, s_scr, q_scr, mx_scr, m_scr, l_scr, acc_scr,
                *, tq, tk, off, scale, fused):
    qi = pl.program_id(1)
    nfull = (off + qi * tq) // tk
    ntile = nfull + 1 if fused else nfull
    rows = 4 * tq
    lane = lax.broadcasted_iota(jnp.int32, (1, 128), 1)
    row_c = (lax.broadcasted_iota(jnp.int32, (rows, 1), 0) % tq) // CHUNK
    col_c = lax.broadcasted_iota(jnp.int32, (1, tq), 1) // CHUNK
    c = scale * 1.4426950408889634
    neg = -1e30
    for g in range(MLA_HEADS // 4):
        gs = slice(HEAD_W * g, HEAD_W * (g + 1))
        qr = qr_ref[0, :, 128 * g:128 * (g + 1)]
        for pp in range(2):
            qn = qn_ref[0, :, 128 * (2 * g + pp):128 * (2 * g + pp + 1)]
            for a in range(2):
                q_scr[pp, a * tq:(a + 1) * tq, 0:128] = jnp.where((lane >> 6) == a, qn, jnp.zeros_like(qn))
                q_scr[pp, a * tq:(a + 1) * tq, 128:256] = jnp.where((lane >> 5) == 2 * pp + a, qr, jnp.zeros_like(qr))

        def raw_scores(k_of_pair):
            return jnp.concatenate([_dot_nt(q_scr[pp], k_of_pair(2 * g + pp)) for pp in range(2)], axis=0)

        def scores(j, carry):
            ks = pl.ds(pl.multiple_of(j * tk, tk), tk)
            s = raw_scores(lambda p: kp_ref[0, ks, 256 * p:256 * (p + 1)])
            s_scr[j] = s
            mx_scr[...] = jnp.maximum(mx_scr[...], _fold(jnp.maximum, s))
            return carry

        if fused:
            ks_d = pl.ds(pl.multiple_of(nfull * tk, tk), tk)
            s_d = jnp.where(col_c <= row_c, raw_scores(lambda p: kp_ref[0, ks_d, 256 * p:256 * (p + 1)]), neg)
            s_scr[nfull] = s_d
        else:
            s_d = jnp.where(col_c <= row_c, raw_scores(lambda p: kd_ref[0, :, 256 * p:256 * (p + 1)]), neg)
        if tq % 128 == 0:
            mx_scr[...] = _fold(jnp.maximum, s_d)
        else:
            mx_scr[...] = jnp.broadcast_to(jnp.max(s_d, axis=-1, keepdims=True), (rows, 128))
        _fori_pairs(nfull, scores, 0)
        m = jnp.max(mx_scr[...], axis=-1, keepdims=True)
        m_scr[...] = jnp.broadcast_to(m, (rows, 128))

        def values(j, carry):
            ks = pl.ds(pl.multiple_of(j * tk, tk), tk)
            m_rep = m_scr[...]
            ps = [jnp.exp2((s_scr[j, :, i:i + 128] - m_rep) * c) for i in range(0, tk, 128)]
            lsum = l_scr[...]
            for p in ps:
                lsum = lsum + p
            l_scr[...] = lsum
            acc_scr[...] += _dot(jnp.concatenate(ps, axis=-1).astype(BF16), vp_ref[0, ks, gs])
            return carry

        if fused:
            l_scr[...] = jnp.zeros((rows, 128), F32)
            acc_scr[...] = jnp.zeros((rows, HEAD_W), F32)
        else:
            p_d = jnp.exp2((s_d - m) * c)
            if tq % 128 == 0:
                l_scr[...] = _fold(jnp.add, p_d)
            else:
                l_scr[...] = jnp.broadcast_to(jnp.sum(p_d, axis=-1, keepdims=True), (rows, 128)) * (1.0 / 128)
            acc_scr[...] = _dot(p_d.astype(BF16), vd_ref[0, :, gs])
        _fori_pairs(ntile, values, 0)
        inv = 1.0 / jnp.sum(l_scr[...], axis=-1, keepdims=True)
        lane_v = lax.broadcasted_iota(jnp.int32, (1, HEAD_W), 1)
        out_g = jnp.zeros((tq, HEAD_W), F32)
        for hh in range(4):
            rs = slice(hh * tq, (hh + 1) * tq)
            out_g = jnp.where((lane_v >> 6) == hh, acc_scr[rs, :] * inv[rs], out_g)
        o_ref[0, :, gs] = out_g.astype(o_ref.dtype)


def _mla(qn, qr, past, diag, off):
    b, l, _ = qn.shape
    lp = past[0].shape[1]
    tq = min(l, ATT_TILE)
    tk = ATT_TILE
    fused = off == 0 and tq == tk
    qblk = lambda w: pl.BlockSpec((1, tq, w), lambda i, t: (i, t, 0))
    kblk = lambda w: pl.BlockSpec((1, lp, w), lambda i, t: (i, 0, 0))
    nmax = (off + l - tq) // tk + (1 if fused else 0)
    kern = functools.partial(_mla_kernel, tq=tq, tk=tk, off=off, scale=(MLA_NOPE + MLA_ROPE) ** -0.5, fused=fused)
    return pl.pallas_call(
        kern,
        out_shape=jax.ShapeDtypeStruct((b, l, 512), BF16),
        grid=(b, l // tq),
        in_specs=[qblk(512), qblk(256), kblk(1024), kblk(512), qblk(1024), qblk(512)],
        out_specs=qblk(512),
        scratch_shapes=[pltpu.VMEM((nmax, 4 * tq, tk), F32), pltpu.VMEM((2, 2 * tq, 256), BF16),
                        pltpu.VMEM((4 * tq, 128), F32), pltpu.VMEM((4 * tq, 128), F32), pltpu.VMEM((4 * tq, 128), F32),
                        pltpu.VMEM((4 * tq, HEAD_W), F32)],
        compiler_params=_params("parallel", "parallel"),
        name="mla_attn",
    )(qn, qr, *past, *diag)


def _neg_suffix_matrix(n):
    j = lax.broadcasted_iota(jnp.int32, (n, n), 0)
    s = lax.broadcasted_iota(jnp.int32, (n, n), 1)
    return jnp.where(j >= s, -1.0, 0.0).astype(BF16)


def _sb_kernel(q_ref, kp_ref, vp_ref, kd_ref, vd_ref, o_ref, kb_ref, vb_ref, q_scr, run_scr, acc_scr, *, tq, tk, off):
    qi = pl.program_id(1)

    @pl.when(qi == 0)
    def _():
        kb_ref[...] = kp_ref[0].astype(BF16)
        vb_ref[...] = vp_ref[0].astype(BF16)

    nfull = (off + qi * tq) // tk
    lane = lax.broadcasted_iota(jnp.int32, (1, HEAD_W), 1)
    nh = SB_HEADS
    dmask = (lax.broadcasted_iota(jnp.int32, (1, tq), 1)
             < lax.broadcasted_iota(jnp.int32, (nh * tq, 1), 0) % tq)
    t_diag = _neg_suffix_matrix(tq)
    t_full = t_diag if tk == tq else _neg_suffix_matrix(tk)
    q = q_ref[0]

    def tile(k, v, mask, tmat, first):
        zz = _dot_nt(q_scr[...], k)
        sp = jnp.maximum(zz, 0.0) + jnp.log(1.0 + jnp.exp(-jnp.abs(zz)))
        if mask is not None:
            sp = jnp.where(mask, sp, 0.0)
        hi = sp.astype(BF16)
        lo = (sp - hi.astype(F32)).astype(BF16)
        incl = _dot(hi, tmat) + _dot(lo, tmat)
        if first:
            w = jnp.exp(zz + incl)
            run = jnp.zeros((nh * tq, 128), F32)
        else:
            run = run_scr[...]
            w = jnp.exp(zz + incl + (run if zz.shape[1] == 128 else jnp.concatenate([run] * (zz.shape[1] // 128), -1)))
        if mask is not None:
            w = jnp.where(mask, w, 0.0)
        pv = _dot(w.astype(BF16), v)
        run = run - jnp.sum(sp, axis=-1, keepdims=True)
        run_scr[...] = run
        if first:
            acc_scr[...] = pv
        else:
            acc_scr[...] += pv
        return jnp.max(run)

    for h in range(nh):
        q_scr[h * tq:(h + 1) * tq, :] = jnp.where((lane >> 6) == h, q, jnp.zeros_like(q))
    top = tile(kd_ref[0].astype(BF16), vd_ref[0].astype(BF16), dmask, t_diag, True)

    def more(state):
        j, top = state
        return jnp.logical_and(j >= 0, top > SB_LOG_ZERO)

    def body(state):
        j, _ = state
        ks = pl.ds(pl.multiple_of(j * tk, tk), tk)
        return j - 1, tile(kb_ref[ks, :], vb_ref[ks, :], None, t_full, False)

    lax.while_loop(more, body, (nfull - 1, top))
    out = jnp.zeros((tq, HEAD_W), F32)
    for h in range(nh):
        out = jnp.where((lane >> 6) == h, acc_scr[h * tq:(h + 1) * tq, :], out)
    o_ref[0] = out.astype(o_ref.dtype)


def _sb(sq, past, diag, layer, off):
    b, l, _ = sq.shape
    lp = past[0].shape[2]
    tq = min(l, ATT_TILE)
    qblk = pl.BlockSpec((1, tq, HEAD_W), lambda i, t: (i, t, 0))
    dblk = pl.BlockSpec((None, 1, tq, HEAD_W), lambda i, t: (layer, i, t, 0))
    kblk = pl.BlockSpec((None, 1, lp, HEAD_W), lambda i, t: (layer, i, 0, 0))
    kern = functools.partial(_sb_kernel, tq=tq, tk=ATT_TILE, off=off)
    return pl.pallas_call(
        kern,
        out_shape=jax.ShapeDtypeStruct((b, l, HEAD_W), BF16),
        grid=(b, l // tq),
        in_specs=[qblk, kblk, kblk, dblk, dblk],
        out_specs=qblk,
        scratch_shapes=[pltpu.VMEM((lp, HEAD_W), BF16), pltpu.VMEM((lp, HEAD_W), BF16),
                        pltpu.VMEM((SB_HEADS * tq, HEAD_W), BF16), pltpu.VMEM((SB_HEADS * tq, 128), F32),
                        pltpu.VMEM((SB_HEADS * tq, HEAD_W), F32)],
        compiler_params=_params("parallel", "arbitrary"),
        name="sb_attn",
    )(sq, *past, *diag)


def _ret_kernel(q_ref, k_ref, v_ref, g_ref, s0_ref, gn_ref, dec_ref, qd_ref, kd_ref, sd_ref,
                o_ref, s_ref, *, c):
    n = q_ref.shape[1] // c
    lane = lax.broadcasted_iota(jnp.int32, (1, HEAD_W), 1)
    krow = lax.broadcasted_iota(jnp.int32, (HEAD_W, 1), 0)
    bd_mask = ((krow & 127) >> 5) == (lane >> 6)
    s_ref[0] = s0_ref[0]

    def step(i, carry):
        rs = pl.ds(pl.multiple_of(i * c, c), c)
        q = q_ref[0, rs, :]
        kf = k_ref[0, rs, :]
        k = kf.astype(BF16)
        v = v_ref[0, rs, :]
        state = s_ref[0]
        o = _dot(q, state.astype(BF16)) * qd_ref[...]
        for h in range(RET_HEADS):
            mk = ((lane & 127) >> 5) == h
            mv = (lane >> 6) == h
            att = _dot_nt(jnp.where(mk, q, jnp.zeros_like(q)), k) * dec_ref[h]
            o = o + jnp.where(mv, _dot(att.astype(BF16), v), 0.0)
        upd = _dot_tn((kf * kd_ref[...]).astype(BF16), v)
        s_ref[0] = sd_ref[...] * state + jnp.where(bd_mask, upd, 0.0)
        mu = jnp.zeros_like(o)
        for h in range(RET_HEADS):
            mv = (lane >> 6) == h
            mu = mu + jnp.where(mv, jnp.sum(jnp.where(mv, o, 0.0), axis=-1, keepdims=True), 0.0)
        dlt = o - mu * (1.0 / RET_DV)
        var = jnp.zeros_like(o)
        for h in range(RET_HEADS):
            mv = (lane >> 6) == h
            var = var + jnp.where(mv, jnp.sum(jnp.where(mv, dlt * dlt, 0.0), axis=-1, keepdims=True), 0.0)
        ro = dlt * lax.rsqrt(var * (1.0 / RET_DV) + EPS) * gn_ref[...]
        o_ref[0, rs, :] = (_silu(g_ref[0, rs, :]) * ro).astype(o_ref.dtype)
        return carry

    lax.fori_loop(0, n, step, 0)


def _ret_consts(c):
    lg = jnp.log(1.0 - 2.0 ** (-5.0 - jnp.arange(RET_HEADS, dtype=F32)))
    i = jnp.arange(c, dtype=F32)
    rel = i[:, None] - i[None, :]
    dec = jnp.where(rel >= 0, jnp.exp(lg[:, None, None] * jnp.maximum(rel, 0.0)), 0.0)
    v_head = jnp.arange(HEAD_W) // RET_DV
    k_head = (jnp.arange(HEAD_W) % 128) // (RET_DK // 2)
    qd = jnp.exp(lg[None, v_head] * (i[:, None] + 1.0))
    kd = jnp.exp(lg[None, k_head] * (c - 1.0 - i[:, None]))
    sd = jnp.exp(lg * c)[v_head][None, :]
    return dec, qd, kd, sd


def _ret(rq, rk, rv, rg, s0, gn):
    b, l, _ = rq.shape
    c = min(l, RET_CHUNK)
    dec, qd, kd, sd = _ret_consts(c)
    blk = pl.BlockSpec((1, l, HEAD_W), lambda i: (i, 0, 0))
    sblk = pl.BlockSpec((1, HEAD_W, HEAD_W), lambda i: (i, 0, 0))
    return pl.pallas_call(
        functools.partial(_ret_kernel, c=c),
        out_shape=[jax.ShapeDtypeStruct((b, l, HEAD_W), BF16), jax.ShapeDtypeStruct((b, HEAD_W, HEAD_W), F32)],
        grid=(b,),
        in_specs=[blk, blk, blk, blk, sblk, _const_spec(gn.shape), _const_spec(dec.shape), _const_spec(qd.shape),
                  _const_spec(kd.shape), _const_spec(sd.shape)],
        out_specs=[blk, sblk],
        compiler_params=_params("parallel"),
        name="retention",
    )(rq, rk, rv, rg, s0, gn, dec, qd, kd, sd)


def _ffn_kernel(mla_ref, ret_ref, sb_ref, x_ref, mod_ref, prev_ref, wo_ref, g2_ref, wup_ref, cw_ref, cb_ref,
                wdn_ref, gf_ref, o_ref, st_ref, a_scr, carry, *, final_norm):
    bb, tl, d = x_ref.shape
    rows = bb * tl
    t = pl.program_id(1)
    nt = pl.num_programs(1)

    @pl.when(t == 0)
    def _():
        carry[:, 6:8, :] = prev_ref[...]

    mod = mod_ref[...]
    cat = jnp.concatenate([mla_ref[...].reshape(rows, 512), ret_ref[...].reshape(rows, HEAD_W),
                           sb_ref[...].reshape(rows, HEAD_W)], axis=-1)
    x1 = x_ref[...] + mod[:, 2:3, :] * _dot(cat, wo_ref[...]).reshape(bb, tl, d)
    h = (_rms(x1) * g2_ref[...] * (1.0 + mod[:, 4:5, :]) + mod[:, 3:4, :]).reshape(rows, d).astype(BF16)
    f = jnp.zeros((rows, d), F32)
    for c0 in range(0, D_FF, FF_CHUNK):
        cs = slice(c0, c0 + FF_CHUNK)
        a = _dot(h, wup_ref[:, cs]).reshape(bb, tl, FF_CHUNK)
        b = _dot(h, wup_ref[:, D_FF + c0:D_FF + c0 + FF_CHUNK])
        a_scr[:, 8:, :] = a
        a_scr[:, 6:8, :] = carry[:, 6:8, cs]
        cw = cw_ref[...]
        conv = (cb_ref[:, cs] + cw[0:1, cs] * a_scr[:, 6:6 + tl, :] + cw[1:2, cs] * a_scr[:, 7:7 + tl, :]
                + cw[2:3, cs] * a)
        carry[:, 6:8, cs] = a_scr[:, tl + 6:tl + 8, :]
        y = (_silu(conv).reshape(rows, FF_CHUNK) * b).astype(BF16)
        f = f + _dot(y, wdn_ref[cs, :])
    x2 = x1 + mod[:, 5:6, :] * f.reshape(bb, tl, d)
    if final_norm:
        x2 = _rms(x2) * gf_ref[...]
    o_ref[...] = x2

    @pl.when(t == nt - 1)
    def _():
        st_ref[...] = carry[:, 6:8, :]


def _ffn(mla, ret, sb, x, mod, prev, wo, g2, wup, cw, cb, wdn, gf, final_norm):
    b, l, d = x.shape
    tl = min(l, ROW_TILE)
    bb = ROW_TILE // tl
    tok = lambda w: pl.BlockSpec((bb, tl, w), lambda i, t: (i, t, 0))
    st = pl.BlockSpec((bb, CONV_W - 1, D_FF), lambda i, t: (i, 0, 0))
    once = lambda a: pl.BlockSpec(a.shape, lambda i, t: (0, 0), pipeline_mode=pl.Buffered(1))
    return pl.pallas_call(
        functools.partial(_ffn_kernel, final_norm=final_norm),
        out_shape=[jax.ShapeDtypeStruct((b, l, d), F32), jax.ShapeDtypeStruct((b, CONV_W - 1, D_FF), F32)],
        grid=(b // bb, l // tl),
        in_specs=[tok(512), tok(HEAD_W), tok(HEAD_W), tok(d), pl.BlockSpec((bb, 6, d), lambda i, t: (i, 0, 0)), st,
                  once(wo), _const_spec(g2.shape), once(wup), _const_spec(cw.shape), _const_spec(cb.shape),
                  once(wdn), _const_spec(gf.shape)],
        out_specs=[tok(d), st],
        scratch_shapes=[pltpu.VMEM((bb, tl + 8, FF_CHUNK), F32), pltpu.VMEM((bb, 8, D_FF), F32)],
        compiler_params=_params("parallel", "arbitrary"),
        name="conv_ffn",
    )(mla, ret, sb, x, mod, prev, wo, g2, wup, cw, cb, wdn, gf)


def _half_split(w, heads):
    lead = w.shape[:-1]
    dim = w.shape[-1] // heads
    w = w.reshape(lead + (heads, 2, dim // 2))
    return jnp.swapaxes(w, -3, -2).reshape(lead + (heads * dim,))


def _permute_win(w):
    half = MLA_ROPE // 2
    kr = w[..., 640:672]
    kr_swapped = jnp.concatenate([kr[..., half:], kr[..., :half]], axis=-1)
    tiled = lambda a: jnp.tile(a, (1, 1, 4))
    return jnp.concatenate([w[..., 0:640], _half_split(w[..., 672:928], RET_HEADS),
                            _half_split(w[..., 928:1184], RET_HEADS), w[..., 1184:2464],
                            tiled(kr), tiled(kr_swapped)], axis=-1)


def _permute_wuq(w):
    half = MLA_ROPE // 2
    w = w.reshape(w.shape[:-1] + (MLA_HEADS, MLA_NOPE + MLA_ROPE))
    flat = lambda a: a.reshape(a.shape[:-2] + (-1,))
    rope = w[..., MLA_NOPE:]
    swapped = jnp.concatenate([rope[..., half:], rope[..., :half]], axis=-1)
    return jnp.concatenate([flat(w[..., :MLA_NOPE]), flat(rope), flat(swapped)], axis=-1)


def _permute_wukv(w):
    w = w.reshape(w.shape[:-1] + (MLA_HEADS, MLA_NOPE + MLA_V))
    flat = lambda a: a.reshape(a.shape[:-2] + (-1,))
    return jnp.concatenate([flat(w[..., :MLA_NOPE]), flat(w[..., MLA_NOPE:])], axis=-1)


def _ret_state_in(st):
    b = st.shape[0]
    st = st.reshape(b, RET_HEADS, 2, RET_DK // 2, RET_DV).transpose(0, 2, 1, 3, 4)
    eye = jnp.eye(RET_HEADS, dtype=st.dtype)
    full = st[:, :, :, :, None, :] * eye[None, None, :, None, :, None]
    return full.reshape(b, HEAD_W, HEAD_W)


def _ret_state_out(s):
    b = s.shape[0]
    s = s.reshape(b, 2, RET_HEADS, RET_DK // 2, RET_HEADS, RET_DV)
    blocks = [s[:, :, h, :, h, :].reshape(b, RET_DK, RET_DV) for h in range(RET_HEADS)]
    return jnp.stack(blocks, axis=1)


def _rope_angles(pos, dim):
    inv = ROPE_BASE ** (-jnp.arange(0, dim, 2, dtype=F32) / dim)
    return pos.astype(F32)[:, None] * inv[None, :]


def _rope_tables(pos, dim, reps):
    ang = _rope_angles(pos, dim)
    return jnp.tile(jnp.cos(ang), (1, reps)), jnp.tile(jnp.sin(ang), (1, reps))


def _rope_tables_swapped(pos, dim, reps):
    ang = _rope_angles(pos, dim)
    c, s = jnp.cos(ang), jnp.sin(ang)
    return jnp.tile(jnp.concatenate([c, c], axis=1), (1, reps)), jnp.tile(jnp.concatenate([-s, s], axis=1), (1, reps))


def _trunk(x, mods, pos0, cache, w):
    b, l, _ = x.shape
    depth = len(mods)
    pos = pos0 + jnp.arange(l)
    cm, sm = _rope_tables_swapped(pos, MLA_ROPE, MLA_HEADS)
    cr, sr = _rope_tables(pos, RET_DK, RET_HEADS)
    if cache is None:
        off = 0
    else:
        off = cache["lat"].shape[2]
        krt_cache = jnp.tile(cache["kr"], (1, 1, 1, 4)).astype(BF16)
        sb_cache = (cache["sk"].reshape(depth, b, off, HEAD_W), cache["sv"].reshape(depth, b, off, HEAD_W))
    stacked = tuple(jnp.zeros((depth, b, l, wd), F32) for _, wd, _, st in _INPROJ_OUTPUTS if st)
    small = [[], []]
    for layer in range(depth):
        mod = mods[layer]
        p = _inproj(x, mod, w["g_norm1"][layer], w["w_in"][layer], w["g_q_norm"][layer], w["w_uq"][layer],
                    w["g_kv_norm"][layer], w["w_ukv"][layer], cm, sm, cr, sr, layer, depth, stacked)
        stacked = (p["lat"], p["kr"], p["sk"], p["sv"])
        sb_new = (p["sk"], p["sv"])
        if cache is None:
            mla_past, sb_past = (p["kp"], p["v"]), sb_new
            s0 = jnp.zeros((b, HEAD_W, HEAD_W), F32)
            prev = jnp.zeros((b, CONV_W - 1, D_FF), F32)
        else:
            mla_past = _kvup(cache["lat"], layer, krt_cache[layer], w["w_ukv"][layer])
            sb_past = sb_cache
            s0 = _ret_state_in(cache["S"][layer])
            prev = cache["conv"][layer]
        mla = _mla(p["qn"], p["qr"], mla_past, (p["kp"], p["v"]), off)
        sb = _sb(p["sq"], sb_past, sb_new, layer, off)
        ret, s_new = _ret(p["rq"], p["rk"], p["rv"], p["rg"], s0, w["g_ret_norm"][layer])
        x, conv_state = _ffn(mla, ret, sb, x, mod, prev, w["w_o"][layer], w["g_norm2"][layer], w["w_up"][layer],
                             w["conv_w"][layer], w["conv_b"][layer], w["w_down"][layer], w["g_final"],
                             layer == depth - 1)
        small[0].append(_ret_state_out(s_new))
        small[1].append(conv_state)
    lat, kr, sk, sv = stacked
    return x, [lat, kr, sk.reshape(depth, b, l, SB_HEADS, SB_DK), sv.reshape(depth, b, l, SB_HEADS, SB_DV),
               jnp.stack(small[0], axis=0), jnp.stack(small[1], axis=0)]


def kernel(x_prompt, x_sample, c_prompt, c_sample, cache_mla_latent, cache_mla_krope, cache_sb_k, cache_sb_v, state_ret, state_ffn_conv, w_in, g_q_norm, w_uq, g_kv_norm, w_ukv, g_ret_norm, w_o, w_up, conv_w, conv_b, w_down, g_norm1, g_norm2, w_ada, b_ada, g_final):
    depth = w_in.shape[0]
    bp = x_prompt.shape[0]
    row = lambda g: g.reshape(g.shape[0], 1, g.shape[-1])
    w = dict(
        w_in=_permute_win(w_in).astype(BF16),
        w_uq=_permute_wuq(w_uq).astype(BF16),
        w_ukv=_permute_wukv(w_ukv).astype(BF16),
        w_o=w_o.astype(BF16), w_up=w_up.astype(BF16), w_down=w_down.astype(BF16),
        g_q_norm=row(g_q_norm), g_kv_norm=row(g_kv_norm), g_ret_norm=row(g_ret_norm),
        g_norm1=row(g_norm1), g_norm2=row(g_norm2), conv_w=conv_w, conv_b=row(conv_b),
        g_final=g_final.reshape(1, -1))
    mod = _ada(jnp.concatenate([c_prompt, c_sample], axis=0), w_ada, b_ada)
    mod = mod.reshape(depth, mod.shape[1], 6, D_MODEL)
    y_p, st_p = _trunk(x_prompt, [mod[l, :bp] for l in range(depth)], 0, None, w)
    cache = dict(lat=cache_mla_latent, kr=cache_mla_krope, sk=cache_sb_k, sv=cache_sb_v, S=state_ret,
                 conv=state_ffn_conv)
    y_s, st_s = _trunk(x_sample, [mod[l, bp:] for l in range(depth)], cache_mla_latent.shape[2], cache, w)
    return (y_p, y_s, *st_p, *st_s)
```

```python
import functools

import numpy as np
import jax
import jax.numpy as jnp
from jax import lax
from jax.experimental import pallas as pl
from jax.experimental.pallas import tpu as pltpu

D_MODEL = 1024
CHUNK = 64
MLA_HEADS = 8
MLA_NOPE = 64
MLA_ROPE = 32
MLA_V = 64
MLA_Q_RANK = 384
MLA_KV_RANK = 256
RET_HEADS = 4
RET_DK = 64
RET_DV = 64
SB_HEADS = 4
SB_DK = 64
SB_DV = 64
D_FF = 2816
CONV_W = 3
ROPE_BASE = 10000.0
EPS = 1e-6

HEAD_W = 256
ROW_TILE = 512
ATT_TILE = 256
RET_CHUNK = 256
FF_CHUNKS = ((0, 1536), (1536, D_FF))
SB_LOG_ZERO = -104.0
VMEM_LIMIT = 56 * 1024 * 1024

F32 = jnp.float32
BF16 = jnp.bfloat16

_ZQ, _ZKV, _RQ, _RK, _RV, _RG, _SQ, _SK, _SV, _KRT, _IN_COLS_P = (
    0, 384, 640, 896, 1152, 1408, 1664, 1920, 2176, 2432, 2688)


def _dot(a, b):
    return jnp.dot(a, b, preferred_element_type=F32)


def _dot_nt(a, b):
    return lax.dot_general(a, b, (((1,), (1,)), ((), ())), preferred_element_type=F32)


def _dot_tn(a, b):
    return lax.dot_general(a, b, (((0,), (0,)), ((), ())), preferred_element_type=F32)


def _rms(x):
    return x * lax.rsqrt(jnp.mean(x * x, axis=-1, keepdims=True) + EPS)


def _silu(x):
    return x / (1.0 + jnp.exp(-x))


def _params(*sem):
    return pltpu.CompilerParams(dimension_semantics=sem, vmem_limit_bytes=VMEM_LIMIT)


def _const_spec(shape):
    nd = len(shape)
    return pl.BlockSpec(shape, lambda *_: (0,) * nd)


def _ada_kernel(c_ref, w_ref, b_ref, o_ref):
    a = _silu(c_ref[...]).astype(BF16)
    o_ref[0] = _dot(a, w_ref[0].astype(BF16)) + b_ref[0]


def _ada(c_all, w_ada, b_ada):
    depth, d, n = w_ada.shape
    rows = c_all.shape[0]
    tn = 1536
    return pl.pallas_call(
        _ada_kernel,
        out_shape=jax.ShapeDtypeStruct((depth, rows, n), F32),
        grid=(depth, n // tn),
        in_specs=[pl.BlockSpec((rows, d), lambda l, j: (0, 0)),
                  pl.BlockSpec((1, d, tn), lambda l, j: (l, 0, j)),
                  pl.BlockSpec((1, 1, tn), lambda l, j: (l, 0, j))],
        out_specs=pl.BlockSpec((1, rows, tn), lambda l, j: (l, 0, j)),
        compiler_params=_params("parallel", "parallel"),
        name="ada_mod",
    )(c_all, w_ada, b_ada.reshape(depth, 1, n))


_INPROJ_INPUTS = 12
_INPROJ_OUTPUTS = (("qn", 512, BF16, False), ("qr", 256, BF16, False), ("lat", MLA_KV_RANK, F32, True),
                   ("kp", 1024, BF16, False), ("v", 512, BF16, False), ("kr", MLA_ROPE, F32, True),
                   ("rq", HEAD_W, BF16, False), ("rk", HEAD_W, F32, False), ("rv", HEAD_W, BF16, False),
                   ("rg", HEAD_W, F32, False), ("sq", HEAD_W, BF16, False), ("sk", HEAD_W, F32, True),
                   ("sv", HEAD_W, F32, True))


def _inproj_kernel(*refs):
    (x_ref, mod_ref, g1_ref, win_ref, gq_ref, wuq_ref, gkv_ref, wukv_ref, cm_ref, sm_ref, cr_ref,
     sr_ref) = refs[:_INPROJ_INPUTS]
    out = dict(zip([o[0] for o in _INPROJ_OUTPUTS], refs[len(refs) - len(_INPROJ_OUTPUTS):]))
    bb, tl, d = x_ref.shape
    rows = bb * tl
    mod = mod_ref[...]
    h = _rms(x_ref[...]) * g1_ref[...] * (1.0 + mod[:, 1:2, :]) + mod[:, 0:1, :]
    z = _dot(h.reshape(rows, d).astype(BF16), win_ref[...])

    def put(name, val, lanes=slice(None)):
        ref = out[name]
        lead = (0,) if len(ref.shape) == 4 else ()
        ref[lead + (slice(None), slice(None), lanes)] = val.reshape(bb, tl, val.shape[-1]).astype(ref.dtype)

    def rope(x1, x2, c_ref, s_ref):
        c, s = c_ref[...][None], s_ref[...][None]
        x1 = x1.reshape(bb, tl, x1.shape[-1])
        x2 = x2.reshape(bb, tl, x2.shape[-1])
        return x1 * c - x2 * s, x1 * s + x2 * c

    def rope_swapped(a, b, width):
        a = a.reshape(bb, tl, width)
        b = b.reshape(bb, tl, width)
        return a * cm_ref[:, :width][None] + b * sm_ref[:, :width][None]

    q = _dot((_rms(z[:, _ZQ:_ZKV]) * gq_ref[...]).astype(BF16), wuq_ref[...])
    put("qn", q[:, :512])
    put("qr", rope_swapped(q[:, 512:768], q[:, 768:1024], 256))
    lat = _rms(z[:, _ZKV:_RQ]) * gkv_ref[...]
    put("lat", lat)
    kr = rope_swapped(z[:, _KRT:_KRT + 128], z[:, _KRT + 128:_KRT + 256], 128)
    put("kr", kr[:, :, :MLA_ROPE])
    kv = _dot(lat.astype(BF16), wukv_ref[...])
    for p in range(MLA_HEADS // 2):
        put("kp", kv[:, 128 * p:128 * (p + 1)], slice(256 * p, 256 * p + 128))
        put("kp", kr, slice(256 * p + 128, 256 * (p + 1)))
    put("v", kv[:, 512:])
    a1, a2 = rope(z[:, _RQ:_RQ + 128], z[:, _RQ + 128:_RK], cr_ref, sr_ref)
    put("rq", a1 * (RET_DK ** -0.5), slice(0, 128))
    put("rq", a2 * (RET_DK ** -0.5), slice(128, 256))
    b1, b2 = rope(z[:, _RK:_RK + 128], z[:, _RK + 128:_RV], cr_ref, sr_ref)
    put("rk", b1, slice(0, 128))
    put("rk", b2, slice(128, 256))
    put("rv", z[:, _RV:_RG])
    put("rg", z[:, _RG:_SQ])
    put("sq", z[:, _SQ:_SK] * (SB_DK ** -0.5))
    put("sk", z[:, _SK:_SV])
    put("sv", z[:, _SV:_KRT])


def _inproj(x, mod, g1, win, gq, wuq, gkv, wukv, cm, sm, cr, sr, layer, depth, stacked):
    b, l, d = x.shape
    tl = min(l, ROW_TILE)
    bb = ROW_TILE // tl
    grid = (b // bb, l // tl)
    tok = lambda w: pl.BlockSpec((bb, tl, w), lambda i, t: (i, t, 0))
    stk = lambda w: pl.BlockSpec((1, bb, tl, w), lambda i, t: (layer, i, t, 0))
    pos = pl.BlockSpec((tl, 128), lambda i, t: (t, 0))
    pos2 = pl.BlockSpec((tl, 256), lambda i, t: (t, 0))
    stacked_idx = [k for k, o in enumerate(_INPROJ_OUTPUTS) if o[3]]
    extra = list(stacked)
    res = pl.pallas_call(
        _inproj_kernel,
        out_shape=[jax.ShapeDtypeStruct((depth, b, l, w) if st else (b, l, w), dt) for _, w, dt, st in _INPROJ_OUTPUTS],
        grid=grid,
        in_specs=[tok(d), pl.BlockSpec((bb, 6, d), lambda i, t: (i, 0, 0)), _const_spec(g1.shape),
                  _const_spec(win.shape), _const_spec(gq.shape), _const_spec(wuq.shape), _const_spec(gkv.shape),
                  _const_spec(wukv.shape), pos2, pos2, pos, pos] + [pl.BlockSpec(memory_space=pl.ANY)] * len(extra),
        out_specs=[stk(w) if st else tok(w) for _, w, _, st in _INPROJ_OUTPUTS],
        input_output_aliases={_INPROJ_INPUTS + k: o for k, o in enumerate(stacked_idx)},
        compiler_params=_params("parallel", "parallel"),
        name="in_proj",
    )(x, mod, g1, win, gq, wuq, gkv, wukv, cm, sm, cr, sr, *extra)
    return dict(zip([o[0] for o in _INPROJ_OUTPUTS], res))


def _kvup_kernel(lat_ref, krt_ref, w_ref, kp_ref, v_ref):
    kv = _dot(lat_ref[0].astype(BF16), w_ref[...])
    for p in range(MLA_HEADS // 2):
        kp_ref[0, :, 256 * p:256 * p + 128] = kv[:, 128 * p:128 * (p + 1)].astype(kp_ref.dtype)
        kp_ref[0, :, 256 * p + 128:256 * (p + 1)] = krt_ref[0]
    v_ref[0] = kv[:, 512:].astype(v_ref.dtype)


def _kvup(lat, layer, krt, wukv):
    _, b, lk, r = lat.shape
    blk = lambda w: pl.BlockSpec((1, lk, w), lambda i: (i, 0, 0))
    return pl.pallas_call(
        _kvup_kernel,
        out_shape=(jax.ShapeDtypeStruct((b, lk, 1024), BF16), jax.ShapeDtypeStruct((b, lk, 512), BF16)),
        grid=(b,),
        in_specs=[pl.BlockSpec((None, 1, lk, r), lambda i: (layer, i, 0, 0)), blk(128), _const_spec(wukv.shape)],
        out_specs=[blk(1024), blk(512)],
        compiler_params=_params("parallel"),
        name="kv_up",
    )(lat, krt, wukv)


def _fori_pairs(n, body, init):
    carry = lax.fori_loop(0, n // 2, lambda i, c: body(2 * i + 1, body(2 * i, c)), init)
    return lax.fori_loop(0, n % 2, lambda i, c: body(n - 1, c), carry)


def _fold(op, x):
    w = x.shape[1]
    parts = [x[:, i:i + 128] for i in range(0, w, 128)]
    out = parts[0]
    for p in parts[1:]:
        out = op(out, p)
    return out


def _mla_kernel(qn_ref, qr_ref, kp_ref, vp_ref, kd_ref, vd_ref, o_ref, s_scr, q_scr, mx_scr, m_scr, l_scr, acc_scr,
                *, tq, tk, off, scale, fused):
    qi = pl.program_id(1)
    nfull = (off + qi * tq) // tk
    ntile = nfull + 1 if fused else nfull
    rows = 4 * tq
    lane = lax.broadcasted_iota(jnp.int32, (1, 128), 1)
    row_c = (lax.broadcasted_iota(jnp.int32, (rows, 1), 0) % tq) // CHUNK
    col_c = lax.broadcasted_iota(jnp.int32, (1, tq), 1) // CHUNK
    c = scale * 1.4426950408889634
    neg = -1e30
    for g in range(MLA_HEADS // 4):
        gs = slice(HEAD_W * g, HEAD_W * (g + 1))
        qr = qr_ref[0, :, 128 * g:128 * (g + 1)]
        for pp in range(2):
            qn = qn_ref[0, :, 128 * (2 * g + pp):128 * (2 * g + pp + 1)]
            for a in range(2):
                q_scr[pp, a * tq:(a + 1) * tq, 0:128] = jnp.where((lane >> 6) == a, qn, jnp.zeros_like(qn))
                q_scr[pp, a * tq:(a + 1) * tq, 128:256] = jnp.where((lane >> 5) == 2 * pp + a, qr, jnp.zeros_like(qr))

        def raw_scores(k_of_pair):
            return jnp.concatenate([_dot_nt(q_scr[pp], k_of_pair(2 * g + pp)) for pp in range(2)], axis=0)

        def scores(j, carry):
            ks = pl.ds(pl.multiple_of(j * tk, tk), tk)
            s = raw_scores(lambda p: kp_ref[0, ks, 256 * p:256 * (p + 1)])
            s_scr[j] = s
            mx_scr[...] = jnp.maximum(mx_scr[...], _fold(jnp.maximum, s))
            return carry

        if fused:
            ks_d = pl.ds(pl.multiple_of(nfull * tk, tk), tk)
            s_d = jnp.where(col_c <= row_c, raw_scores(lambda p: kp_ref[0, ks_d, 256 * p:256 * (p + 1)]), neg)
            s_scr[nfull] = s_d
        else:
            s_d = jnp.where(col_c <= row_c, raw_scores(lambda p: kd_ref[0, :, 256 * p:256 * (p + 1)]), neg)
        if tq % 128 == 0:
            mx_scr[...] = _fold(jnp.maximum, s_d)
        else:
            mx_scr[...] = jnp.broadcast_to(jnp.max(s_d, axis=-1, keepdims=True), (rows, 128))
        _fori_pairs(nfull, scores, 0)
        m = jnp.max(mx_scr[...], axis=-1, keepdims=True)
        m_scr[...] = jnp.broadcast_to(m, (rows, 128))

        def values(j, carry):
            ks = pl.ds(pl.multiple_of(j * tk, tk), tk)
            m_rep = m_scr[...]
            ps = [jnp.exp2((s_scr[j, :, i:i + 128] - m_rep) * c) for i in range(0, tk, 128)]
            lsum = l_scr[...]
            for p in ps:
                lsum = lsum + p
            l_scr[...] = lsum
            acc_scr[...] += _dot(jnp.concatenate(ps, axis=-1).astype(BF16), vp_ref[0, ks, gs])
            return carry

        if fused:
            l_scr[...] = jnp.zeros((rows, 128), F32)
            acc_scr[...] = jnp.zeros((rows, HEAD_W), F32)
        else:
            p_d = jnp.exp2((s_d - m) * c)
            if tq % 128 == 0:
                l_scr[...] = _fold(jnp.add, p_d)
            else:
                l_scr[...] = jnp.broadcast_to(jnp.sum(p_d, axis=-1, keepdims=True), (rows, 128)) * (1.0 / 128)
            acc_scr[...] = _dot(p_d.astype(BF16), vd_ref[0, :, gs])
        _fori_pairs(ntile, values, 0)
        inv = 1.0 / jnp.sum(l_scr[...], axis=-1, keepdims=True)
        lane_v = lax.broadcasted_iota(jnp.int32, (1, HEAD_W), 1)
        out_g = jnp.zeros((tq, HEAD_W), F32)
        for hh in range(4):
            rs = slice(hh * tq, (hh + 1) * tq)
            out_g = jnp.where((lane_v >> 6) == hh, acc_scr[rs, :] * inv[rs], out_g)
        o_ref[0, :, gs] = out_g.astype(o_ref.dtype)


def _mla(qn, qr, past, diag, off):
    b, l, _ = qn.shape
    lp = past[0].shape[1]
    tq = min(l, ATT_TILE)
    tk = ATT_TILE
    fused = off == 0 and tq == tk
    qblk = lambda w: pl.BlockSpec((1, tq, w), lambda i, t: (i, t, 0))
    kblk = lambda w: pl.BlockSpec((1, lp, w), lambda i, t: (i, 0, 0))
    nmax = (off + l - tq) // tk + (1 if fused else 0)
    kern = functools.partial(_mla_kernel, tq=tq, tk=tk, off=off, scale=(MLA_NOPE + MLA_ROPE) ** -0.5, fused=fused)
    return pl.pallas_call(
        kern,
        out_shape=jax.ShapeDtypeStruct((b, l, 512), BF16),
        grid=(b, l // tq),
        in_specs=[qblk(512), qblk(256), kblk(1024), kblk(512), qblk(1024), qblk(512)],
        out_specs=qblk(512),
        scratch_shapes=[pltpu.VMEM((nmax, 4 * tq, tk), F32), pltpu.VMEM((2, 2 * tq, 256), BF16),
                        pltpu.VMEM((4 * tq, 128), F32), pltpu.VMEM((4 * tq, 128), F32), pltpu.VMEM((4 * tq, 128), F32),
                        pltpu.VMEM((4 * tq, HEAD_W), F32)],
        compiler_params=_params("parallel", "parallel"),
        name="mla_attn",
    )(qn, qr, *past, *diag)


def _neg_suffix_matrix(n):
    j = lax.broadcasted_iota(jnp.int32, (n, n), 0)
    s = lax.broadcasted_iota(jnp.int32, (n, n), 1)
    return jnp.where(j >= s, -1.0, 0.0).astype(BF16)


def _sb_kernel(q_ref, kp_ref, vp_ref, kd_ref, vd_ref, o_ref, kb_ref, vb_ref, q_scr, run_scr, acc_scr, *, tq, tk, off):
    qi = pl.program_id(1)

    @pl.when(qi == 0)
    def _():
        kb_ref[...] = kp_ref[0].astype(BF16)
        vb_ref[...] = vp_ref[0].astype(BF16)

    nfull = (off + qi * tq) // tk
    lane = lax.broadcasted_iota(jnp.int32, (1, HEAD_W), 1)
    nh = SB_HEADS
    dmask = (lax.broadcasted_iota(jnp.int32, (1, tq), 1)
             < lax.broadcasted_iota(jnp.int32, (nh * tq, 1), 0) % tq)
    t_diag = _neg_suffix_matrix(tq)
    t_full = t_diag if tk == tq else _neg_suffix_matrix(tk)
    q = q_ref[0]

    def tile(k, v, mask, tmat, first):
        zz = _dot_nt(q_scr[...], k)
        sp = jnp.maximum(zz, 0.0) + jnp.log(1.0 + jnp.exp(-jnp.abs(zz)))
        if mask is not None:
            sp = jnp.where(mask, sp, 0.0)
        hi = sp.astype(BF16)
        lo = (sp - hi.astype(F32)).astype(BF16)
        incl = _dot(hi, tmat) + _dot(lo, tmat)
        if first:
            w = jnp.exp(zz + incl)
            run = jnp.zeros((nh * tq, 128), F32)
        else:
            run = run_scr[...]
            w = jnp.exp(zz + incl + (run if zz.shape[1] == 128 else jnp.concatenate([run] * (zz.shape[1] // 128), -1)))
        if mask is not None:
            w = jnp.where(mask, w, 0.0)
        pv = _dot(w.astype(BF16), v)
        run = run - jnp.sum(sp, axis=-1, keepdims=True)
        run_scr[...] = run
        if first:
            acc_scr[...] = pv
        else:
            acc_scr[...] += pv
        return jnp.max(run)

    for h in range(nh):
        q_scr[h * tq:(h + 1) * tq, :] = jnp.where((lane >> 6) == h, q, jnp.zeros_like(q))
    top = tile(kd_ref[0].astype(BF16), vd_ref[0].astype(BF16), dmask, t_diag, True)

    def more(state):
        j, top = state
        return jnp.logical_and(j >= 0, top > SB_LOG_ZERO)

    def body(state):
        j, _ = state
        ks = pl.ds(pl.multiple_of(j * tk, tk), tk)
        return j - 1, tile(kb_ref[ks, :], vb_ref[ks, :], None, t_full, False)

    lax.while_loop(more, body, (nfull - 1, top))
    out = jnp.zeros((tq, HEAD_W), F32)
    for h in range(nh):
        out = jnp.where((lane >> 6) == h, acc_scr[h * tq:(h + 1) * tq, :], out)
    o_ref[0] = out.astype(o_ref.dtype)


def _sb(sq, past, diag, layer, off):
    b, l, _ = sq.shape
    lp = past[0].shape[2]
    tq = min(l, ATT_TILE)
    qblk = pl.BlockSpec((1, tq, HEAD_W), lambda i, t: (i, t, 0))
    dblk = pl.BlockSpec((None, 1, tq, HEAD_W), lambda i, t: (layer, i, t, 0))
    kblk = pl.BlockSpec((None, 1, lp, HEAD_W), lambda i, t: (layer, i, 0, 0))
    kern = functools.partial(_sb_kernel, tq=tq, tk=ATT_TILE, off=off)
    return pl.pallas_call(
        kern,
        out_shape=jax.ShapeDtypeStruct((b, l, HEAD_W), BF16),
        grid=(b, l // tq),
        in_specs=[qblk, kblk, kblk, dblk, dblk],
        out_specs=qblk,
        scratch_shapes=[pltpu.VMEM((lp, HEAD_W), BF16), pltpu.VMEM((lp, HEAD_W), BF16),
                        pltpu.VMEM((SB_HEADS * tq, HEAD_W), BF16), pltpu.VMEM((SB_HEADS * tq, 128), F32),
                        pltpu.VMEM((SB_HEADS * tq, HEAD_W), F32)],
        compiler_params=_params("parallel", "arbitrary"),
        name="sb_attn",
    )(sq, *past, *diag)


def _group_mean(x, avg):
    hi = x.astype(BF16)
    lo = (x - hi.astype(F32)).astype(BF16)
    return _dot(hi, avg) + _dot(lo, avg)


def _ret_kernel(q_ref, k_ref, v_ref, g_ref, s0_ref, gn_ref, dec_ref, qd_ref, kd_ref, sd_ref,
                o_ref, s_ref, *, c):
    n = q_ref.shape[1] // c
    lane = lax.broadcasted_iota(jnp.int32, (1, HEAD_W), 1)
    krow = lax.broadcasted_iota(jnp.int32, (HEAD_W, 1), 0)
    bd_mask = ((krow & 127) >> 5) == (lane >> 6)
    avg = jnp.where((krow >> 6) == (lane >> 6), 1.0 / RET_DV, 0.0).astype(BF16)
    s_ref[0] = s0_ref[0]

    def step(i, carry):
        rs = pl.ds(pl.multiple_of(i * c, c), c)
        q = q_ref[0, rs, :]
        kf = k_ref[0, rs, :]
        k = kf.astype(BF16)
        v = v_ref[0, rs, :]
        state = s_ref[0]
        o = _dot(q, state.astype(BF16)) * qd_ref[...]
        qs = jnp.concatenate([jnp.where(((lane & 127) >> 5) == h, q, jnp.zeros_like(q)) for h in range(RET_HEADS)],
                             axis=0)
        pv = _dot((_dot_nt(qs, k) * dec_ref[...]).astype(BF16), v)
        for h in range(RET_HEADS):
            o = o + jnp.where((lane >> 6) == h, pv[h * c:(h + 1) * c], 0.0)
        upd = _dot_tn((kf * kd_ref[...]).astype(BF16), v)
        s_ref[0] = sd_ref[...] * state + jnp.where(bd_mask, upd, 0.0)
        dlt = o - _group_mean(o, avg)
        ro = dlt * lax.rsqrt(_group_mean(dlt * dlt, avg) + EPS) * gn_ref[...]
        o_ref[0, rs, :] = (_silu(g_ref[0, rs, :]) * ro).astype(o_ref.dtype)
        return carry

    _fori_pairs(n, step, 0)


def _ret_consts(c):
    lg = jnp.log(1.0 - 2.0 ** (-5.0 - jnp.arange(RET_HEADS, dtype=F32)))
    i = jnp.arange(c, dtype=F32)
    rel = i[:, None] - i[None, :]
    dec = jnp.where(rel >= 0, jnp.exp(lg[:, None, None] * jnp.maximum(rel, 0.0)), 0.0)
    v_head = jnp.arange(HEAD_W) // RET_DV
    k_head = (jnp.arange(HEAD_W) % 128) // (RET_DK // 2)
    qd = jnp.exp(lg[None, v_head] * (i[:, None] + 1.0))
    kd = jnp.exp(lg[None, k_head] * (c - 1.0 - i[:, None]))
    sd = jnp.exp(lg * c)[v_head][None, :]
    return dec, qd, kd, sd


def _ret(rq, rk, rv, rg, s0, gn):
    b, l, _ = rq.shape
    c = min(l, RET_CHUNK)
    dec, qd, kd, sd = _ret_consts(c)
    dec = dec.reshape(RET_HEADS * c, c)
    blk = pl.BlockSpec((1, l, HEAD_W), lambda i: (i, 0, 0))
    sblk = pl.BlockSpec((1, HEAD_W, HEAD_W), lambda i: (i, 0, 0))
    return pl.pallas_call(
        functools.partial(_ret_kernel, c=c),
        out_shape=[jax.ShapeDtypeStruct((b, l, HEAD_W), BF16), jax.ShapeDtypeStruct((b, HEAD_W, HEAD_W), F32)],
        grid=(b,),
        in_specs=[blk, blk, blk, blk, sblk, _const_spec(gn.shape), _const_spec(dec.shape), _const_spec(qd.shape),
                  _const_spec(kd.shape), _const_spec(sd.shape)],
        out_specs=[blk, sblk],
        compiler_params=_params("parallel"),
        name="retention",
    )(rq, rk, rv, rg, s0, gn, dec, qd, kd, sd)


def _ffn_kernel(mla_ref, ret_ref, sb_ref, x_ref, mod_ref, prev_ref, wo_ref, g2_ref, wup_ref, cw_ref, cb_ref,
                wdn_ref, gf_ref, o_ref, st_ref, a_scr, carry, *, final_norm):
    bb, tl, d = x_ref.shape
    rows = bb * tl
    t = pl.program_id(1)
    nt = pl.num_programs(1)

    @pl.when(t == 0)
    def _():
        carry[:, 6:8, :] = prev_ref[...]

    mod = mod_ref[...]
    cat = jnp.concatenate([mla_ref[...].reshape(rows, 512), ret_ref[...].reshape(rows, HEAD_W),
                           sb_ref[...].reshape(rows, HEAD_W)], axis=-1)
    x1 = x_ref[...] + mod[:, 2:3, :] * _dot(cat, wo_ref[...]).reshape(bb, tl, d)
    h = (_rms(x1) * g2_ref[...] * (1.0 + mod[:, 4:5, :]) + mod[:, 3:4, :]).reshape(rows, d).astype(BF16)
    f = jnp.zeros((rows, d), F32)
    for c0, c1 in FF_CHUNKS:
        cs, n = slice(c0, c1), c1 - c0
        a = _dot(h, wup_ref[:, cs]).reshape(bb, tl, n)
        b = _dot(h, wup_ref[:, D_FF + c0:D_FF + c1])
        a_scr[:, 8:, :n] = a
        a_scr[:, 6:8, :n] = carry[:, 6:8, cs]
        cw = cw_ref[...]
        conv = (cb_ref[:, cs] + cw[0:1, cs] * a_scr[:, 6:6 + tl, :n] + cw[1:2, cs] * a_scr[:, 7:7 + tl, :n]
                + cw[2:3, cs] * a)
        carry[:, 6:8, cs] = a_scr[:, tl + 6:tl + 8, :n]
        y = (_silu(conv).reshape(rows, n) * b).astype(BF16)
        f = f + _dot(y, wdn_ref[cs, :])
    x2 = x1 + mod[:, 5:6, :] * f.reshape(bb, tl, d)
    if final_norm:
        x2 = _rms(x2) * gf_ref[...]
    o_ref[...] = x2

    @pl.when(t == nt - 1)
    def _():
        st_ref[...] = carry[:, 6:8, :]


def _ffn(mla, ret, sb, x, mod, prev, wo, g2, wup, cw, cb, wdn, gf, final_norm):
    b, l, d = x.shape
    tl = min(l, ROW_TILE)
    bb = ROW_TILE // tl
    tok = lambda w: pl.BlockSpec((bb, tl, w), lambda i, t: (i, t, 0))
    st = pl.BlockSpec((bb, CONV_W - 1, D_FF), lambda i, t: (i, 0, 0))
    once = lambda a: pl.BlockSpec(a.shape, lambda i, t: (0, 0), pipeline_mode=pl.Buffered(1))
    return pl.pallas_call(
        functools.partial(_ffn_kernel, final_norm=final_norm),
        out_shape=[jax.ShapeDtypeStruct((b, l, d), F32), jax.ShapeDtypeStruct((b, CONV_W - 1, D_FF), F32)],
        grid=(b // bb, l // tl),
        in_specs=[tok(512), tok(HEAD_W), tok(HEAD_W), tok(d), pl.BlockSpec((bb, 6, d), lambda i, t: (i, 0, 0)), st,
                  once(wo), _const_spec(g2.shape), once(wup), _const_spec(cw.shape), _const_spec(cb.shape),
                  once(wdn), _const_spec(gf.shape)],
        out_specs=[tok(d), st],
        scratch_shapes=[pltpu.VMEM((bb, tl + 8, max(c1 - c0 for c0, c1 in FF_CHUNKS)), F32),
                        pltpu.VMEM((bb, 8, D_FF), F32)],
        compiler_params=_params("parallel", "arbitrary"),
        name="conv_ffn",
    )(mla, ret, sb, x, mod, prev, wo, g2, wup, cw, cb, wdn, gf)


def _half_split(w, heads):
    lead = w.shape[:-1]
    dim = w.shape[-1] // heads
    w = w.reshape(lead + (heads, 2, dim // 2))
    return jnp.swapaxes(w, -3, -2).reshape(lead + (heads * dim,))


def _permute_win(w):
    half = MLA_ROPE // 2
    kr = w[..., 640:672]
    kr_swapped = jnp.concatenate([kr[..., half:], kr[..., :half]], axis=-1)
    tiled = lambda a: jnp.tile(a, (1, 1, 4))
    return jnp.concatenate([w[..., 0:640], _half_split(w[..., 672:928], RET_HEADS),
                            _half_split(w[..., 928:1184], RET_HEADS), w[..., 1184:2464],
                            tiled(kr), tiled(kr_swapped)], axis=-1)


def _permute_wuq(w):
    half = MLA_ROPE // 2
    w = w.reshape(w.shape[:-1] + (MLA_HEADS, MLA_NOPE + MLA_ROPE))
    flat = lambda a: a.reshape(a.shape[:-2] + (-1,))
    rope = w[..., MLA_NOPE:]
    swapped = jnp.concatenate([rope[..., half:], rope[..., :half]], axis=-1)
    return jnp.concatenate([flat(w[..., :MLA_NOPE]), flat(rope), flat(swapped)], axis=-1)


def _permute_wukv(w):
    w = w.reshape(w.shape[:-1] + (MLA_HEADS, MLA_NOPE + MLA_V))
    flat = lambda a: a.reshape(a.shape[:-2] + (-1,))
    return jnp.concatenate([flat(w[..., :MLA_NOPE]), flat(w[..., MLA_NOPE:])], axis=-1)


def _ret_state_in(st):
    b = st.shape[0]
    st = st.reshape(b, RET_HEADS, 2, RET_DK // 2, RET_DV).transpose(0, 2, 1, 3, 4)
    eye = jnp.eye(RET_HEADS, dtype=st.dtype)
    full = st[:, :, :, :, None, :] * eye[None, None, :, None, :, None]
    return full.reshape(b, HEAD_W, HEAD_W)


def _ret_state_out(s):
    b = s.shape[0]
    s = s.reshape(b, 2, RET_HEADS, RET_DK // 2, RET_HEADS, RET_DV)
    blocks = [s[:, :, h, :, h, :].reshape(b, RET_DK, RET_DV) for h in range(RET_HEADS)]
    return jnp.stack(blocks, axis=1)


def _rope_angles(pos, dim):
    inv = ROPE_BASE ** (-jnp.arange(0, dim, 2, dtype=F32) / dim)
    return pos.astype(F32)[:, None] * inv[None, :]


def _rope_tables(pos, dim, reps):
    ang = _rope_angles(pos, dim)
    return jnp.tile(jnp.cos(ang), (1, reps)), jnp.tile(jnp.sin(ang), (1, reps))


def _rope_tables_swapped(pos, dim, reps):
    ang = _rope_angles(pos, dim)
    c, s = jnp.cos(ang), jnp.sin(ang)
    return jnp.tile(jnp.concatenate([c, c], axis=1), (1, reps)), jnp.tile(jnp.concatenate([-s, s], axis=1), (1, reps))


def _trunk(x, mods, pos0, cache, w):
    b, l, _ = x.shape
    depth = len(mods)
    pos = pos0 + jnp.arange(l)
    cm, sm = _rope_tables_swapped(pos, MLA_ROPE, MLA_HEADS)
    cr, sr = _rope_tables(pos, RET_DK, RET_HEADS)
    if cache is None:
        off = 0
    else:
        off = cache["lat"].shape[2]
        krt_cache = jnp.tile(cache["kr"], (1, 1, 1, 4)).astype(BF16)
        sb_cache = (cache["sk"].reshape(depth, b, off, HEAD_W), cache["sv"].reshape(depth, b, off, HEAD_W))
    stacked = tuple(jnp.zeros((depth, b, l, wd), F32) for _, wd, _, st in _INPROJ_OUTPUTS if st)
    small = [[], []]
    for layer in range(depth):
        mod = mods[layer]
        p = _inproj(x, mod, w["g_norm1"][layer], w["w_in"][layer], w["g_q_norm"][layer], w["w_uq"][layer],
                    w["g_kv_norm"][layer], w["w_ukv"][layer], cm, sm, cr, sr, layer, depth, stacked)
        stacked = (p["lat"], p["kr"], p["sk"], p["sv"])
        sb_new = (p["sk"], p["sv"])
        if cache is None:
            mla_past, sb_past = (p["kp"], p["v"]), sb_new
            s0 = jnp.zeros((b, HEAD_W, HEAD_W), F32)
            prev = jnp.zeros((b, CONV_W - 1, D_FF), F32)
        else:
            mla_past = _kvup(cache["lat"], layer, krt_cache[layer], w["w_ukv"][layer])
            sb_past = sb_cache
            s0 = _ret_state_in(cache["S"][layer])
            prev = cache["conv"][layer]
        mla = _mla(p["qn"], p["qr"], mla_past, (p["kp"], p["v"]), off)
        sb = _sb(p["sq"], sb_past, sb_new, layer, off)
        ret, s_new = _ret(p["rq"], p["rk"], p["rv"], p["rg"], s0, w["g_ret_norm"][layer])
        x, conv_state = _ffn(mla, ret, sb, x, mod, prev, w["w_o"][layer], w["g_norm2"][layer], w["w_up"][layer],
                             w["conv_w"][layer], w["conv_b"][layer], w["w_down"][layer], w["g_final"],
                             layer == depth - 1)
        small[0].append(_ret_state_out(s_new))
        small[1].append(conv_state)
    lat, kr, sk, sv = stacked
    return x, [lat, kr, sk.reshape(depth, b, l, SB_HEADS, SB_DK), sv.reshape(depth, b, l, SB_HEADS, SB_DV),
               jnp.stack(small[0], axis=0), jnp.stack(small[1], axis=0)]


def kernel(x_prompt, x_sample, c_prompt, c_sample, cache_mla_latent, cache_mla_krope, cache_sb_k, cache_sb_v, state_ret, state_ffn_conv, w_in, g_q_norm, w_uq, g_kv_norm, w_ukv, g_ret_norm, w_o, w_up, conv_w, conv_b, w_down, g_norm1, g_norm2, w_ada, b_ada, g_final):
    depth = w_in.shape[0]
    bp = x_prompt.shape[0]
    row = lambda g: g.reshape(g.shape[0], 1, g.shape[-1])
    w = dict(
        w_in=_permute_win(w_in).astype(BF16),
        w_uq=_permute_wuq(w_uq).astype(BF16),
        w_ukv=_permute_wukv(w_ukv).astype(BF16),
        w_o=w_o.astype(BF16), w_up=w_up.astype(BF16), w_down=w_down.astype(BF16),
        g_q_norm=row(g_q_norm), g_kv_norm=row(g_kv_norm), g_ret_norm=row(g_ret_norm),
        g_norm1=row(g_norm1), g_norm2=row(g_norm2), conv_w=conv_w, conv_b=row(conv_b),
        g_final=g_final.reshape(1, -1))
    mod = _ada(jnp.concatenate([c_prompt, c_sample], axis=0), w_ada, b_ada)
    mod = mod.reshape(depth, mod.shape[1], 6, D_MODEL)
    y_p, st_p = _trunk(x_prompt, [mod[l, :bp] for l in range(depth)], 0, None, w)
    cache = dict(lat=cache_mla_latent, kr=cache_mla_krope, sk=cache_sb_k, sv=cache_sb_v, S=state_ret,
                 conv=state_ffn_conv)
    y_s, st_s = _trunk(x_sample, [mod[l, bp:] for l in range(depth)], cache_mla_latent.shape[2], cache, w)
    return (y_p, y_s, *st_p, *st_s)
```

```python
import functools

import numpy as np
import jax
import jax.numpy as jnp
from jax import lax
from jax.experimental import pallas as pl
from jax.experimental.pallas import tpu as pltpu

D_MODEL = 1024
CHUNK = 64
MLA_HEADS = 8
MLA_NOPE = 64
MLA_ROPE = 32
MLA_V = 64
MLA_Q_RANK = 384
MLA_KV_RANK = 256
RET_HEADS = 4
RET_DK = 64
RET_DV = 64
SB_HEADS = 4
SB_DK = 64
SB_DV = 64
D_FF = 2816
CONV_W = 3
ROPE_BASE = 10000.0
EPS = 1e-6

HEAD_W = 256
ROW_TILE = 512
ATT_TILE = 256
RET_CHUNK = 256
FF_CHUNKS = ((0, 1536), (1536, D_FF))
SB_LOG_ZERO = -104.0
VMEM_LIMIT = 56 * 1024 * 1024

F32 = jnp.float32
BF16 = jnp.bfloat16

_ZQ, _ZKV, _RQ, _RK, _RV, _RG, _SQ, _SK, _SV, _KRT, _IN_COLS_P = (
    0, 384, 640, 896, 1152, 1408, 1664, 1920, 2176, 2432, 2688)


def _dot(a, b):
    return jnp.dot(a, b, preferred_element_type=F32)


def _dot_nt(a, b):
    return lax.dot_general(a, b, (((1,), (1,)), ((), ())), preferred_element_type=F32)


def _dot_tn(a, b):
    return lax.dot_general(a, b, (((0,), (0,)), ((), ())), preferred_element_type=F32)


def _rms(x):
    return x * lax.rsqrt(jnp.mean(x * x, axis=-1, keepdims=True) + EPS)


def _silu(x):
    return x / (1.0 + jnp.exp(-x))


def _params(*sem):
    return pltpu.CompilerParams(dimension_semantics=sem, vmem_limit_bytes=VMEM_LIMIT)


def _const_spec(shape):
    nd = len(shape)
    return pl.BlockSpec(shape, lambda *_: (0,) * nd)


def _ada_kernel(c_ref, w_ref, b_ref, o_ref):
    a = _silu(c_ref[...]).astype(BF16)
    o_ref[0] = _dot(a, w_ref[0].astype(BF16)) + b_ref[0]


def _ada(c_all, w_ada, b_ada):
    depth, d, n = w_ada.shape
    rows = c_all.shape[0]
    tn = 1536
    return pl.pallas_call(
        _ada_kernel,
        out_shape=jax.ShapeDtypeStruct((depth, rows, n), F32),
        grid=(depth, n // tn),
        in_specs=[pl.BlockSpec((rows, d), lambda l, j: (0, 0)),
                  pl.BlockSpec((1, d, tn), lambda l, j: (l, 0, j)),
                  pl.BlockSpec((1, 1, tn), lambda l, j: (l, 0, j))],
        out_specs=pl.BlockSpec((1, rows, tn), lambda l, j: (l, 0, j)),
        compiler_params=_params("parallel", "parallel"),
        name="ada_mod",
    )(c_all, w_ada, b_ada.reshape(depth, 1, n))


_INPROJ_INPUTS = 13
_INPROJ_OUTPUTS = (("qn", 512, BF16, False), ("qr", 256, BF16, False), ("lat", MLA_KV_RANK, F32, True),
                   ("kp", 1024, BF16, False), ("vt", None, BF16, False), ("kr", MLA_ROPE, F32, True),
                   ("rq", HEAD_W, BF16, False), ("rk", HEAD_W, F32, False), ("rv", HEAD_W, BF16, False),
                   ("rg", HEAD_W, F32, False), ("sq", HEAD_W, BF16, False), ("sk", HEAD_W, F32, True),
                   ("sv", HEAD_W, F32, True))


def _inproj_kernel(*refs):
    (x_ref, mod_ref, g1_ref, win_ref, gq_ref, wuq_ref, gkv_ref, wuk_ref, wvt_ref, cm_ref, sm_ref, cr_ref,
     sr_ref) = refs[:_INPROJ_INPUTS]
    out = dict(zip([o[0] for o in _INPROJ_OUTPUTS], refs[len(refs) - len(_INPROJ_OUTPUTS):]))
    bb, tl, d = x_ref.shape
    rows = bb * tl
    mod = mod_ref[...]
    h = _rms(x_ref[...]) * g1_ref[...] * (1.0 + mod[:, 1:2, :]) + mod[:, 0:1, :]
    z = _dot(h.reshape(rows, d).astype(BF16), win_ref[...])

    def put(name, val, lanes=slice(None)):
        ref = out[name]
        lead = (0,) if len(ref.shape) == 4 else ()
        ref[lead + (slice(None), slice(None), lanes)] = val.reshape(bb, tl, val.shape[-1]).astype(ref.dtype)

    def rope(x1, x2, c_ref, s_ref):
        c, s = c_ref[...][None], s_ref[...][None]
        x1 = x1.reshape(bb, tl, x1.shape[-1])
        x2 = x2.reshape(bb, tl, x2.shape[-1])
        return x1 * c - x2 * s, x1 * s + x2 * c

    def rope_swapped(a, b, width):
        a = a.reshape(bb, tl, width)
        b = b.reshape(bb, tl, width)
        return a * cm_ref[:, :width][None] + b * sm_ref[:, :width][None]

    q = _dot((_rms(z[:, _ZQ:_ZKV]) * gq_ref[...]).astype(BF16), wuq_ref[...])
    put("qn", q[:, :512])
    put("qr", rope_swapped(q[:, 512:768], q[:, 768:1024], 256))
    lat = _rms(z[:, _ZKV:_RQ]) * gkv_ref[...]
    put("lat", lat)
    kr = rope_swapped(z[:, _KRT:_KRT + 128], z[:, _KRT + 128:_KRT + 256], 128)
    put("kr", kr[:, :, :MLA_ROPE])
    lat_b = lat.astype(BF16)
    kn = _dot(lat_b, wuk_ref[...])
    for p in range(MLA_HEADS // 2):
        put("kp", kn[:, 128 * p:128 * (p + 1)], slice(256 * p, 256 * p + 128))
        put("kp", kr, slice(256 * p + 128, 256 * (p + 1)))
    vt_ref = out["vt"]
    tkk = vt_ref.shape[-1]
    for bi in range(bb):
        for kk in range(tl // tkk):
            r0 = bi * tl + kk * tkk
            vt_ref[bi, kk] = _dot_nt(wvt_ref[...], lat_b[r0:r0 + tkk]).astype(vt_ref.dtype)
    a1, a2 = rope(z[:, _RQ:_RQ + 128], z[:, _RQ + 128:_RK], cr_ref, sr_ref)
    put("rq", a1 * (RET_DK ** -0.5), slice(0, 128))
    put("rq", a2 * (RET_DK ** -0.5), slice(128, 256))
    b1, b2 = rope(z[:, _RK:_RK + 128], z[:, _RK + 128:_RV], cr_ref, sr_ref)
    put("rk", b1, slice(0, 128))
    put("rk", b2, slice(128, 256))
    put("rv", z[:, _RV:_RG])
    put("rg", z[:, _RG:_SQ])
    put("sq", z[:, _SQ:_SK] * (SB_DK ** -0.5))
    put("sk", z[:, _SK:_SV])
    put("sv", z[:, _SV:_KRT])


def _inproj(x, mod, g1, win, gq, wuq, gkv, wuk, wvt, cm, sm, cr, sr, layer, depth, stacked):
    b, l, d = x.shape
    tl = min(l, ROW_TILE)
    bb = ROW_TILE // tl
    grid = (b // bb, l // tl)
    tok = lambda w: pl.BlockSpec((bb, tl, w), lambda i, t: (i, t, 0))
    stk = lambda w: pl.BlockSpec((1, bb, tl, w), lambda i, t: (layer, i, t, 0))
    pos = pl.BlockSpec((tl, 128), lambda i, t: (t, 0))
    pos2 = pl.BlockSpec((tl, 256), lambda i, t: (t, 0))
    stacked_idx = [k for k, o in enumerate(_INPROJ_OUTPUTS) if o[3]]
    extra = list(stacked)
    tkk = min(l, ATT_TILE)
    vt_shape = jax.ShapeDtypeStruct((b, l // tkk, MLA_HEADS * MLA_V, tkk), BF16)
    vt_spec = pl.BlockSpec((bb, tl // tkk, MLA_HEADS * MLA_V, tkk), lambda i, t: (i, t, 0, 0))
    res = pl.pallas_call(
        _inproj_kernel,
        out_shape=[vt_shape if w is None else jax.ShapeDtypeStruct((depth, b, l, w) if st else (b, l, w), dt)
                   for _, w, dt, st in _INPROJ_OUTPUTS],
        grid=grid,
        in_specs=[tok(d), pl.BlockSpec((bb, 6, d), lambda i, t: (i, 0, 0)), _const_spec(g1.shape),
                  _const_spec(win.shape), _const_spec(gq.shape), _const_spec(wuq.shape), _const_spec(gkv.shape),
                  _const_spec(wuk.shape), _const_spec(wvt.shape), pos2, pos2, pos, pos]
        + [pl.BlockSpec(memory_space=pl.ANY)] * len(extra),
        out_specs=[vt_spec if w is None else (stk(w) if st else tok(w)) for _, w, _, st in _INPROJ_OUTPUTS],
        input_output_aliases={_INPROJ_INPUTS + k: o for k, o in enumerate(stacked_idx)},
        compiler_params=_params("parallel", "parallel"),
        name="in_proj",
    )(x, mod, g1, win, gq, wuq, gkv, wuk, wvt, cm, sm, cr, sr, *extra)
    return dict(zip([o[0] for o in _INPROJ_OUTPUTS], res))


def _kvup_kernel(lat_ref, krt_ref, wuk_ref, wvt_ref, kp_ref, vt_ref):
    lat_b = lat_ref[0].astype(BF16)
    kn = _dot(lat_b, wuk_ref[...])
    for p in range(MLA_HEADS // 2):
        kp_ref[0, :, 256 * p:256 * p + 128] = kn[:, 128 * p:128 * (p + 1)].astype(kp_ref.dtype)
        kp_ref[0, :, 256 * p + 128:256 * (p + 1)] = krt_ref[0]
    tkk = vt_ref.shape[-1]
    for kk in range(vt_ref.shape[1]):
        vt_ref[0, kk] = _dot_nt(wvt_ref[...], lat_b[kk * tkk:(kk + 1) * tkk]).astype(vt_ref.dtype)


def _kvup(lat, layer, krt, wuk, wvt):
    _, b, lk, r = lat.shape
    nv = MLA_HEADS * MLA_V
    blk = lambda w: pl.BlockSpec((1, lk, w), lambda i: (i, 0, 0))
    return pl.pallas_call(
        _kvup_kernel,
        out_shape=(jax.ShapeDtypeStruct((b, lk, 1024), BF16),
                   jax.ShapeDtypeStruct((b, lk // ATT_TILE, nv, ATT_TILE), BF16)),
        grid=(b,),
        in_specs=[pl.BlockSpec((None, 1, lk, r), lambda i: (layer, i, 0, 0)), blk(128), _const_spec(wuk.shape),
                  _const_spec(wvt.shape)],
        out_specs=[blk(1024), pl.BlockSpec((1, lk // ATT_TILE, nv, ATT_TILE), lambda i: (i, 0, 0, 0))],
        compiler_params=_params("parallel"),
        name="kv_up",
    )(lat, krt, wuk, wvt)


def _fori_pairs(n, body, init):
    carry = lax.fori_loop(0, n // 2, lambda i, c: body(2 * i + 1, body(2 * i, c)), init)
    return lax.fori_loop(0, n % 2, lambda i, c: body(n - 1, c), carry)


def _fold(op, x):
    w = x.shape[1]
    parts = [x[:, i:i + 128] for i in range(0, w, 128)]
    out = parts[0]
    for p in parts[1:]:
        out = op(out, p)
    return out


def _fold8(reduce, x):
    return reduce(x.reshape(x.shape[0] // 8, 8, x.shape[1]), axis=0)


def _mla_kernel(qn_ref, qr_ref, kp_ref, vtp_ref, kd_ref, vtd_ref, o_ref, s_scr, q_scr, mx_scr, m_scr, l_scr, acc_scr,
                *, tq, tk, off, scale, fused):
    qi = pl.program_id(1)
    nfull = (off + qi * tq) // tk
    ntile = nfull + 1 if fused else nfull
    lane = lax.broadcasted_iota(jnp.int32, (1, 128), 1)
    key_c = lax.broadcasted_iota(jnp.int32, (tq, 1), 0) // CHUNK
    qry_c = lax.broadcasted_iota(jnp.int32, (1, tq), 1) // CHUNK
    dmask = key_c <= qry_c
    c = scale * 1.4426950408889634
    neg = -1e30
    lane_v = lax.broadcasted_iota(jnp.int32, (1, HEAD_W), 1)
    feat = lax.broadcasted_iota(jnp.int32, (MLA_V, 1), 0)
    for g in range(MLA_HEADS // 4):
        gs = slice(HEAD_W * g, HEAD_W * (g + 1))
        qr = qr_ref[0, :, 128 * g:128 * (g + 1)]
        for hh in range(4):
            qn = qn_ref[0, :, 128 * (2 * g + hh // 2):128 * (2 * g + hh // 2 + 1)]
            q_scr[hh, :, 0:128] = jnp.where((lane >> 6) == hh % 2, qn, jnp.zeros_like(qn))
            q_scr[hh, :, 128:256] = jnp.where((lane >> 5) == hh, qr, jnp.zeros_like(qr))

        def head_scores(keys_of_pair, hh):
            return _dot_nt(keys_of_pair(2 * g + hh // 2), q_scr[hh])

        def vt_rows(hh):
            return slice(MLA_V * (4 * g + hh), MLA_V * (4 * g + hh + 1))

        def scores(j, carry):
            ks = pl.ds(pl.multiple_of(j * tk, tk), tk)
            for hh in range(4):
                s = head_scores(lambda p: kp_ref[0, ks, 256 * p:256 * (p + 1)], hh)
                s_scr[j, hh] = s
                mx_scr[hh] = jnp.maximum(mx_scr[hh], _fold8(jnp.max, s))
            return carry

        own = []
        for hh in range(4):
            if fused:
                ks_d = pl.ds(pl.multiple_of(nfull * tk, tk), tk)
                s_d = jnp.where(dmask, head_scores(lambda p: kp_ref[0, ks_d, 256 * p:256 * (p + 1)], hh), neg)
                s_scr[nfull, hh] = s_d
            else:
                s_d = jnp.where(dmask, head_scores(lambda p: kd_ref[0, :, 256 * p:256 * (p + 1)], hh), neg)
                own.append(s_d)
            mx_scr[hh] = _fold8(jnp.max, s_d)
        _fori_pairs(nfull, scores, 0)
        for hh in range(4):
            m_scr[hh] = jnp.broadcast_to(jnp.max(mx_scr[hh], axis=0, keepdims=True), (8, tq))

        def values(j, carry):
            for hh in range(4):
                p = jnp.exp2((s_scr[j, hh] - m_scr[hh, 0:1, :]) * c)
                l_scr[hh] += _fold8(jnp.sum, p)
                acc_scr[hh] += _dot(vtp_ref[0, j, vt_rows(hh), :], p.astype(BF16))
            return carry

        for hh in range(4):
            if fused:
                l_scr[hh] = jnp.zeros((8, tq), F32)
                acc_scr[hh] = jnp.zeros((MLA_V, tq), F32)
            else:
                p = jnp.exp2((own[hh] - m_scr[hh, 0:1, :]) * c)
                l_scr[hh] = _fold8(jnp.sum, p)
                acc_scr[hh] = _dot(vtd_ref[0, 0, vt_rows(hh), :], p.astype(BF16))
        _fori_pairs(ntile, values, 0)
        out_g = jnp.zeros((tq, HEAD_W), F32)
        for hh in range(4):
            inv = 1.0 / jnp.sum(l_scr[hh], axis=0, keepdims=True)
            place = jnp.where(feat + MLA_V * hh == lane_v, 1.0, 0.0).astype(BF16)
            out_g = out_g + _dot_tn((acc_scr[hh] * inv).astype(BF16), place)
        o_ref[0, :, gs] = out_g.astype(o_ref.dtype)


def _mla(qn, qr, past, diag, off):
    b, l, _ = qn.shape
    lp = past[0].shape[1]
    tq = min(l, ATT_TILE)
    tk = ATT_TILE
    nv = MLA_HEADS * MLA_V
    fused = off == 0 and tq == tk
    qblk = lambda w: pl.BlockSpec((1, tq, w), lambda i, t: (i, t, 0))
    nmax = (off + l - tq) // tk + (1 if fused else 0)
    kern = functools.partial(_mla_kernel, tq=tq, tk=tk, off=off, scale=(MLA_NOPE + MLA_ROPE) ** -0.5, fused=fused)
    return pl.pallas_call(
        kern,
        out_shape=jax.ShapeDtypeStruct((b, l, 512), BF16),
        grid=(b, l // tq),
        in_specs=[qblk(512), qblk(256), pl.BlockSpec((1, lp, 1024), lambda i, t: (i, 0, 0)),
                  pl.BlockSpec((1, lp // tk, nv, tk), lambda i, t: (i, 0, 0, 0)), qblk(1024),
                  pl.BlockSpec((1, 1, nv, tq), lambda i, t: (i, t, 0, 0))],
        out_specs=qblk(512),
        scratch_shapes=[pltpu.VMEM((nmax, 4, tk, tq), F32), pltpu.VMEM((4, tq, 256), BF16),
                        pltpu.VMEM((4, 8, tq), F32), pltpu.VMEM((4, 8, tq), F32), pltpu.VMEM((4, 8, tq), F32),
                        pltpu.VMEM((4, MLA_V, tq), F32)],
        compiler_params=_params("parallel", "parallel"),
        name="mla_attn",
    )(qn, qr, *past, *diag)


def _neg_suffix_matrix(n):
    j = lax.broadcasted_iota(jnp.int32, (n, n), 0)
    s = lax.broadcasted_iota(jnp.int32, (n, n), 1)
    return jnp.where(j >= s, -1.0, 0.0).astype(BF16)


def _sb_kernel(q_ref, kp_ref, vp_ref, kd_ref, vd_ref, o_ref, kb_ref, vb_ref, q_scr, run_scr, acc_scr, *, tq, tk, off):
    qi = pl.program_id(1)

    @pl.when(qi == 0)
    def _():
        kb_ref[...] = kp_ref[0].astype(BF16)
        vb_ref[...] = vp_ref[0].astype(BF16)

    nfull = (off + qi * tq) // tk
    lane = lax.broadcasted_iota(jnp.int32, (1, HEAD_W), 1)
    nh = SB_HEADS
    dmask = (lax.broadcasted_iota(jnp.int32, (1, tq), 1)
             < lax.broadcasted_iota(jnp.int32, (nh * tq, 1), 0) % tq)
    t_diag = _neg_suffix_matrix(tq)
    t_full = t_diag if tk == tq else _neg_suffix_matrix(tk)
    q = q_ref[0]

    def tile(k, v, mask, tmat, first):
        zz = _dot_nt(q_scr[...], k)
        sp = jnp.maximum(zz, 0.0) + jnp.log(1.0 + jnp.exp(-jnp.abs(zz)))
        if mask is not None:
            sp = jnp.where(mask, sp, 0.0)
        hi = sp.astype(BF16)
        lo = (sp - hi.astype(F32)).astype(BF16)
        incl = _dot(hi, tmat) + _dot(lo, tmat)
        if first:
            w = jnp.exp(zz + incl)
            run = jnp.zeros((nh * tq, 128), F32)
        else:
            run = run_scr[...]
            w = jnp.exp(zz + incl + (run if zz.shape[1] == 128 else jnp.concatenate([run] * (zz.shape[1] // 128), -1)))
        if mask is not None:
            w = jnp.where(mask, w, 0.0)
        pv = _dot(w.astype(BF16), v)
        run = run - jnp.sum(sp, axis=-1, keepdims=True)
        run_scr[...] = run
        if first:
            acc_scr[...] = pv
        else:
            acc_scr[...] += pv
        return jnp.max(run)

    for h in range(nh):
        q_scr[h * tq:(h + 1) * tq, :] = jnp.where((lane >> 6) == h, q, jnp.zeros_like(q))
    top = tile(kd_ref[0].astype(BF16), vd_ref[0].astype(BF16), dmask, t_diag, True)

    def more(state):
        j, top = state
        return jnp.logical_and(j >= 0, top > SB_LOG_ZERO)

    def body(state):
        j, _ = state
        ks = pl.ds(pl.multiple_of(j * tk, tk), tk)
        return j - 1, tile(kb_ref[ks, :], vb_ref[ks, :], None, t_full, False)

    lax.while_loop(more, body, (nfull - 1, top))
    out = jnp.zeros((tq, HEAD_W), F32)
    for h in range(nh):
        out = jnp.where((lane >> 6) == h, acc_scr[h * tq:(h + 1) * tq, :], out)
    o_ref[0] = out.astype(o_ref.dtype)


def _sb(sq, past, diag, layer, off):
    b, l, _ = sq.shape
    lp = past[0].shape[2]
    tq = min(l, ATT_TILE)
    qblk = pl.BlockSpec((1, tq, HEAD_W), lambda i, t: (i, t, 0))
    dblk = pl.BlockSpec((None, 1, tq, HEAD_W), lambda i, t: (layer, i, t, 0))
    kblk = pl.BlockSpec((None, 1, lp, HEAD_W), lambda i, t: (layer, i, 0, 0))
    kern = functools.partial(_sb_kernel, tq=tq, tk=ATT_TILE, off=off)
    return pl.pallas_call(
        kern,
        out_shape=jax.ShapeDtypeStruct((b, l, HEAD_W), BF16),
        grid=(b, l // tq),
        in_specs=[qblk, kblk, kblk, dblk, dblk],
        out_specs=qblk,
        scratch_shapes=[pltpu.VMEM((lp, HEAD_W), BF16), pltpu.VMEM((lp, HEAD_W), BF16),
                        pltpu.VMEM((SB_HEADS * tq, HEAD_W), BF16), pltpu.VMEM((SB_HEADS * tq, 128), F32),
                        pltpu.VMEM((SB_HEADS * tq, HEAD_W), F32)],
        compiler_params=_params("parallel", "arbitrary"),
        name="sb_attn",
    )(sq, *past, *diag)


def _group_mean(x, avg):
    hi = x.astype(BF16)
    lo = (x - hi.astype(F32)).astype(BF16)
    return _dot(hi, avg) + _dot(lo, avg)


def _ret_kernel(q_ref, k_ref, v_ref, g_ref, s0_ref, gn_ref, dec_ref, qd_ref, kd_ref, sd_ref,
                o_ref, s_ref, *, c):
    n = q_ref.shape[1] // c
    lane = lax.broadcasted_iota(jnp.int32, (1, HEAD_W), 1)
    krow = lax.broadcasted_iota(jnp.int32, (HEAD_W, 1), 0)
    bd_mask = ((krow & 127) >> 5) == (lane >> 6)
    avg = jnp.where((krow >> 6) == (lane >> 6), 1.0 / RET_DV, 0.0).astype(BF16)
    s_ref[0] = s0_ref[0]

    def step(i, carry):
        rs = pl.ds(pl.multiple_of(i * c, c), c)
        q = q_ref[0, rs, :]
        kf = k_ref[0, rs, :]
        k = kf.astype(BF16)
        v = v_ref[0, rs, :]
        state = s_ref[0]
        o = _dot(q, state.astype(BF16)) * qd_ref[...]
        qs = jnp.concatenate([jnp.where(((lane & 127) >> 5) == h, q, jnp.zeros_like(q)) for h in range(RET_HEADS)],
                             axis=0)
        pv = _dot((_dot_nt(qs, k) * dec_ref[...]).astype(BF16), v)
        for h in range(RET_HEADS):
            o = o + jnp.where((lane >> 6) == h, pv[h * c:(h + 1) * c], 0.0)
        upd = _dot_tn((kf * kd_ref[...]).astype(BF16), v)
        s_ref[0] = sd_ref[...] * state + jnp.where(bd_mask, upd, 0.0)
        dlt = o - _group_mean(o, avg)
        ro = dlt * lax.rsqrt(_group_mean(dlt * dlt, avg) + EPS) * gn_ref[...]
        o_ref[0, rs, :] = (_silu(g_ref[0, rs, :]) * ro).astype(o_ref.dtype)
        return carry

    _fori_pairs(n, step, 0)


def _ret_consts(c):
    lg = jnp.log(1.0 - 2.0 ** (-5.0 - jnp.arange(RET_HEADS, dtype=F32)))
    i = jnp.arange(c, dtype=F32)
    rel = i[:, None] - i[None, :]
    dec = jnp.where(rel >= 0, jnp.exp(lg[:, None, None] * jnp.maximum(rel, 0.0)), 0.0)
    v_head = jnp.arange(HEAD_W) // RET_DV
    k_head = (jnp.arange(HEAD_W) % 128) // (RET_DK // 2)
    qd = jnp.exp(lg[None, v_head] * (i[:, None] + 1.0))
    kd = jnp.exp(lg[None, k_head] * (c - 1.0 - i[:, None]))
    sd = jnp.exp(lg * c)[v_head][None, :]
    return dec, qd, kd, sd


def _ret(rq, rk, rv, rg, s0, gn):
    b, l, _ = rq.shape
    c = min(l, RET_CHUNK)
    dec, qd, kd, sd = _ret_consts(c)
    dec = dec.reshape(RET_HEADS * c, c)
    blk = pl.BlockSpec((1, l, HEAD_W), lambda i: (i, 0, 0))
    sblk = pl.BlockSpec((1, HEAD_W, HEAD_W), lambda i: (i, 0, 0))
    return pl.pallas_call(
        functools.partial(_ret_kernel, c=c),
        out_shape=[jax.ShapeDtypeStruct((b, l, HEAD_W), BF16), jax.ShapeDtypeStruct((b, HEAD_W, HEAD_W), F32)],
        grid=(b,),
        in_specs=[blk, blk, blk, blk, sblk, _const_spec(gn.shape), _const_spec(dec.shape), _const_spec(qd.shape),
                  _const_spec(kd.shape), _const_spec(sd.shape)],
        out_specs=[blk, sblk],
        compiler_params=_params("parallel"),
        name="retention",
    )(rq, rk, rv, rg, s0, gn, dec, qd, kd, sd)


def _ffn_kernel(mla_ref, ret_ref, sb_ref, x_ref, mod_ref, prev_ref, wo_ref, g2_ref, wup_ref, cw_ref, cb_ref,
                wdn_ref, gf_ref, o_ref, st_ref, a_scr, carry, *, final_norm):
    bb, tl, d = x_ref.shape
    rows = bb * tl
    t = pl.program_id(1)
    nt = pl.num_programs(1)

    @pl.when(t == 0)
    def _():
        carry[:, 6:8, :] = prev_ref[...]

    mod = mod_ref[...]
    cat = jnp.concatenate([mla_ref[...].reshape(rows, 512), ret_ref[...].reshape(rows, HEAD_W),
                           sb_ref[...].reshape(rows, HEAD_W)], axis=-1)
    x1 = x_ref[...] + mod[:, 2:3, :] * _dot(cat, wo_ref[...]).reshape(bb, tl, d)
    h = (_rms(x1) * g2_ref[...] * (1.0 + mod[:, 4:5, :]) + mod[:, 3:4, :]).reshape(rows, d).astype(BF16)
    f = jnp.zeros((rows, d), F32)
    for c0, c1 in FF_CHUNKS:
        cs, n = slice(c0, c1), c1 - c0
        a = _dot(h, wup_ref[:, cs]).reshape(bb, tl, n)
        b = _dot(h, wup_ref[:, D_FF + c0:D_FF + c1])
        a_scr[:, 8:, :n] = a
        a_scr[:, 6:8, :n] = carry[:, 6:8, cs]
        cw = cw_ref[...]
        conv = (cb_ref[:, cs] + cw[0:1, cs] * a_scr[:, 6:6 + tl, :n] + cw[1:2, cs] * a_scr[:, 7:7 + tl, :n]
                + cw[2:3, cs] * a)
        carry[:, 6:8, cs] = a_scr[:, tl + 6:tl + 8, :n]
        y = (_silu(conv).reshape(rows, n) * b).astype(BF16)
        f = f + _dot(y, wdn_ref[cs, :])
    x2 = x1 + mod[:, 5:6, :] * f.reshape(bb, tl, d)
    if final_norm:
        x2 = _rms(x2) * gf_ref[...]
    o_ref[...] = x2

    @pl.when(t == nt - 1)
    def _():
        st_ref[...] = carry[:, 6:8, :]


def _ffn(mla, ret, sb, x, mod, prev, wo, g2, wup, cw, cb, wdn, gf, final_norm):
    b, l, d = x.shape
    tl = min(l, ROW_TILE)
    bb = ROW_TILE // tl
    tok = lambda w: pl.BlockSpec((bb, tl, w), lambda i, t: (i, t, 0))
    st = pl.BlockSpec((bb, CONV_W - 1, D_FF), lambda i, t: (i, 0, 0))
    once = lambda a: pl.BlockSpec(a.shape, lambda i, t: (0, 0), pipeline_mode=pl.Buffered(1))
    return pl.pallas_call(
        functools.partial(_ffn_kernel, final_norm=final_norm),
        out_shape=[jax.ShapeDtypeStruct((b, l, d), F32), jax.ShapeDtypeStruct((b, CONV_W - 1, D_FF), F32)],
        grid=(b // bb, l // tl),
        in_specs=[tok(512), tok(HEAD_W), tok(HEAD_W), tok(d), pl.BlockSpec((bb, 6, d), lambda i, t: (i, 0, 0)), st,
                  once(wo), _const_spec(g2.shape), once(wup), _const_spec(cw.shape), _const_spec(cb.shape),
                  once(wdn), _const_spec(gf.shape)],
        out_specs=[tok(d), st],
        scratch_shapes=[pltpu.VMEM((bb, tl + 8, max(c1 - c0 for c0, c1 in FF_CHUNKS)), F32),
                        pltpu.VMEM((bb, 8, D_FF), F32)],
        compiler_params=_params("parallel", "arbitrary"),
        name="conv_ffn",
    )(mla, ret, sb, x, mod, prev, wo, g2, wup, cw, cb, wdn, gf)


def _half_split(w, heads):
    lead = w.shape[:-1]
    dim = w.shape[-1] // heads
    w = w.reshape(lead + (heads, 2, dim // 2))
    return jnp.swapaxes(w, -3, -2).reshape(lead + (heads * dim,))


def _permute_win(w):
    half = MLA_ROPE // 2
    kr = w[..., 640:672]
    kr_swapped = jnp.concatenate([kr[..., half:], kr[..., :half]], axis=-1)
    tiled = lambda a: jnp.tile(a, (1, 1, 4))
    return jnp.concatenate([w[..., 0:640], _half_split(w[..., 672:928], RET_HEADS),
                            _half_split(w[..., 928:1184], RET_HEADS), w[..., 1184:2464],
                            tiled(kr), tiled(kr_swapped)], axis=-1)


def _permute_wuq(w):
    half = MLA_ROPE // 2
    w = w.reshape(w.shape[:-1] + (MLA_HEADS, MLA_NOPE + MLA_ROPE))
    flat = lambda a: a.reshape(a.shape[:-2] + (-1,))
    rope = w[..., MLA_NOPE:]
    swapped = jnp.concatenate([rope[..., half:], rope[..., :half]], axis=-1)
    return jnp.concatenate([flat(w[..., :MLA_NOPE]), flat(rope), flat(swapped)], axis=-1)


def _permute_wukv(w):
    w = w.reshape(w.shape[:-1] + (MLA_HEADS, MLA_NOPE + MLA_V))
    flat = lambda a: a.reshape(a.shape[:-2] + (-1,))
    return jnp.concatenate([flat(w[..., :MLA_NOPE]), flat(w[..., MLA_NOPE:])], axis=-1)


def _ret_state_in(st):
    b = st.shape[0]
    st = st.reshape(b, RET_HEADS, 2, RET_DK // 2, RET_DV).transpose(0, 2, 1, 3, 4)
    eye = jnp.eye(RET_HEADS, dtype=st.dtype)
    full = st[:, :, :, :, None, :] * eye[None, None, :, None, :, None]
    return full.reshape(b, HEAD_W, HEAD_W)


def _ret_state_out(s):
    b = s.shape[0]
    s = s.reshape(b, 2, RET_HEADS, RET_DK // 2, RET_HEADS, RET_DV)
    blocks = [s[:, :, h, :, h, :].reshape(b, RET_DK, RET_DV) for h in range(RET_HEADS)]
    return jnp.stack(blocks, axis=1)


def _rope_angles(pos, dim):
    inv = ROPE_BASE ** (-jnp.arange(0, dim, 2, dtype=F32) / dim)
    return pos.astype(F32)[:, None] * inv[None, :]


def _rope_tables(pos, dim, reps):
    ang = _rope_angles(pos, dim)
    return jnp.tile(jnp.cos(ang), (1, reps)), jnp.tile(jnp.sin(ang), (1, reps))


def _rope_tables_swapped(pos, dim, reps):
    ang = _rope_angles(pos, dim)
    c, s = jnp.cos(ang), jnp.sin(ang)
    return jnp.tile(jnp.concatenate([c, c], axis=1), (1, reps)), jnp.tile(jnp.concatenate([-s, s], axis=1), (1, reps))


def _trunk(x, mods, pos0, cache, w):
    b, l, _ = x.shape
    depth = len(mods)
    pos = pos0 + jnp.arange(l)
    cm, sm = _rope_tables_swapped(pos, MLA_ROPE, MLA_HEADS)
    cr, sr = _rope_tables(pos, RET_DK, RET_HEADS)
    if cache is None:
        off = 0
    else:
        off = cache["lat"].shape[2]
        krt_cache = jnp.tile(cache["kr"], (1, 1, 1, 4)).astype(BF16)
        sb_cache = (cache["sk"].reshape(depth, b, off, HEAD_W), cache["sv"].reshape(depth, b, off, HEAD_W))
    stacked = tuple(jnp.zeros((depth, b, l, wd), F32) for _, wd, _, st in _INPROJ_OUTPUTS if st)
    small = [[], []]
    for layer in range(depth):
        mod = mods[layer]
        p = _inproj(x, mod, w["g_norm1"][layer], w["w_in"][layer], w["g_q_norm"][layer], w["w_uq"][layer],
                    w["g_kv_norm"][layer], w["w_uk"][layer], w["w_vt"][layer], cm, sm, cr, sr, layer, depth, stacked)
        stacked = (p["lat"], p["kr"], p["sk"], p["sv"])
        sb_new = (p["sk"], p["sv"])
        if cache is None:
            mla_past, sb_past = (p["kp"], p["vt"]), sb_new
            s0 = jnp.zeros((b, HEAD_W, HEAD_W), F32)
            prev = jnp.zeros((b, CONV_W - 1, D_FF), F32)
        else:
            mla_past = _kvup(cache["lat"], layer, krt_cache[layer], w["w_uk"][layer], w["w_vt"][layer])
            sb_past = sb_cache
            s0 = _ret_state_in(cache["S"][layer])
            prev = cache["conv"][layer]
        mla = _mla(p["qn"], p["qr"], mla_past, (p["kp"], p["vt"]), off)
        sb = _sb(p["sq"], sb_past, sb_new, layer, off)
        ret, s_new = _ret(p["rq"], p["rk"], p["rv"], p["rg"], s0, w["g_ret_norm"][layer])
        x, conv_state = _ffn(mla, ret, sb, x, mod, prev, w["w_o"][layer], w["g_norm2"][layer], w["w_up"][layer],
                             w["conv_w"][layer], w["conv_b"][layer], w["w_down"][layer], w["g_final"],
                             layer == depth - 1)
        small[0].append(_ret_state_out(s_new))
        small[1].append(conv_state)
    lat, kr, sk, sv = stacked
    return x, [lat, kr, sk.reshape(depth, b, l, SB_HEADS, SB_DK), sv.reshape(depth, b, l, SB_HEADS, SB_DV),
               jnp.stack(small[0], axis=0), jnp.stack(small[1], axis=0)]


def kernel(x_prompt, x_sample, c_prompt, c_sample, cache_mla_latent, cache_mla_krope, cache_sb_k, cache_sb_v, state_ret, state_ffn_conv, w_in, g_q_norm, w_uq, g_kv_norm, w_ukv, g_ret_norm, w_o, w_up, conv_w, conv_b, w_down, g_norm1, g_norm2, w_ada, b_ada, g_final):
    depth = w_in.shape[0]
    bp = x_prompt.shape[0]
    row = lambda g: g.reshape(g.shape[0], 1, g.shape[-1])
    w = dict(
        w_in=_permute_win(w_in).astype(BF16),
        w_uq=_permute_wuq(w_uq).astype(BF16),
        w_uk=_permute_wukv(w_ukv)[..., :MLA_HEADS * MLA_NOPE].astype(BF16),
        w_vt=jnp.swapaxes(_permute_wukv(w_ukv)[..., MLA_HEADS * MLA_NOPE:], -1, -2).astype(BF16),
        w_o=w_o.astype(BF16), w_up=w_up.astype(BF16), w_down=w_down.astype(BF16),
        g_q_norm=row(g_q_norm), g_kv_norm=row(g_kv_norm), g_ret_norm=row(g_ret_norm),
        g_norm1=row(g_norm1), g_norm2=row(g_norm2), conv_w=conv_w, conv_b=row(conv_b),
        g_final=g_final.reshape(1, -1))
    mod = _ada(jnp.concatenate([c_prompt, c_sample], axis=0), w_ada, b_ada)
    mod = mod.reshape(depth, mod.shape[1], 6, D_MODEL)
    y_p, st_p = _trunk(x_prompt, [mod[l, :bp] for l in range(depth)], 0, None, w)
    cache = dict(lat=cache_mla_latent, kr=cache_mla_krope, sk=cache_sb_k, sv=cache_sb_v, S=state_ret,
                 conv=state_ffn_conv)
    y_s, st_s = _trunk(x_sample, [mod[l, bp:] for l in range(depth)], cache_mla_latent.shape[2], cache, w)
    return (y_p, y_s, *st_p, *st_s)
```

```python
import functools

import numpy as np
import jax
import jax.numpy as jnp
from jax import lax
from jax.experimental import pallas as pl
from jax.experimental.pallas import tpu as pltpu

D_MODEL = 1024
CHUNK = 64
MLA_HEADS = 8
MLA_NOPE = 64
MLA_ROPE = 32
MLA_V = 64
MLA_Q_RANK = 384
MLA_KV_RANK = 256
RET_HEADS = 4
RET_DK = 64
RET_DV = 64
SB_HEADS = 4
SB_DK = 64
SB_DV = 64
D_FF = 2816
CONV_W = 3
ROPE_BASE = 10000.0
EPS = 1e-6

HEAD_W = 256
ROW_TILE = 512
ATT_TILE = 256
RET_CHUNK = 256
FF_CHUNKS = ((0, 1536), (1536, D_FF))
MLA_EXP2_SCALE = (MLA_NOPE + MLA_ROPE) ** -0.5 * 1.4426950408889634
SB_LOG_ZERO = -104.0
VMEM_LIMIT = 56 * 1024 * 1024

F32 = jnp.float32
BF16 = jnp.bfloat16

_ZQ, _ZKV, _RQ, _RK, _RV, _RG, _SQ, _SK, _SV, _KRT, _IN_COLS_P = (
    0, 384, 640, 896, 1152, 1408, 1664, 1920, 2176, 2432, 2688)


def _dot(a, b):
    return jnp.dot(a, b, preferred_element_type=F32)


def _dot_nt(a, b):
    return lax.dot_general(a, b, (((1,), (1,)), ((), ())), preferred_element_type=F32)


def _dot_tn(a, b):
    return lax.dot_general(a, b, (((0,), (0,)), ((), ())), preferred_element_type=F32)


def _rms(x):
    return x * lax.rsqrt(jnp.mean(x * x, axis=-1, keepdims=True) + EPS)


def _silu(x):
    return x / (1.0 + jnp.exp(-x))


def _params(*sem):
    return pltpu.CompilerParams(dimension_semantics=sem, vmem_limit_bytes=VMEM_LIMIT)


def _const_spec(shape):
    nd = len(shape)
    return pl.BlockSpec(shape, lambda *_: (0,) * nd)


def _ada_kernel(c_ref, w_ref, b_ref, o_ref):
    a = _silu(c_ref[...]).astype(BF16)
    o_ref[0] = _dot(a, w_ref[0].astype(BF16)) + b_ref[0]


def _ada(c_all, w_ada, b_ada):
    depth, d, n = w_ada.shape
    rows = c_all.shape[0]
    tn = 1536
    return pl.pallas_call(
        _ada_kernel,
        out_shape=jax.ShapeDtypeStruct((depth, rows, n), F32),
        grid=(depth, n // tn),
        in_specs=[pl.BlockSpec((rows, d), lambda l, j: (0, 0)),
                  pl.BlockSpec((1, d, tn), lambda l, j: (l, 0, j)),
                  pl.BlockSpec((1, 1, tn), lambda l, j: (l, 0, j))],
        out_specs=pl.BlockSpec((1, rows, tn), lambda l, j: (l, 0, j)),
        compiler_params=_params("parallel", "parallel"),
        name="ada_mod",
    )(c_all, w_ada, b_ada.reshape(depth, 1, n))


_INPROJ_INPUTS = 13
_INPROJ_OUTPUTS = (("qn", 512, BF16, False), ("qr", 256, BF16, False), ("lat", MLA_KV_RANK, F32, True),
                   ("kp", 1024, BF16, False), ("vt", None, BF16, False), ("kr", MLA_ROPE, F32, True),
                   ("rq", HEAD_W, BF16, False), ("rk", HEAD_W, F32, False), ("rv", HEAD_W, BF16, False),
                   ("rg", HEAD_W, F32, False), ("sq", HEAD_W, BF16, False), ("sk", HEAD_W, F32, True),
                   ("sv", HEAD_W, F32, True))


def _inproj_kernel(*refs):
    (x_ref, mod_ref, g1_ref, win_ref, gq_ref, wuq_ref, gkv_ref, wuk_ref, wvt_ref, cm_ref, sm_ref, cr_ref,
     sr_ref) = refs[:_INPROJ_INPUTS]
    names = [o[0] for o in _INPROJ_OUTPUTS]
    out = dict(zip(names, refs[len(refs) - len(names):]))
    stacked = [o[0] for o in _INPROJ_OUTPUTS if o[3]]
    earlier = refs[_INPROJ_INPUTS:len(refs) - len(names)]
    layer = len(earlier) // len(stacked)
    for r in range(layer):
        for k, name in enumerate(stacked):
            out[name][r] = earlier[r * len(stacked) + k][...]
    bb, tl, d = x_ref.shape
    rows = bb * tl
    mod = mod_ref[...]
    h = _rms(x_ref[...]) * g1_ref[...] * (1.0 + mod[:, 1:2, :]) + mod[:, 0:1, :]
    z = _dot(h.reshape(rows, d).astype(BF16), win_ref[...])

    def put(name, val, lanes=slice(None)):
        ref = out[name]
        lead = (layer,) if len(ref.shape) == 4 else ()
        ref[lead + (slice(None), slice(None), lanes)] = val.reshape(bb, tl, val.shape[-1]).astype(ref.dtype)

    def rope(x1, x2, c_ref, s_ref):
        c, s = c_ref[...][None], s_ref[...][None]
        x1 = x1.reshape(bb, tl, x1.shape[-1])
        x2 = x2.reshape(bb, tl, x2.shape[-1])
        return x1 * c - x2 * s, x1 * s + x2 * c

    def rope_swapped(a, b, width):
        a = a.reshape(bb, tl, width)
        b = b.reshape(bb, tl, width)
        return a * cm_ref[:, :width][None] + b * sm_ref[:, :width][None]

    q = _dot((_rms(z[:, _ZQ:_ZKV]) * gq_ref[...]).astype(BF16), wuq_ref[...])
    put("qn", q[:, :512] * MLA_EXP2_SCALE)
    put("qr", rope_swapped(q[:, 512:768], q[:, 768:1024], 256) * MLA_EXP2_SCALE)
    lat = _rms(z[:, _ZKV:_RQ]) * gkv_ref[...]
    put("lat", lat)
    kr = rope_swapped(z[:, _KRT:_KRT + 128], z[:, _KRT + 128:_KRT + 256], 128)
    put("kr", kr[:, :, :MLA_ROPE])
    lat_b = lat.astype(BF16)
    kn = _dot(lat_b, wuk_ref[...])
    for p in range(MLA_HEADS // 2):
        put("kp", kn[:, 128 * p:128 * (p + 1)], slice(256 * p, 256 * p + 128))
        put("kp", kr, slice(256 * p + 128, 256 * (p + 1)))
    vt_ref = out["vt"]
    tkk = vt_ref.shape[-1]
    for bi in range(bb):
        for kk in range(tl // tkk):
            r0 = bi * tl + kk * tkk
            vt_ref[bi, kk] = _dot_nt(wvt_ref[...], lat_b[r0:r0 + tkk]).astype(vt_ref.dtype)
    a1, a2 = rope(z[:, _RQ:_RQ + 128], z[:, _RQ + 128:_RK], cr_ref, sr_ref)
    put("rq", a1 * (RET_DK ** -0.5), slice(0, 128))
    put("rq", a2 * (RET_DK ** -0.5), slice(128, 256))
    b1, b2 = rope(z[:, _RK:_RK + 128], z[:, _RK + 128:_RV], cr_ref, sr_ref)
    put("rk", b1, slice(0, 128))
    put("rk", b2, slice(128, 256))
    put("rv", z[:, _RV:_RG])
    put("rg", z[:, _RG:_SQ])
    put("sq", z[:, _SQ:_SK] * (SB_DK ** -0.5))
    put("sk", z[:, _SK:_SV])
    put("sv", z[:, _SV:_KRT])


def _inproj(x, mod, g1, win, gq, wuq, gkv, wuk, wvt, cm, sm, cr, sr, earlier):
    b, l, d = x.shape
    tl = min(l, ROW_TILE)
    bb = ROW_TILE // tl
    grid = (b // bb, l // tl)
    depth = None if earlier is None else len(earlier) + 1
    tok = lambda w: pl.BlockSpec((bb, tl, w), lambda i, t: (i, t, 0))
    stk = lambda w: pl.BlockSpec((depth, bb, tl, w), lambda i, t: (0, i, t, 0))
    pos = pl.BlockSpec((tl, 128), lambda i, t: (t, 0))
    pos2 = pl.BlockSpec((tl, 256), lambda i, t: (t, 0))
    stacked = lambda st: st and depth is not None
    extra = [] if earlier is None else [a for row in earlier for a in row]
    tkk = min(l, ATT_TILE)
    vt_shape = jax.ShapeDtypeStruct((b, l // tkk, MLA_HEADS * MLA_V, tkk), BF16)
    vt_spec = pl.BlockSpec((bb, tl // tkk, MLA_HEADS * MLA_V, tkk), lambda i, t: (i, t, 0, 0))
    res = pl.pallas_call(
        _inproj_kernel,
        out_shape=[vt_shape if w is None else jax.ShapeDtypeStruct((depth, b, l, w) if stacked(st) else (b, l, w), dt)
                   for _, w, dt, st in _INPROJ_OUTPUTS],
        grid=grid,
        in_specs=[tok(d), pl.BlockSpec((bb, 6, d), lambda i, t: (i, 0, 0)), _const_spec(g1.shape),
                  _const_spec(win.shape), _const_spec(gq.shape), _const_spec(wuq.shape), _const_spec(gkv.shape),
                  _const_spec(wuk.shape), _const_spec(wvt.shape), pos2, pos2, pos, pos]
        + [tok(a.shape[-1]) for a in extra],
        out_specs=[vt_spec if w is None else (stk(w) if stacked(st) else tok(w)) for _, w, _, st in _INPROJ_OUTPUTS],
        compiler_params=_params("parallel", "parallel"),
        name="in_proj",
    )(x, mod, g1, win, gq, wuq, gkv, wuk, wvt, cm, sm, cr, sr, *extra)
    return dict(zip([o[0] for o in _INPROJ_OUTPUTS], res))


def _kvup_kernel(lat_ref, krt_ref, wuk_ref, wvt_ref, kp_ref, vt_ref):
    lat_b = lat_ref[0].astype(BF16)
    kn = _dot(lat_b, wuk_ref[...])
    for p in range(MLA_HEADS // 2):
        kp_ref[0, :, 256 * p:256 * p + 128] = kn[:, 128 * p:128 * (p + 1)].astype(kp_ref.dtype)
        kp_ref[0, :, 256 * p + 128:256 * (p + 1)] = krt_ref[0]
    tkk = vt_ref.shape[-1]
    for kk in range(vt_ref.shape[1]):
        vt_ref[0, kk] = _dot_nt(wvt_ref[...], lat_b[kk * tkk:(kk + 1) * tkk]).astype(vt_ref.dtype)


def _kvup(lat, layer, krt, wuk, wvt):
    _, b, lk, r = lat.shape
    nv = MLA_HEADS * MLA_V
    blk = lambda w: pl.BlockSpec((1, lk, w), lambda i: (i, 0, 0))
    return pl.pallas_call(
        _kvup_kernel,
        out_shape=(jax.ShapeDtypeStruct((b, lk, 1024), BF16),
                   jax.ShapeDtypeStruct((b, lk // ATT_TILE, nv, ATT_TILE), BF16)),
        grid=(b,),
        in_specs=[pl.BlockSpec((None, 1, lk, r), lambda i: (layer, i, 0, 0)), blk(128), _const_spec(wuk.shape),
                  _const_spec(wvt.shape)],
        out_specs=[blk(1024), pl.BlockSpec((1, lk // ATT_TILE, nv, ATT_TILE), lambda i: (i, 0, 0, 0))],
        compiler_params=_params("parallel"),
        name="kv_up",
    )(lat, krt, wuk, wvt)


def _fori_pairs(n, body, init):
    carry = lax.fori_loop(0, n // 2, lambda i, c: body(2 * i + 1, body(2 * i, c)), init)
    return lax.fori_loop(0, n % 2, lambda i, c: body(n - 1, c), carry)


def _fold(op, x):
    w = x.shape[1]
    parts = [x[:, i:i + 128] for i in range(0, w, 128)]
    out = parts[0]
    for p in parts[1:]:
        out = op(out, p)
    return out


def _fold8(reduce, x):
    return reduce(x.reshape(x.shape[0] // 8, 8, x.shape[1]), axis=0)


def _mla_kernel(qn_ref, qr_ref, kp_ref, vtp_ref, kd_ref, vtd_ref, o_ref, s_scr, q_scr, mx_scr, m_scr, l_scr, acc_scr,
                *, tq, tk, off, fused):
    qi = pl.program_id(1)
    nfull = (off + qi * tq) // tk
    ntile = nfull + 1 if fused else nfull
    lane = lax.broadcasted_iota(jnp.int32, (1, 128), 1)
    key_c = lax.broadcasted_iota(jnp.int32, (tq, 1), 0) // CHUNK
    qry_c = lax.broadcasted_iota(jnp.int32, (1, tq), 1) // CHUNK
    dmask = key_c <= qry_c
    neg = -1e30
    lane_v = lax.broadcasted_iota(jnp.int32, (1, HEAD_W), 1)
    feat = lax.broadcasted_iota(jnp.int32, (MLA_V, 1), 0)
    for g in range(MLA_HEADS // 4):
        gs = slice(HEAD_W * g, HEAD_W * (g + 1))
        qr = qr_ref[0, :, 128 * g:128 * (g + 1)]
        for hh in range(4):
            qn = qn_ref[0, :, 128 * (2 * g + hh // 2):128 * (2 * g + hh // 2 + 1)]
            q_scr[hh, :, 0:128] = jnp.where((lane >> 6) == hh % 2, qn, jnp.zeros_like(qn))
            q_scr[hh, :, 128:256] = jnp.where((lane >> 5) == hh, qr, jnp.zeros_like(qr))

        def head_scores(keys_of_pair, hh):
            return _dot_nt(keys_of_pair(2 * g + hh // 2), q_scr[hh])

        def vt_rows(hh):
            return slice(MLA_V * (4 * g + hh), MLA_V * (4 * g + hh + 1))

        def scores(j, carry):
            ks = pl.ds(pl.multiple_of(j * tk, tk), tk)
            for hh in range(4):
                s = head_scores(lambda p: kp_ref[0, ks, 256 * p:256 * (p + 1)], hh)
                s_scr[j, hh] = s
                mx_scr[hh] = jnp.maximum(mx_scr[hh], _fold8(jnp.max, s))
            return carry

        own = []
        for hh in range(4):
            if fused:
                ks_d = pl.ds(pl.multiple_of(nfull * tk, tk), tk)
                s_d = jnp.where(dmask, head_scores(lambda p: kp_ref[0, ks_d, 256 * p:256 * (p + 1)], hh), neg)
                s_scr[nfull, hh] = s_d
            else:
                s_d = jnp.where(dmask, head_scores(lambda p: kd_ref[0, :, 256 * p:256 * (p + 1)], hh), neg)
                own.append(s_d)
            mx_scr[hh] = _fold8(jnp.max, s_d)
        _fori_pairs(nfull, scores, 0)
        for hh in range(4):
            m_scr[hh] = jnp.broadcast_to(jnp.max(mx_scr[hh], axis=0, keepdims=True), (8, tq))

        def values(j, carry):
            for hh in range(4):
                p = jnp.exp2(s_scr[j, hh] - m_scr[hh, 0:1, :])
                l_scr[hh] += _fold8(jnp.sum, p)
                acc_scr[hh] += _dot(vtp_ref[0, j, vt_rows(hh), :], p.astype(BF16))
            return carry

        for hh in range(4):
            if fused:
                l_scr[hh] = jnp.zeros((8, tq), F32)
                acc_scr[hh] = jnp.zeros((MLA_V, tq), F32)
            else:
                p = jnp.exp2(own[hh] - m_scr[hh, 0:1, :])
                l_scr[hh] = _fold8(jnp.sum, p)
                acc_scr[hh] = _dot(vtd_ref[0, 0, vt_rows(hh), :], p.astype(BF16))
        _fori_pairs(ntile, values, 0)
        out_g = jnp.zeros((tq, HEAD_W), F32)
        for hh in range(4):
            inv = 1.0 / jnp.sum(l_scr[hh], axis=0, keepdims=True)
            place = jnp.where(feat + MLA_V * hh == lane_v, 1.0, 0.0).astype(BF16)
            out_g = out_g + _dot_tn((acc_scr[hh] * inv).astype(BF16), place)
        o_ref[0, :, gs] = out_g.astype(o_ref.dtype)


def _mla(qn, qr, past, diag, off):
    b, l, _ = qn.shape
    lp = past[0].shape[1]
    tq = min(l, ATT_TILE)
    tk = ATT_TILE
    nv = MLA_HEADS * MLA_V
    fused = off == 0 and tq == tk
    qblk = lambda w: pl.BlockSpec((1, tq, w), lambda i, t: (i, t, 0))
    nmax = (off + l - tq) // tk + (1 if fused else 0)
    kern = functools.partial(_mla_kernel, tq=tq, tk=tk, off=off, fused=fused)
    return pl.pallas_call(
        kern,
        out_shape=jax.ShapeDtypeStruct((b, l, 512), BF16),
        grid=(b, l // tq),
        in_specs=[qblk(512), qblk(256), pl.BlockSpec((1, lp, 1024), lambda i, t: (i, 0, 0)),
                  pl.BlockSpec((1, lp // tk, nv, tk), lambda i, t: (i, 0, 0, 0)), qblk(1024),
                  pl.BlockSpec((1, 1, nv, tq), lambda i, t: (i, t, 0, 0))],
        out_specs=qblk(512),
        scratch_shapes=[pltpu.VMEM((nmax, 4, tk, tq), F32), pltpu.VMEM((4, tq, 256), BF16),
                        pltpu.VMEM((4, 8, tq), F32), pltpu.VMEM((4, 8, tq), F32), pltpu.VMEM((4, 8, tq), F32),
                        pltpu.VMEM((4, MLA_V, tq), F32)],
        compiler_params=_params("parallel", "parallel"),
        name="mla_attn",
    )(qn, qr, *past, *diag)


def _neg_suffix_matrix(n):
    j = lax.broadcasted_iota(jnp.int32, (n, n), 0)
    s = lax.broadcasted_iota(jnp.int32, (n, n), 1)
    return jnp.where(j >= s, -1.0, 0.0).astype(BF16)


def _sb_kernel(q_ref, kp_ref, vp_ref, kd_ref, vd_ref, o_ref, kb_ref, vb_ref, q_scr, run_scr, acc_scr, *, tq, tk, off):
    qi = pl.program_id(1)

    @pl.when(qi == 0)
    def _():
        kb_ref[...] = kp_ref[0].astype(BF16)
        vb_ref[...] = vp_ref[0].astype(BF16)

    nfull = (off + qi * tq) // tk
    lane = lax.broadcasted_iota(jnp.int32, (1, HEAD_W), 1)
    nh = SB_HEADS
    dmask = (lax.broadcasted_iota(jnp.int32, (1, tq), 1)
             < lax.broadcasted_iota(jnp.int32, (nh * tq, 1), 0) % tq)
    t_diag = _neg_suffix_matrix(tq)
    t_full = t_diag if tk == tq else _neg_suffix_matrix(tk)
    q = q_ref[0]

    def tile(k, v, mask, tmat, first):
        zz = _dot_nt(q_scr[...], k)
        sp = jnp.maximum(zz, 0.0) + jnp.log(1.0 + jnp.exp(-jnp.abs(zz)))
        if mask is not None:
            sp = jnp.where(mask, sp, 0.0)
        hi = sp.astype(BF16)
        lo = (sp - hi.astype(F32)).astype(BF16)
        incl = _dot(hi, tmat) + _dot(lo, tmat)
        if first:
            w = jnp.exp(zz + incl)
            run = jnp.zeros((nh * tq, 128), F32)
        else:
            run = run_scr[...]
            w = jnp.exp(zz + incl + (run if zz.shape[1] == 128 else jnp.concatenate([run] * (zz.shape[1] // 128), -1)))
        if mask is not None:
            w = jnp.where(mask, w, 0.0)
        pv = _dot(w.astype(BF16), v)
        run = run - jnp.sum(sp, axis=-1, keepdims=True)
        run_scr[...] = run
        if first:
            acc_scr[...] = pv
        else:
            acc_scr[...] += pv
        return jnp.max(run)

    for h in range(nh):
        q_scr[h * tq:(h + 1) * tq, :] = jnp.where((lane >> 6) == h, q, jnp.zeros_like(q))
    top = tile(kd_ref[0].astype(BF16), vd_ref[0].astype(BF16), dmask, t_diag, True)

    def more(state):
        j, top = state
        return jnp.logical_and(j >= 0, top > SB_LOG_ZERO)

    def body(state):
        j, _ = state
        ks = pl.ds(pl.multiple_of(j * tk, tk), tk)
        return j - 1, tile(kb_ref[ks, :], vb_ref[ks, :], None, t_full, False)

    lax.while_loop(more, body, (nfull - 1, top))
    out = jnp.zeros((tq, HEAD_W), F32)
    for h in range(nh):
        out = jnp.where((lane >> 6) == h, acc_scr[h * tq:(h + 1) * tq, :], out)
    o_ref[0] = out.astype(o_ref.dtype)


def _sb(sq, past, diag, off):
    b, l, _ = sq.shape
    lp = past[0].shape[2]
    tq = min(l, ATT_TILE)
    qblk = pl.BlockSpec((1, tq, HEAD_W), lambda i, t: (i, t, 0))
    dblk = pl.BlockSpec((None, 1, tq, HEAD_W), lambda i, t: (diag[2], i, t, 0))
    kblk = pl.BlockSpec((None, 1, lp, HEAD_W), lambda i, t: (past[2], i, 0, 0))
    kern = functools.partial(_sb_kernel, tq=tq, tk=ATT_TILE, off=off)
    return pl.pallas_call(
        kern,
        out_shape=jax.ShapeDtypeStruct((b, l, HEAD_W), BF16),
        grid=(b, l // tq),
        in_specs=[qblk, kblk, kblk, dblk, dblk],
        out_specs=qblk,
        scratch_shapes=[pltpu.VMEM((lp, HEAD_W), BF16), pltpu.VMEM((lp, HEAD_W), BF16),
                        pltpu.VMEM((SB_HEADS * tq, HEAD_W), BF16), pltpu.VMEM((SB_HEADS * tq, 128), F32),
                        pltpu.VMEM((SB_HEADS * tq, HEAD_W), F32)],
        compiler_params=_params("parallel", "arbitrary"),
        name="sb_attn",
    )(sq, *past[:2], *diag[:2])


def _group_mean(x, avg):
    hi = x.astype(BF16)
    lo = (x - hi.astype(F32)).astype(BF16)
    return _dot(hi, avg) + _dot(lo, avg)


def _ret_kernel(q_ref, k_ref, v_ref, g_ref, s0_ref, gn_ref, dec_ref, qd_ref, kd_ref, sd_ref,
                o_ref, s_ref, *, c):
    n = q_ref.shape[1] // c
    lane = lax.broadcasted_iota(jnp.int32, (1, HEAD_W), 1)
    krow = lax.broadcasted_iota(jnp.int32, (HEAD_W, 1), 0)
    bd_mask = ((krow & 127) >> 5) == (lane >> 6)
    avg = jnp.where((krow >> 6) == (lane >> 6), 1.0 / RET_DV, 0.0).astype(BF16)
    s_ref[0] = s0_ref[0]

    def step(i, carry):
        rs = pl.ds(pl.multiple_of(i * c, c), c)
        q = q_ref[0, rs, :]
        kf = k_ref[0, rs, :]
        k = kf.astype(BF16)
        v = v_ref[0, rs, :]
        state = s_ref[0]
        o = _dot(q, state.astype(BF16)) * qd_ref[...]
        qs = jnp.concatenate([jnp.where(((lane & 127) >> 5) == h, q, jnp.zeros_like(q)) for h in range(RET_HEADS)],
                             axis=0)
        pv = _dot((_dot_nt(qs, k) * dec_ref[...]).astype(BF16), v)
        for h in range(RET_HEADS):
            o = o + jnp.where((lane >> 6) == h, pv[h * c:(h + 1) * c], 0.0)
        upd = _dot_tn((kf * kd_ref[...]).astype(BF16), v)
        s_ref[0] = sd_ref[...] * state + jnp.where(bd_mask, upd, 0.0)
        dlt = o - _group_mean(o, avg)
        ro = dlt * lax.rsqrt(_group_mean(dlt * dlt, avg) + EPS) * gn_ref[...]
        o_ref[0, rs, :] = (_silu(g_ref[0, rs, :]) * ro).astype(o_ref.dtype)
        return carry

    _fori_pairs(n, step, 0)


def _ret_consts(c):
    lg = jnp.log(1.0 - 2.0 ** (-5.0 - jnp.arange(RET_HEADS, dtype=F32)))
    i = jnp.arange(c, dtype=F32)
    rel = i[:, None] - i[None, :]
    dec = jnp.where(rel >= 0, jnp.exp(lg[:, None, None] * jnp.maximum(rel, 0.0)), 0.0)
    v_head = jnp.arange(HEAD_W) // RET_DV
    k_head = (jnp.arange(HEAD_W) % 128) // (RET_DK // 2)
    qd = jnp.exp(lg[None, v_head] * (i[:, None] + 1.0))
    kd = jnp.exp(lg[None, k_head] * (c - 1.0 - i[:, None]))
    sd = jnp.exp(lg * c)[v_head][None, :]
    return dec, qd, kd, sd


def _ret(rq, rk, rv, rg, s0, gn):
    b, l, _ = rq.shape
    c = min(l, RET_CHUNK)
    dec, qd, kd, sd = _ret_consts(c)
    dec = dec.reshape(RET_HEADS * c, c)
    blk = pl.BlockSpec((1, l, HEAD_W), lambda i: (i, 0, 0))
    sblk = pl.BlockSpec((1, HEAD_W, HEAD_W), lambda i: (i, 0, 0))
    return pl.pallas_call(
        functools.partial(_ret_kernel, c=c),
        out_shape=[jax.ShapeDtypeStruct((b, l, HEAD_W), BF16), jax.ShapeDtypeStruct((b, HEAD_W, HEAD_W), F32)],
        grid=(b,),
        in_specs=[blk, blk, blk, blk, sblk, _const_spec(gn.shape), _const_spec(dec.shape), _const_spec(qd.shape),
                  _const_spec(kd.shape), _const_spec(sd.shape)],
        out_specs=[blk, sblk],
        compiler_params=_params("parallel"),
        name="retention",
    )(rq, rk, rv, rg, s0, gn, dec, qd, kd, sd)


def _ffn_kernel(mla_ref, ret_ref, sb_ref, x_ref, mod_ref, prev_ref, wo_ref, g2_ref, wup_ref, cw_ref, cb_ref,
                wdn_ref, gf_ref, o_ref, st_ref, a_scr, carry, *, final_norm):
    bb, tl, d = x_ref.shape
    rows = bb * tl
    t = pl.program_id(1)
    nt = pl.num_programs(1)

    @pl.when(t == 0)
    def _():
        carry[:, 6:8, :] = prev_ref[...]

    mod = mod_ref[...]
    cat = jnp.concatenate([mla_ref[...].reshape(rows, 512), ret_ref[...].reshape(rows, HEAD_W),
                           sb_ref[...].reshape(rows, HEAD_W)], axis=-1)
    x1 = x_ref[...] + mod[:, 2:3, :] * _dot(cat, wo_ref[...]).reshape(bb, tl, d)
    h = (_rms(x1) * g2_ref[...] * (1.0 + mod[:, 4:5, :]) + mod[:, 3:4, :]).reshape(rows, d).astype(BF16)
    f = jnp.zeros((rows, d), F32)
    for c0, c1 in FF_CHUNKS:
        cs, n = slice(c0, c1), c1 - c0
        a = _dot(h, wup_ref[:, cs]).reshape(bb, tl, n)
        b = _dot(h, wup_ref[:, D_FF + c0:D_FF + c1])
        a_scr[:, 8:, :n] = a
        a_scr[:, 6:8, :n] = carry[:, 6:8, cs]
        cw = cw_ref[...]
        conv = (cb_ref[:, cs] + cw[0:1, cs] * a_scr[:, 6:6 + tl, :n] + cw[1:2, cs] * a_scr[:, 7:7 + tl, :n]
                + cw[2:3, cs] * a)
        carry[:, 6:8, cs] = a_scr[:, tl + 6:tl + 8, :n]
        y = (_silu(conv).reshape(rows, n) * b).astype(BF16)
        f = f + _dot(y, wdn_ref[cs, :])
    x2 = x1 + mod[:, 5:6, :] * f.reshape(bb, tl, d)
    if final_norm:
        x2 = _rms(x2) * gf_ref[...]
    o_ref[...] = x2

    @pl.when(t == nt - 1)
    def _():
        st_ref[...] = carry[:, 6:8, :]


def _ffn(mla, ret, sb, x, mod, prev, wo, g2, wup, cw, cb, wdn, gf, final_norm):
    b, l, d = x.shape
    tl = min(l, ROW_TILE)
    bb = ROW_TILE // tl
    tok = lambda w: pl.BlockSpec((bb, tl, w), lambda i, t: (i, t, 0))
    st = pl.BlockSpec((bb, CONV_W - 1, D_FF), lambda i, t: (i, 0, 0))
    once = lambda a: pl.BlockSpec(a.shape, lambda i, t: (0, 0), pipeline_mode=pl.Buffered(1))
    return pl.pallas_call(
        functools.partial(_ffn_kernel, final_norm=final_norm),
        out_shape=[jax.ShapeDtypeStruct((b, l, d), F32), jax.ShapeDtypeStruct((b, CONV_W - 1, D_FF), F32)],
        grid=(b // bb, l // tl),
        in_specs=[tok(512), tok(HEAD_W), tok(HEAD_W), tok(d), pl.BlockSpec((bb, 6, d), lambda i, t: (i, 0, 0)), st,
                  once(wo), _const_spec(g2.shape), once(wup), _const_spec(cw.shape), _const_spec(cb.shape),
                  once(wdn), _const_spec(gf.shape)],
        out_specs=[tok(d), st],
        scratch_shapes=[pltpu.VMEM((bb, tl + 8, max(c1 - c0 for c0, c1 in FF_CHUNKS)), F32),
                        pltpu.VMEM((bb, 8, D_FF), F32)],
        compiler_params=_params("parallel", "arbitrary"),
        name="conv_ffn",
    )(mla, ret, sb, x, mod, prev, wo, g2, wup, cw, cb, wdn, gf)


def _half_split(w, heads):
    lead = w.shape[:-1]
    dim = w.shape[-1] // heads
    w = w.reshape(lead + (heads, 2, dim // 2))
    return jnp.swapaxes(w, -3, -2).reshape(lead + (heads * dim,))


def _permute_win(w):
    half = MLA_ROPE // 2
    kr = w[..., 640:672]
    kr_swapped = jnp.concatenate([kr[..., half:], kr[..., :half]], axis=-1)
    tiled = lambda a: jnp.tile(a, (1, 1, 4))
    return jnp.concatenate([w[..., 0:640], _half_split(w[..., 672:928], RET_HEADS),
                            _half_split(w[..., 928:1184], RET_HEADS), w[..., 1184:2464],
                            tiled(kr), tiled(kr_swapped)], axis=-1)


def _permute_wuq(w):
    half = MLA_ROPE // 2
    w = w.reshape(w.shape[:-1] + (MLA_HEADS, MLA_NOPE + MLA_ROPE))
    flat = lambda a: a.reshape(a.shape[:-2] + (-1,))
    rope = w[..., MLA_NOPE:]
    swapped = jnp.concatenate([rope[..., half:], rope[..., :half]], axis=-1)
    return jnp.concatenate([flat(w[..., :MLA_NOPE]), flat(rope), flat(swapped)], axis=-1)


def _permute_wukv(w):
    w = w.reshape(w.shape[:-1] + (MLA_HEADS, MLA_NOPE + MLA_V))
    flat = lambda a: a.reshape(a.shape[:-2] + (-1,))
    return jnp.concatenate([flat(w[..., :MLA_NOPE]), flat(w[..., MLA_NOPE:])], axis=-1)


def _ret_state_in(st):
    b = st.shape[0]
    st = st.reshape(b, RET_HEADS, 2, RET_DK // 2, RET_DV).transpose(0, 2, 1, 3, 4)
    eye = jnp.eye(RET_HEADS, dtype=st.dtype)
    full = st[:, :, :, :, None, :] * eye[None, None, :, None, :, None]
    return full.reshape(b, HEAD_W, HEAD_W)


def _ret_state_out(s):
    b = s.shape[0]
    s = s.reshape(b, 2, RET_HEADS, RET_DK // 2, RET_HEADS, RET_DV)
    blocks = [s[:, :, h, :, h, :].reshape(b, RET_DK, RET_DV) for h in range(RET_HEADS)]
    return jnp.stack(blocks, axis=1)


def _rope_angles(pos, dim):
    inv = ROPE_BASE ** (-jnp.arange(0, dim, 2, dtype=F32) / dim)
    return pos.astype(F32)[:, None] * inv[None, :]


def _rope_tables(pos, dim, reps):
    ang = _rope_angles(pos, dim)
    return jnp.tile(jnp.cos(ang), (1, reps)), jnp.tile(jnp.sin(ang), (1, reps))


def _rope_tables_swapped(pos, dim, reps):
    ang = _rope_angles(pos, dim)
    c, s = jnp.cos(ang), jnp.sin(ang)
    return jnp.tile(jnp.concatenate([c, c], axis=1), (1, reps)), jnp.tile(jnp.concatenate([-s, s], axis=1), (1, reps))


def _trunk(x, mods, pos0, cache, w):
    b, l, _ = x.shape
    depth = len(mods)
    pos = pos0 + jnp.arange(l)
    cm, sm = _rope_tables_swapped(pos, MLA_ROPE, MLA_HEADS)
    cr, sr = _rope_tables(pos, RET_DK, RET_HEADS)
    if cache is None:
        off = 0
    else:
        off = cache["lat"].shape[2]
        krt_cache = jnp.tile(cache["kr"], (1, 1, 1, 4)).astype(BF16)
        sb_cache = (cache["sk"].reshape(depth, b, off, HEAD_W), cache["sv"].reshape(depth, b, off, HEAD_W))
    earlier = []
    small = [[], []]
    for layer in range(depth):
        mod = mods[layer]
        last = layer == depth - 1
        p = _inproj(x, mod, w["g_norm1"][layer], w["w_in"][layer], w["g_q_norm"][layer], w["w_uq"][layer],
                    w["g_kv_norm"][layer], w["w_uk"][layer], w["w_vt"][layer], cm, sm, cr, sr,
                    earlier if last else None)
        if last:
            sb_new = (p["sk"], p["sv"], layer)
        else:
            earlier.append((p["lat"], p["kr"], p["sk"], p["sv"]))
            sb_new = (p["sk"][None], p["sv"][None], 0)
        if cache is None:
            mla_past, sb_past = (p["kp"], p["vt"]), sb_new
            s0 = jnp.zeros((b, HEAD_W, HEAD_W), F32)
            prev = jnp.zeros((b, CONV_W - 1, D_FF), F32)
        else:
            mla_past = _kvup(cache["lat"], layer, krt_cache[layer], w["w_uk"][layer], w["w_vt"][layer])
            sb_past = sb_cache + (layer,)
            s0 = _ret_state_in(cache["S"][layer])
            prev = cache["conv"][layer]
        mla = _mla(p["qn"], p["qr"], mla_past, (p["kp"], p["vt"]), off)
        sb = _sb(p["sq"], sb_past, sb_new, off)
        ret, s_new = _ret(p["rq"], p["rk"], p["rv"], p["rg"], s0, w["g_ret_norm"][layer])
        x, conv_state = _ffn(mla, ret, sb, x, mod, prev, w["w_o"][layer], w["g_norm2"][layer], w["w_up"][layer],
                             w["conv_w"][layer], w["conv_b"][layer], w["w_down"][layer], w["g_final"],
                             layer == depth - 1)
        small[0].append(_ret_state_out(s_new))
        small[1].append(conv_state)
    lat, kr, sk, sv = p["lat"], p["kr"], p["sk"], p["sv"]
    return x, [lat, kr, sk.reshape(depth, b, l, SB_HEADS, SB_DK), sv.reshape(depth, b, l, SB_HEADS, SB_DV),
               jnp.stack(small[0], axis=0), jnp.stack(small[1], axis=0)]


def kernel(x_prompt, x_sample, c_prompt, c_sample, cache_mla_latent, cache_mla_krope, cache_sb_k, cache_sb_v, state_ret, state_ffn_conv, w_in, g_q_norm, w_uq, g_kv_norm, w_ukv, g_ret_norm, w_o, w_up, conv_w, conv_b, w_down, g_norm1, g_norm2, w_ada, b_ada, g_final):
    depth = w_in.shape[0]
    bp = x_prompt.shape[0]
    row = lambda g: g.reshape(g.shape[0], 1, g.shape[-1])
    w = dict(
        w_in=_permute_win(w_in).astype(BF16),
        w_uq=_permute_wuq(w_uq).astype(BF16),
        w_uk=_permute_wukv(w_ukv)[..., :MLA_HEADS * MLA_NOPE].astype(BF16),
        w_vt=jnp.swapaxes(_permute_wukv(w_ukv)[..., MLA_HEADS * MLA_NOPE:], -1, -2).astype(BF16),
        w_o=w_o.astype(BF16), w_up=w_up.astype(BF16), w_down=w_down.astype(BF16),
        g_q_norm=row(g_q_norm), g_kv_norm=row(g_kv_norm), g_ret_norm=row(g_ret_norm),
        g_norm1=row(g_norm1), g_norm2=row(g_norm2), conv_w=conv_w, conv_b=row(conv_b),
        g_final=g_final.reshape(1, -1))
    mod = _ada(jnp.concatenate([c_prompt, c_sample], axis=0), w_ada, b_ada)
    mod = mod.reshape(depth, mod.shape[1], 6, D_MODEL)
    y_p, st_p = _trunk(x_prompt, [mod[l, :bp] for l in range(depth)], 0, None, w)
    cache = dict(lat=cache_mla_latent, kr=cache_mla_krope, sk=cache_sb_k, sv=cache_sb_v, S=state_ret,
                 conv=state_ffn_conv)
    y_s, st_s = _trunk(x_sample, [mod[l, bp:] for l in range(depth)], cache_mla_latent.shape[2], cache, w)
    return (y_p, y_s, *st_p, *st_s)
```

```python
import functools

import jax
import jax.numpy as jnp
from jax import lax
from jax.experimental import pallas as pl
from jax.experimental.pallas import tpu as pltpu

D_MODEL = 1024
CHUNK = 64
MLA_HEADS = 8
MLA_NOPE = 64
MLA_ROPE = 32
MLA_V = 64
MLA_Q_RANK = 384
MLA_KV_RANK = 256
RET_HEADS = 4
RET_DK = 64
RET_DV = 64
SB_HEADS = 4
SB_DK = 64
SB_DV = 64
D_FF = 2816
CONV_W = 3
ROPE_BASE = 10000.0
EPS = 1e-6

HEAD_W = 256
ROW_TILE = 512
ATT_TILE = 256
RET_CHUNK = 256
FF_CHUNKS = ((0, 1536), (1536, D_FF))
MLA_EXP2_SCALE = (MLA_NOPE + MLA_ROPE) ** -0.5 * 1.4426950408889634
SB_LOG_ZERO = -104.0
VMEM_LIMIT = 56 * 1024 * 1024

F32 = jnp.float32
BF16 = jnp.bfloat16

_ZQ, _ZKV, _RQ, _RK, _RV, _RG, _SQ, _SK, _SV, _KRT, _IN_COLS_P = (
    0, 384, 640, 896, 1152, 1408, 1664, 1920, 2176, 2432, 2688)


def _dot(a, b):
    return jnp.dot(a, b, preferred_element_type=F32)


def _dot_nt(a, b):
    return lax.dot_general(a, b, (((1,), (1,)), ((), ())), preferred_element_type=F32)


def _dot_tn(a, b):
    return lax.dot_general(a, b, (((0,), (0,)), ((), ())), preferred_element_type=F32)


def _rms(x):
    return x * lax.rsqrt(jnp.mean(x * x, axis=-1, keepdims=True) + EPS)


def _silu(x):
    return x / (1.0 + jnp.exp(-x))


def _params(*sem):
    return pltpu.CompilerParams(dimension_semantics=sem, vmem_limit_bytes=VMEM_LIMIT)


def _const_spec(shape):
    nd = len(shape)
    return pl.BlockSpec(shape, lambda *_: (0,) * nd)


def _ada_kernel(c_ref, w_ref, b_ref, o_ref):
    a = _silu(c_ref[...]).astype(BF16)
    o_ref[0] = _dot(a, w_ref[0].astype(BF16)) + b_ref[0]


def _ada(c_all, w_ada, b_ada):
    depth, d, n = w_ada.shape
    rows = c_all.shape[0]
    tn = 1536
    return pl.pallas_call(
        _ada_kernel,
        out_shape=jax.ShapeDtypeStruct((depth, rows, n), F32),
        grid=(depth, n // tn),
        in_specs=[pl.BlockSpec((rows, d), lambda l, j: (0, 0)),
                  pl.BlockSpec((1, d, tn), lambda l, j: (l, 0, j)),
                  pl.BlockSpec((1, 1, tn), lambda l, j: (l, 0, j))],
        out_specs=pl.BlockSpec((1, rows, tn), lambda l, j: (l, 0, j)),
        compiler_params=_params("parallel", "parallel"),
        name="ada_mod",
    )(c_all, w_ada, b_ada.reshape(depth, 1, n))


_INPROJ_INPUTS = 13
_INPROJ_OUTPUTS = (("qn", 512, BF16, False), ("qr", 256, BF16, False), ("lat", MLA_KV_RANK, F32, True),
                   ("kp", 1024, BF16, False), ("vt", None, BF16, False), ("kr", MLA_ROPE, F32, True),
                   ("rq", HEAD_W, BF16, False), ("rk", HEAD_W, F32, False), ("rv", HEAD_W, BF16, False),
                   ("rg", HEAD_W, F32, False), ("sq", HEAD_W, BF16, False), ("sk", HEAD_W, F32, True),
                   ("sv", HEAD_W, F32, True))


def _inproj_kernel(*refs):
    (x_ref, mod_ref, g1_ref, win_ref, gq_ref, wuq_ref, gkv_ref, wuk_ref, wvt_ref, cm_ref, sm_ref, cr_ref,
     sr_ref) = refs[:_INPROJ_INPUTS]
    names = [o[0] for o in _INPROJ_OUTPUTS]
    out = dict(zip(names, refs[len(refs) - len(names):]))
    stacked = [o[0] for o in _INPROJ_OUTPUTS if o[3]]
    earlier = refs[_INPROJ_INPUTS:len(refs) - len(names)]
    layer = len(earlier) // len(stacked)
    for r in range(layer):
        for k, name in enumerate(stacked):
            out[name][r] = earlier[r * len(stacked) + k][...]
    bb, tl, d = x_ref.shape
    rows = bb * tl
    mod = mod_ref[...]
    h = _rms(x_ref[...]) * g1_ref[...] * (1.0 + mod[:, 1:2, :]) + mod[:, 0:1, :]
    z = _dot(h.reshape(rows, d).astype(BF16), win_ref[...])

    def put(name, val, lanes=slice(None)):
        ref = out[name]
        lead = (layer,) if len(ref.shape) == 4 else ()
        ref[lead + (slice(None), slice(None), lanes)] = val.reshape(bb, tl, val.shape[-1]).astype(ref.dtype)

    def rope(x1, x2, c_ref, s_ref):
        c, s = c_ref[...][None], s_ref[...][None]
        x1 = x1.reshape(bb, tl, x1.shape[-1])
        x2 = x2.reshape(bb, tl, x2.shape[-1])
        return x1 * c - x2 * s, x1 * s + x2 * c

    def rope_swapped(a, b, width):
        a = a.reshape(bb, tl, width)
        b = b.reshape(bb, tl, width)
        return a * cm_ref[:, :width][None] + b * sm_ref[:, :width][None]

    q = _dot((_rms(z[:, _ZQ:_ZKV]) * gq_ref[...]).astype(BF16), wuq_ref[...])
    put("qn", q[:, :512] * MLA_EXP2_SCALE)
    put("qr", rope_swapped(q[:, 512:768], q[:, 768:1024], 256) * MLA_EXP2_SCALE)
    lat = _rms(z[:, _ZKV:_RQ]) * gkv_ref[...]
    put("lat", lat)
    kr = rope_swapped(z[:, _KRT:_KRT + 128], z[:, _KRT + 128:_KRT + 256], 128)
    put("kr", kr[:, :, :MLA_ROPE])
    lat_b = lat.astype(BF16)
    kn = _dot(lat_b, wuk_ref[...])
    for p in range(MLA_HEADS // 2):
        put("kp", kn[:, 128 * p:128 * (p + 1)], slice(256 * p, 256 * p + 128))
        put("kp", kr, slice(256 * p + 128, 256 * (p + 1)))
    vt_ref = out["vt"]
    tkk = vt_ref.shape[-1]
    for bi in range(bb):
        for kk in range(tl // tkk):
            r0 = bi * tl + kk * tkk
            vt_ref[bi, kk] = _dot_nt(wvt_ref[...], lat_b[r0:r0 + tkk]).astype(vt_ref.dtype)
    a1, a2 = rope(z[:, _RQ:_RQ + 128], z[:, _RQ + 128:_RK], cr_ref, sr_ref)
    put("rq", a1 * (RET_DK ** -0.5), slice(0, 128))
    put("rq", a2 * (RET_DK ** -0.5), slice(128, 256))
    b1, b2 = rope(z[:, _RK:_RK + 128], z[:, _RK + 128:_RV], cr_ref, sr_ref)
    put("rk", b1, slice(0, 128))
    put("rk", b2, slice(128, 256))
    put("rv", z[:, _RV:_RG])
    put("rg", z[:, _RG:_SQ])
    put("sq", z[:, _SQ:_SK] * (SB_DK ** -0.5))
    put("sk", z[:, _SK:_SV])
    put("sv", z[:, _SV:_KRT])


def _inproj(x, mod, g1, win, gq, wuq, gkv, wuk, wvt, cm, sm, cr, sr, earlier):
    b, l, d = x.shape
    tl = min(l, ROW_TILE)
    bb = ROW_TILE // tl
    grid = (b // bb, l // tl)
    depth = None if earlier is None else len(earlier) + 1
    tok = lambda w: pl.BlockSpec((bb, tl, w), lambda i, t: (i, t, 0))
    stk = lambda w: pl.BlockSpec((depth, bb, tl, w), lambda i, t: (0, i, t, 0))
    pos = pl.BlockSpec((tl, 128), lambda i, t: (t, 0))
    pos2 = pl.BlockSpec((tl, 256), lambda i, t: (t, 0))
    stacked = lambda st: st and depth is not None
    extra = [] if earlier is None else [a for row in earlier for a in row]
    tkk = min(l, ATT_TILE)
    vt_shape = jax.ShapeDtypeStruct((b, l // tkk, MLA_HEADS * MLA_V, tkk), BF16)
    vt_spec = pl.BlockSpec((bb, tl // tkk, MLA_HEADS * MLA_V, tkk), lambda i, t: (i, t, 0, 0))
    res = pl.pallas_call(
        _inproj_kernel,
        out_shape=[vt_shape if w is None else jax.ShapeDtypeStruct((depth, b, l, w) if stacked(st) else (b, l, w), dt)
                   for _, w, dt, st in _INPROJ_OUTPUTS],
        grid=grid,
        in_specs=[tok(d), pl.BlockSpec((bb, 6, d), lambda i, t: (i, 0, 0)), _const_spec(g1.shape),
                  _const_spec(win.shape), _const_spec(gq.shape), _const_spec(wuq.shape), _const_spec(gkv.shape),
                  _const_spec(wuk.shape), _const_spec(wvt.shape), pos2, pos2, pos, pos]
        + [tok(a.shape[-1]) for a in extra],
        out_specs=[vt_spec if w is None else (stk(w) if stacked(st) else tok(w)) for _, w, _, st in _INPROJ_OUTPUTS],
        compiler_params=_params("parallel", "parallel"),
        name="in_proj",
    )(x, mod, g1, win, gq, wuq, gkv, wuk, wvt, cm, sm, cr, sr, *extra)
    return dict(zip([o[0] for o in _INPROJ_OUTPUTS], res))


def _kvup_kernel(lat_ref, krt_ref, wuk_ref, wvt_ref, kp_ref, vt_ref):
    lat_b = lat_ref[0].astype(BF16)
    kn = _dot(lat_b, wuk_ref[...])
    for p in range(MLA_HEADS // 2):
        kp_ref[0, :, 256 * p:256 * p + 128] = kn[:, 128 * p:128 * (p + 1)].astype(kp_ref.dtype)
        kp_ref[0, :, 256 * p + 128:256 * (p + 1)] = krt_ref[0]
    tkk = vt_ref.shape[-1]
    for kk in range(vt_ref.shape[1]):
        vt_ref[0, kk] = _dot_nt(wvt_ref[...], lat_b[kk * tkk:(kk + 1) * tkk]).astype(vt_ref.dtype)


def _kvup(lat, layer, krt, wuk, wvt):
    _, b, lk, r = lat.shape
    nv = MLA_HEADS * MLA_V
    blk = lambda w: pl.BlockSpec((1, lk, w), lambda i: (i, 0, 0))
    return pl.pallas_call(
        _kvup_kernel,
        out_shape=(jax.ShapeDtypeStruct((b, lk, 1024), BF16),
                   jax.ShapeDtypeStruct((b, lk // ATT_TILE, nv, ATT_TILE), BF16)),
        grid=(b,),
        in_specs=[pl.BlockSpec((None, 1, lk, r), lambda i: (layer, i, 0, 0)), blk(128), _const_spec(wuk.shape),
                  _const_spec(wvt.shape)],
        out_specs=[blk(1024), pl.BlockSpec((1, lk // ATT_TILE, nv, ATT_TILE), lambda i: (i, 0, 0, 0))],
        compiler_params=_params("parallel"),
        name="kv_up",
    )(lat, krt, wuk, wvt)


def _fori_pairs(n, body, init):
    carry = lax.fori_loop(0, n // 2, lambda i, c: body(2 * i + 1, body(2 * i, c)), init)
    return lax.fori_loop(0, n % 2, lambda i, c: body(n - 1, c), carry)


def _fold8(reduce, x):
    return reduce(x.reshape(x.shape[0] // 8, 8, x.shape[1]), axis=0)


def _mla_kernel(qn_ref, qr_ref, kp_ref, vtp_ref, kd_ref, vtd_ref, o_ref, s_scr, q_scr, mx_scr, m_scr, l_scr, acc_scr,
                *, tq, tk, off, fused, hp):
    qi = pl.program_id(1)
    nfull = (off + qi * tq) // tk
    ntile = nfull + 1 if fused else nfull
    nslot = 4 // hp
    wl = hp * tq
    lane = lax.broadcasted_iota(jnp.int32, (1, 128), 1)
    key_c = lax.broadcasted_iota(jnp.int32, (tq, 1), 0) // CHUNK
    qry_c = (lax.broadcasted_iota(jnp.int32, (1, wl), 1) % tq) // CHUNK
    dmask = key_c <= qry_c
    neg = -1e30
    lane_v = lax.broadcasted_iota(jnp.int32, (1, HEAD_W), 1)
    feat = lax.broadcasted_iota(jnp.int32, (MLA_V, 1), 0)
    for g in range(MLA_HEADS // 4):
        gs = slice(HEAD_W * g, HEAD_W * (g + 1))
        qr = qr_ref[0, :, 128 * g:128 * (g + 1)]
        for hh in range(4):
            qn = qn_ref[0, :, 128 * (2 * g + hh // 2):128 * (2 * g + hh // 2 + 1)]
            rows = slice((hh % hp) * tq, (hh % hp + 1) * tq)
            q_scr[hh // hp, rows, 0:128] = jnp.where((lane >> 6) == hh % 2, qn, jnp.zeros_like(qn))
            q_scr[hh // hp, rows, 128:256] = jnp.where((lane >> 5) == hh, qr, jnp.zeros_like(qr))

        def slot_scores(keys_of_pair, u):
            return _dot_nt(keys_of_pair(2 * g + (u * hp) // 2), q_scr[u])

        def vt_rows(u):
            return slice(MLA_V * (4 * g + u * hp), MLA_V * (4 * g + (u + 1) * hp))

        def scores(j, carry):
            ks = pl.ds(pl.multiple_of(j * tk, tk), tk)
            for u in range(nslot):
                s = slot_scores(lambda p: kp_ref[0, ks, 256 * p:256 * (p + 1)], u)
                s_scr[j, u] = s
                mx_scr[u] = jnp.maximum(mx_scr[u], _fold8(jnp.max, s))
            return carry

        own = []
        for u in range(nslot):
            if fused:
                ks_d = pl.ds(pl.multiple_of(nfull * tk, tk), tk)
                s_d = jnp.where(dmask, slot_scores(lambda p: kp_ref[0, ks_d, 256 * p:256 * (p + 1)], u), neg)
                s_scr[nfull, u] = s_d
            else:
                s_d = jnp.where(dmask, slot_scores(lambda p: kd_ref[0, :, 256 * p:256 * (p + 1)], u), neg)
                own.append(s_d)
            mx_scr[u] = _fold8(jnp.max, s_d)
        _fori_pairs(nfull, scores, 0)
        for u in range(nslot):
            m_scr[u] = jnp.broadcast_to(jnp.max(mx_scr[u], axis=0, keepdims=True), (8, wl))

        def values(j, carry):
            for u in range(nslot):
                p = jnp.exp2(s_scr[j, u] - m_scr[u, 0:1, :])
                l_scr[u] += _fold8(jnp.sum, p)
                acc_scr[u] += _dot(vtp_ref[0, j, vt_rows(u), :], p.astype(BF16))
            return carry

        for u in range(nslot):
            if fused:
                l_scr[u] = jnp.zeros((8, wl), F32)
                acc_scr[u] = jnp.zeros((hp * MLA_V, wl), F32)
            else:
                p = jnp.exp2(own[u] - m_scr[u, 0:1, :])
                l_scr[u] = _fold8(jnp.sum, p)
                acc_scr[u] = _dot(vtd_ref[0, 0, vt_rows(u), :], p.astype(BF16))
        _fori_pairs(ntile, values, 0)
        out_g = jnp.zeros((tq, HEAD_W), F32)
        for hh in range(4):
            u, a = hh // hp, hh % hp
            inv = 1.0 / jnp.sum(l_scr[u], axis=0, keepdims=True)
            place = jnp.where(feat + MLA_V * hh == lane_v, 1.0, 0.0).astype(BF16)
            placed = _dot_tn((acc_scr[u, MLA_V * a:MLA_V * (a + 1), :] * inv).astype(BF16), place)
            out_g = out_g + placed[a * tq:(a + 1) * tq]
        o_ref[0, :, gs] = out_g.astype(o_ref.dtype)


def _mla(qn, qr, past, diag, off):
    b, l, _ = qn.shape
    lp = past[0].shape[1]
    tq = min(l, ATT_TILE)
    tk = ATT_TILE
    nv = MLA_HEADS * MLA_V
    fused = off == 0 and tq == tk
    qblk = lambda w: pl.BlockSpec((1, tq, w), lambda i, t: (i, t, 0))
    nmax = (off + l - tq) // tk + (1 if fused else 0)
    hp = 1 if tq % 128 == 0 else 2
    wl = hp * tq
    kern = functools.partial(_mla_kernel, tq=tq, tk=tk, off=off, fused=fused, hp=hp)
    return pl.pallas_call(
        kern,
        out_shape=jax.ShapeDtypeStruct((b, l, 512), BF16),
        grid=(b, l // tq),
        in_specs=[qblk(512), qblk(256), pl.BlockSpec((1, lp, 1024), lambda i, t: (i, 0, 0)),
                  pl.BlockSpec((1, lp // tk, nv, tk), lambda i, t: (i, 0, 0, 0)), qblk(1024),
                  pl.BlockSpec((1, 1, nv, tq), lambda i, t: (i, t, 0, 0))],
        out_specs=qblk(512),
        scratch_shapes=[pltpu.VMEM((nmax, 4 // hp, tk, wl), F32), pltpu.VMEM((4 // hp, wl, 256), BF16),
                        pltpu.VMEM((4 // hp, 8, wl), F32), pltpu.VMEM((4 // hp, 8, wl), F32),
                        pltpu.VMEM((4 // hp, 8, wl), F32), pltpu.VMEM((4 // hp, hp * MLA_V, wl), F32)],
        compiler_params=_params("parallel", "parallel"),
        name="mla_attn",
    )(qn, qr, *past, *diag)


def _neg_suffix_matrix(n):
    j = lax.broadcasted_iota(jnp.int32, (n, n), 0)
    s = lax.broadcasted_iota(jnp.int32, (n, n), 1)
    return jnp.where(j >= s, -1.0, 0.0).astype(BF16)


def _sb_kernel(q_ref, kp_ref, vp_ref, kd_ref, vd_ref, o_ref, kb_ref, vb_ref, q_scr, run_scr, acc_scr, *, tq, tk, off):
    qi = pl.program_id(1)

    @pl.when(qi == 0)
    def _():
        kb_ref[...] = kp_ref[0].astype(BF16)
        vb_ref[...] = vp_ref[0].astype(BF16)

    nfull = (off + qi * tq) // tk
    lane = lax.broadcasted_iota(jnp.int32, (1, HEAD_W), 1)
    nh = SB_HEADS
    dmask = (lax.broadcasted_iota(jnp.int32, (1, tq), 1)
             < lax.broadcasted_iota(jnp.int32, (nh * tq, 1), 0) % tq)
    t_diag = _neg_suffix_matrix(tq)
    t_full = t_diag if tk == tq else _neg_suffix_matrix(tk)
    q = q_ref[0]

    def tile(k, v, mask, tmat, first):
        zz = _dot_nt(q_scr[...], k)
        sp = jnp.maximum(zz, 0.0) + jnp.log(1.0 + jnp.exp(-jnp.abs(zz)))
        if mask is not None:
            sp = jnp.where(mask, sp, 0.0)
        hi = sp.astype(BF16)
        lo = (sp - hi.astype(F32)).astype(BF16)
        incl = _dot(hi, tmat) + _dot(lo, tmat)
        if first:
            w = jnp.exp(zz + incl)
            run = jnp.zeros((nh * tq, 128), F32)
        else:
            run = run_scr[...]
            w = jnp.exp(zz + incl + (run if zz.shape[1] == 128 else jnp.concatenate([run] * (zz.shape[1] // 128), -1)))
        if mask is not None:
            w = jnp.where(mask, w, 0.0)
        pv = _dot(w.astype(BF16), v)
        run = run - jnp.sum(sp, axis=-1, keepdims=True)
        run_scr[...] = run
        if first:
            acc_scr[...] = pv
        else:
            acc_scr[...] += pv
        return jnp.max(run)

    for h in range(nh):
        q_scr[h * tq:(h + 1) * tq, :] = jnp.where((lane >> 6) == h, q, jnp.zeros_like(q))
    top = tile(kd_ref[0].astype(BF16), vd_ref[0].astype(BF16), dmask, t_diag, True)

    def more(state):
        j, top = state
        return jnp.logical_and(j >= 0, top > SB_LOG_ZERO)

    def body(state):
        j, _ = state
        ks = pl.ds(pl.multiple_of(j * tk, tk), tk)
        return j - 1, tile(kb_ref[ks, :], vb_ref[ks, :], None, t_full, False)

    lax.while_loop(more, body, (nfull - 1, top))
    out = jnp.zeros((tq, HEAD_W), F32)
    for h in range(nh):
        out = jnp.where((lane >> 6) == h, acc_scr[h * tq:(h + 1) * tq, :], out)
    o_ref[0] = out.astype(o_ref.dtype)


def _sb(sq, past, diag, off):
    b, l, _ = sq.shape
    lp = past[0].shape[2]
    tq = min(l, ATT_TILE)
    qblk = pl.BlockSpec((1, tq, HEAD_W), lambda i, t: (i, t, 0))
    dblk = pl.BlockSpec((None, 1, tq, HEAD_W), lambda i, t: (diag[2], i, t, 0))
    kblk = pl.BlockSpec((None, 1, lp, HEAD_W), lambda i, t: (past[2], i, 0, 0))
    kern = functools.partial(_sb_kernel, tq=tq, tk=ATT_TILE, off=off)
    return pl.pallas_call(
        kern,
        out_shape=jax.ShapeDtypeStruct((b, l, HEAD_W), BF16),
        grid=(b, l // tq),
        in_specs=[qblk, kblk, kblk, dblk, dblk],
        out_specs=qblk,
        scratch_shapes=[pltpu.VMEM((lp, HEAD_W), BF16), pltpu.VMEM((lp, HEAD_W), BF16),
                        pltpu.VMEM((SB_HEADS * tq, HEAD_W), BF16), pltpu.VMEM((SB_HEADS * tq, 128), F32),
                        pltpu.VMEM((SB_HEADS * tq, HEAD_W), F32)],
        compiler_params=_params("parallel", "arbitrary"),
        name="sb_attn",
    )(sq, *past[:2], *diag[:2])


def _group_mean(x, avg):
    hi = x.astype(BF16)
    lo = (x - hi.astype(F32)).astype(BF16)
    return _dot(hi, avg) + _dot(lo, avg)


def _ret_kernel(q_ref, k_ref, v_ref, g_ref, s0_ref, gn_ref, dec_ref, qd_ref, kd_ref, sd_ref,
                o_ref, s_ref, *, c):
    n = q_ref.shape[1] // c
    lane = lax.broadcasted_iota(jnp.int32, (1, HEAD_W), 1)
    krow = lax.broadcasted_iota(jnp.int32, (HEAD_W, 1), 0)
    bd_mask = ((krow & 127) >> 5) == (lane >> 6)
    avg = jnp.where((krow >> 6) == (lane >> 6), 1.0 / RET_DV, 0.0).astype(BF16)
    s_ref[0] = s0_ref[0]

    def step(i, carry):
        rs = pl.ds(pl.multiple_of(i * c, c), c)
        q = q_ref[0, rs, :]
        kf = k_ref[0, rs, :]
        k = kf.astype(BF16)
        v = v_ref[0, rs, :]
        state = s_ref[0]
        o = _dot(q, state.astype(BF16)) * qd_ref[...]
        qs = jnp.concatenate([jnp.where(((lane & 127) >> 5) == h, q, jnp.zeros_like(q)) for h in range(RET_HEADS)],
                             axis=0)
        pv = _dot((_dot_nt(qs, k) * dec_ref[...]).astype(BF16), v)
        for h in range(RET_HEADS):
            o = o + jnp.where((lane >> 6) == h, pv[h * c:(h + 1) * c], 0.0)
        upd = _dot_tn((kf * kd_ref[...]).astype(BF16), v)
        s_ref[0] = sd_ref[...] * state + jnp.where(bd_mask, upd, 0.0)
        dlt = o - _group_mean(o, avg)
        ro = dlt * lax.rsqrt(_group_mean(dlt * dlt, avg) + EPS) * gn_ref[...]
        o_ref[0, rs, :] = (_silu(g_ref[0, rs, :]) * ro).astype(o_ref.dtype)
        return carry

    _fori_pairs(n, step, 0)


def _ret_consts(c):
    lg = jnp.log(1.0 - 2.0 ** (-5.0 - jnp.arange(RET_HEADS, dtype=F32)))
    i = jnp.arange(c, dtype=F32)
    rel = i[:, None] - i[None, :]
    dec = jnp.where(rel >= 0, jnp.exp(lg[:, None, None] * jnp.maximum(rel, 0.0)), 0.0)
    v_head = jnp.arange(HEAD_W) // RET_DV
    k_head = (jnp.arange(HEAD_W) % 128) // (RET_DK // 2)
    qd = jnp.exp(lg[None, v_head] * (i[:, None] + 1.0))
    kd = jnp.exp(lg[None, k_head] * (c - 1.0 - i[:, None]))
    sd = jnp.exp(lg * c)[v_head][None, :]
    return dec, qd, kd, sd


def _ret(rq, rk, rv, rg, s0, gn):
    b, l, _ = rq.shape
    c = min(l, RET_CHUNK)
    dec, qd, kd, sd = _ret_consts(c)
    dec = dec.reshape(RET_HEADS * c, c)
    blk = pl.BlockSpec((1, l, HEAD_W), lambda i: (i, 0, 0))
    sblk = pl.BlockSpec((1, HEAD_W, HEAD_W), lambda i: (i, 0, 0))
    return pl.pallas_call(
        functools.partial(_ret_kernel, c=c),
        out_shape=[jax.ShapeDtypeStruct((b, l, HEAD_W), BF16), jax.ShapeDtypeStruct((b, HEAD_W, HEAD_W), F32)],
        grid=(b,),
        in_specs=[blk, blk, blk, blk, sblk, _const_spec(gn.shape), _const_spec(dec.shape), _const_spec(qd.shape),
                  _const_spec(kd.shape), _const_spec(sd.shape)],
        out_specs=[blk, sblk],
        compiler_params=_params("parallel"),
        name="retention",
    )(rq, rk, rv, rg, s0, gn, dec, qd, kd, sd)


def _ffn_kernel(mla_ref, ret_ref, sb_ref, x_ref, mod_ref, prev_ref, wo_ref, g2_ref, wup_ref, cw_ref, cb_ref,
                wdn_ref, gf_ref, o_ref, st_ref, a_scr, carry, *, final_norm):
    bb, tl, d = x_ref.shape
    rows = bb * tl
    t = pl.program_id(1)
    nt = pl.num_programs(1)

    @pl.when(t == 0)
    def _():
        carry[:, 6:8, :] = prev_ref[...]

    mod = mod_ref[...]
    cat = jnp.concatenate([mla_ref[...].reshape(rows, 512), ret_ref[...].reshape(rows, HEAD_W),
                           sb_ref[...].reshape(rows, HEAD_W)], axis=-1)
    x1 = x_ref[...] + mod[:, 2:3, :] * _dot(cat, wo_ref[...]).reshape(bb, tl, d)
    h = (_rms(x1) * g2_ref[...] * (1.0 + mod[:, 4:5, :]) + mod[:, 3:4, :]).reshape(rows, d).astype(BF16)
    f = jnp.zeros((rows, d), F32)
    for c0, c1 in FF_CHUNKS:
        cs, n = slice(c0, c1), c1 - c0
        a = _dot(h, wup_ref[:, cs]).reshape(bb, tl, n)
        b = _dot(h, wup_ref[:, D_FF + c0:D_FF + c1])
        a_scr[:, 8:, :n] = a
        a_scr[:, 6:8, :n] = carry[:, 6:8, cs]
        cw = cw_ref[...]
        conv = (cb_ref[:, cs] + cw[0:1, cs] * a_scr[:, 6:6 + tl, :n] + cw[1:2, cs] * a_scr[:, 7:7 + tl, :n]
                + cw[2:3, cs] * a)
        carry[:, 6:8, cs] = a_scr[:, tl + 6:tl + 8, :n]
        y = (_silu(conv).reshape(rows, n) * b).astype(BF16)
        f = f + _dot(y, wdn_ref[cs, :])
    x2 = x1 + mod[:, 5:6, :] * f.reshape(bb, tl, d)
    if final_norm:
        x2 = _rms(x2) * gf_ref[...]
    o_ref[...] = x2

    @pl.when(t == nt - 1)
    def _():
        st_ref[...] = carry[:, 6:8, :]


def _ffn(mla, ret, sb, x, mod, prev, wo, g2, wup, cw, cb, wdn, gf, final_norm):
    b, l, d = x.shape
    tl = min(l, ROW_TILE)
    bb = ROW_TILE // tl
    tok = lambda w: pl.BlockSpec((bb, tl, w), lambda i, t: (i, t, 0))
    st = pl.BlockSpec((bb, CONV_W - 1, D_FF), lambda i, t: (i, 0, 0))
    once = lambda a: pl.BlockSpec(a.shape, lambda i, t: (0, 0), pipeline_mode=pl.Buffered(1))
    return pl.pallas_call(
        functools.partial(_ffn_kernel, final_norm=final_norm),
        out_shape=[jax.ShapeDtypeStruct((b, l, d), F32), jax.ShapeDtypeStruct((b, CONV_W - 1, D_FF), F32)],
        grid=(b // bb, l // tl),
        in_specs=[tok(512), tok(HEAD_W), tok(HEAD_W), tok(d), pl.BlockSpec((bb, 6, d), lambda i, t: (i, 0, 0)), st,
                  once(wo), _const_spec(g2.shape), once(wup), _const_spec(cw.shape), _const_spec(cb.shape),
                  once(wdn), _const_spec(gf.shape)],
        out_specs=[tok(d), st],
        scratch_shapes=[pltpu.VMEM((bb, tl + 8, max(c1 - c0 for c0, c1 in FF_CHUNKS)), F32),
                        pltpu.VMEM((bb, 8, D_FF), F32)],
        compiler_params=_params("parallel", "arbitrary"),
        name="conv_ffn",
    )(mla, ret, sb, x, mod, prev, wo, g2, wup, cw, cb, wdn, gf)


def _half_split(w, heads):
    lead = w.shape[:-1]
    dim = w.shape[-1] // heads
    w = w.reshape(lead + (heads, 2, dim // 2))
    return jnp.swapaxes(w, -3, -2).reshape(lead + (heads * dim,))


def _permute_win(w):
    half = MLA_ROPE // 2
    kr = w[..., 640:672]
    kr_swapped = jnp.concatenate([kr[..., half:], kr[..., :half]], axis=-1)
    tiled = lambda a: jnp.tile(a, (1, 1, 4))
    return jnp.concatenate([w[..., 0:640], _half_split(w[..., 672:928], RET_HEADS),
                            _half_split(w[..., 928:1184], RET_HEADS), w[..., 1184:2464],
                            tiled(kr), tiled(kr_swapped)], axis=-1)


def _permute_wuq(w):
    half = MLA_ROPE // 2
    w = w.reshape(w.shape[:-1] + (MLA_HEADS, MLA_NOPE + MLA_ROPE))
    flat = lambda a: a.reshape(a.shape[:-2] + (-1,))
    rope = w[..., MLA_NOPE:]
    swapped = jnp.concatenate([rope[..., half:], rope[..., :half]], axis=-1)
    return jnp.concatenate([flat(w[..., :MLA_NOPE]), flat(rope), flat(swapped)], axis=-1)


def _permute_wukv(w):
    w = w.reshape(w.shape[:-1] + (MLA_HEADS, MLA_NOPE + MLA_V))
    flat = lambda a: a.reshape(a.shape[:-2] + (-1,))
    return jnp.concatenate([flat(w[..., :MLA_NOPE]), flat(w[..., MLA_NOPE:])], axis=-1)


def _ret_state_in(st):
    b = st.shape[0]
    st = st.reshape(b, RET_HEADS, 2, RET_DK // 2, RET_DV).transpose(0, 2, 1, 3, 4)
    eye = jnp.eye(RET_HEADS, dtype=st.dtype)
    full = st[:, :, :, :, None, :] * eye[None, None, :, None, :, None]
    return full.reshape(b, HEAD_W, HEAD_W)


def _ret_state_out(s):
    b = s.shape[0]
    s = s.reshape(b, 2, RET_HEADS, RET_DK // 2, RET_HEADS, RET_DV)
    blocks = [s[:, :, h, :, h, :].reshape(b, RET_DK, RET_DV) for h in range(RET_HEADS)]
    return jnp.stack(blocks, axis=1)


def _rope_angles(pos, dim):
    inv = ROPE_BASE ** (-jnp.arange(0, dim, 2, dtype=F32) / dim)
    return pos.astype(F32)[:, None] * inv[None, :]


def _rope_tables(pos, dim, reps):
    ang = _rope_angles(pos, dim)
    return jnp.tile(jnp.cos(ang), (1, reps)), jnp.tile(jnp.sin(ang), (1, reps))


def _rope_tables_swapped(pos, dim, reps):
    ang = _rope_angles(pos, dim)
    c, s = jnp.cos(ang), jnp.sin(ang)
    return jnp.tile(jnp.concatenate([c, c], axis=1), (1, reps)), jnp.tile(jnp.concatenate([-s, s], axis=1), (1, reps))


def _trunk(x, mods, pos0, cache, w):
    b, l, _ = x.shape
    depth = len(mods)
    pos = pos0 + jnp.arange(l)
    cm, sm = _rope_tables_swapped(pos, MLA_ROPE, MLA_HEADS)
    cr, sr = _rope_tables(pos, RET_DK, RET_HEADS)
    if cache is None:
        off = 0
    else:
        off = cache["lat"].shape[2]
        krt_cache = jnp.tile(cache["kr"], (1, 1, 1, 4)).astype(BF16)
        sb_cache = (cache["sk"].reshape(depth, b, off, HEAD_W), cache["sv"].reshape(depth, b, off, HEAD_W))
    earlier = []
    small = [[], []]
    for layer in range(depth):
        mod = mods[layer]
        last = layer == depth - 1
        p = _inproj(x, mod, w["g_norm1"][layer], w["w_in"][layer], w["g_q_norm"][layer], w["w_uq"][layer],
                    w["g_kv_norm"][layer], w["w_uk"][layer], w["w_vt"][layer], cm, sm, cr, sr,
                    earlier if last else None)
        if last:
            sb_new = (p["sk"], p["sv"], layer)
        else:
            earlier.append((p["lat"], p["kr"], p["sk"], p["sv"]))
            sb_new = (p["sk"][None], p["sv"][None], 0)
        if cache is None:
            mla_past, sb_past = (p["kp"], p["vt"]), sb_new
            s0 = jnp.zeros((b, HEAD_W, HEAD_W), F32)
            prev = jnp.zeros((b, CONV_W - 1, D_FF), F32)
        else:
            mla_past = _kvup(cache["lat"], layer, krt_cache[layer], w["w_uk"][layer], w["w_vt"][layer])
            sb_past = sb_cache + (layer,)
            s0 = _ret_state_in(cache["S"][layer])
            prev = cache["conv"][layer]
        mla = _mla(p["qn"], p["qr"], mla_past, (p["kp"], p["vt"]), off)
        sb = _sb(p["sq"], sb_past, sb_new, off)
        ret, s_new = _ret(p["rq"], p["rk"], p["rv"], p["rg"], s0, w["g_ret_norm"][layer])
        x, conv_state = _ffn(mla, ret, sb, x, mod, prev, w["w_o"][layer], w["g_norm2"][layer], w["w_up"][layer],
                             w["conv_w"][layer], w["conv_b"][layer], w["w_down"][layer], w["g_final"],
                             layer == depth - 1)
        small[0].append(_ret_state_out(s_new))
        small[1].append(conv_state)
    lat, kr, sk, sv = p["lat"], p["kr"], p["sk"], p["sv"]
    return x, [lat, kr, sk.reshape(depth, b, l, SB_HEADS, SB_DK), sv.reshape(depth, b, l, SB_HEADS, SB_DV),
               jnp.stack(small[0], axis=0), jnp.stack(small[1], axis=0)]


def kernel(x_prompt, x_sample, c_prompt, c_sample, cache_mla_latent, cache_mla_krope, cache_sb_k, cache_sb_v, state_ret, state_ffn_conv, w_in, g_q_norm, w_uq, g_kv_norm, w_ukv, g_ret_norm, w_o, w_up, conv_w, conv_b, w_down, g_norm1, g_norm2, w_ada, b_ada, g_final):
    depth = w_in.shape[0]
    bp = x_prompt.shape[0]
    row = lambda g: g.reshape(g.shape[0], 1, g.shape[-1])
    w = dict(
        w_in=_permute_win(w_in).astype(BF16),
        w_uq=_permute_wuq(w_uq).astype(BF16),
        w_uk=_permute_wukv(w_ukv)[..., :MLA_HEADS * MLA_NOPE].astype(BF16),
        w_vt=jnp.swapaxes(_permute_wukv(w_ukv)[..., MLA_HEADS * MLA_NOPE:], -1, -2).astype(BF16),
        w_o=w_o.astype(BF16), w_up=w_up.astype(BF16), w_down=w_down.astype(BF16),
        g_q_norm=row(g_q_norm), g_kv_norm=row(g_kv_norm), g_ret_norm=row(g_ret_norm),
        g_norm1=row(g_norm1), g_norm2=row(g_norm2), conv_w=conv_w, conv_b=row(conv_b),
        g_final=g_final.reshape(1, -1))
    mod = _ada(jnp.concatenate([c_prompt, c_sample], axis=0), w_ada, b_ada)
    mod = mod.reshape(depth, mod.shape[1], 6, D_MODEL)
    y_p, st_p = _trunk(x_prompt, [mod[l, :bp] for l in range(depth)], 0, None, w)
    cache = dict(lat=cache_mla_latent, kr=cache_mla_krope, sk=cache_sb_k, sv=cache_sb_v, S=state_ret,
                 conv=state_ffn_conv)
    y_s, st_s = _trunk(x_sample, [mod[l, bp:] for l in range(depth)], cache_mla_latent.shape[2], cache, w)
    return (y_p, y_s, *st_p, *st_s)
```

```python
import functools

import jax
import jax.numpy as jnp
from jax import lax
from jax.experimental import pallas as pl
from jax.experimental.pallas import tpu as pltpu

D_MODEL = 1024
CHUNK = 64
MLA_HEADS = 8
MLA_NOPE = 64
MLA_ROPE = 32
MLA_V = 64
MLA_Q_RANK = 384
MLA_KV_RANK = 256
RET_HEADS = 4
RET_DK = 64
RET_DV = 64
SB_HEADS = 4
SB_DK = 64
SB_DV = 64
D_FF = 2816
CONV_W = 3
ROPE_BASE = 10000.0
EPS = 1e-6

HEAD_W = 256
ROW_TILE = 512
ATT_TILE = 256
RET_CHUNK = 256
FF_CHUNKS = ((0, 1536), (1536, D_FF))
MLA_EXP2_SCALE = (MLA_NOPE + MLA_ROPE) ** -0.5 * 1.4426950408889634
SB_LOG_ZERO = -104.0
VMEM_LIMIT = 56 * 1024 * 1024

F32 = jnp.float32
BF16 = jnp.bfloat16

_ZQ, _ZKV, _RQ, _RK, _RV, _RG, _SQ, _SK, _SV, _KRT, _IN_COLS_P = (
    0, 384, 640, 896, 1152, 1408, 1664, 1920, 2176, 2432, 2688)


def _dot(a, b):
    return jnp.dot(a, b, preferred_element_type=F32)


def _dot_nt(a, b):
    return lax.dot_general(a, b, (((1,), (1,)), ((), ())), preferred_element_type=F32)


def _dot_tn(a, b):
    return lax.dot_general(a, b, (((0,), (0,)), ((), ())), preferred_element_type=F32)


def _rms(x):
    return x * lax.rsqrt(jnp.mean(x * x, axis=-1, keepdims=True) + EPS)


def _silu(x):
    return x / (1.0 + jnp.exp(-x))


def _params(*sem):
    return pltpu.CompilerParams(dimension_semantics=sem, vmem_limit_bytes=VMEM_LIMIT)


def _const_spec(shape):
    nd = len(shape)
    return pl.BlockSpec(shape, lambda *_: (0,) * nd)


def _ada_kernel(c_ref, w_ref, b_ref, o_ref):
    a = _silu(c_ref[...]).astype(BF16)
    o_ref[0] = _dot(a, w_ref[0].astype(BF16)) + b_ref[0]


def _ada(c_all, w_ada, b_ada):
    depth, d, n = w_ada.shape
    rows = c_all.shape[0]
    tn = 1536
    return pl.pallas_call(
        _ada_kernel,
        out_shape=jax.ShapeDtypeStruct((depth, rows, n), F32),
        grid=(depth, n // tn),
        in_specs=[pl.BlockSpec((rows, d), lambda l, j: (0, 0)),
                  pl.BlockSpec((1, d, tn), lambda l, j: (l, 0, j)),
                  pl.BlockSpec((1, 1, tn), lambda l, j: (l, 0, j))],
        out_specs=pl.BlockSpec((1, rows, tn), lambda l, j: (l, 0, j)),
        compiler_params=_params("parallel", "parallel"),
        name="ada_mod",
    )(c_all, w_ada, b_ada.reshape(depth, 1, n))


_INPROJ_INPUTS = 13
_INPROJ_OUTPUTS = (("qn", 512, BF16, False), ("qr", 256, BF16, False), ("lat", MLA_KV_RANK, F32, True),
                   ("kp", 1024, BF16, False), ("vt", None, BF16, False), ("kr", MLA_ROPE, F32, True),
                   ("rq", HEAD_W, BF16, False), ("rk", HEAD_W, F32, False), ("rv", HEAD_W, BF16, False),
                   ("rg", HEAD_W, F32, False), ("sq", HEAD_W, BF16, False), ("sk", HEAD_W, F32, True),
                   ("sv", HEAD_W, F32, True))


def _inproj_kernel(*refs):
    (x_ref, mod_ref, g1_ref, win_ref, gq_ref, wuq_ref, gkv_ref, wuk_ref, wvt_ref, cm_ref, sm_ref, cr_ref,
     sr_ref) = refs[:_INPROJ_INPUTS]
    names = [o[0] for o in _INPROJ_OUTPUTS]
    out = dict(zip(names, refs[len(refs) - len(names):]))
    stacked = [o[0] for o in _INPROJ_OUTPUTS if o[3]]
    earlier = refs[_INPROJ_INPUTS:len(refs) - len(names)]
    layer = len(earlier) // len(stacked)
    for r in range(layer):
        for k, name in enumerate(stacked):
            out[name][r] = earlier[r * len(stacked) + k][...]
    bb, tl, d = x_ref.shape
    rows = bb * tl
    mod = mod_ref[...]
    h = _rms(x_ref[...]) * g1_ref[...] * (1.0 + mod[:, 1:2, :]) + mod[:, 0:1, :]
    z = _dot(h.reshape(rows, d).astype(BF16), win_ref[...])

    def put(name, val, lanes=slice(None)):
        ref = out[name]
        lead = (layer,) if len(ref.shape) == 4 else ()
        ref[lead + (slice(None), slice(None), lanes)] = val.reshape(bb, tl, val.shape[-1]).astype(ref.dtype)

    def rope(x1, x2, c_ref, s_ref):
        c, s = c_ref[...][None], s_ref[...][None]
        x1 = x1.reshape(bb, tl, x1.shape[-1])
        x2 = x2.reshape(bb, tl, x2.shape[-1])
        return x1 * c - x2 * s, x1 * s + x2 * c

    def rope_swapped(a, b, width):
        a = a.reshape(bb, tl, width)
        b = b.reshape(bb, tl, width)
        return a * cm_ref[:, :width][None] + b * sm_ref[:, :width][None]

    q = _dot((_rms(z[:, _ZQ:_ZKV]) * gq_ref[...]).astype(BF16), wuq_ref[...])
    put("qn", q[:, :512] * MLA_EXP2_SCALE)
    put("qr", rope_swapped(q[:, 512:768], q[:, 768:1024], 256) * MLA_EXP2_SCALE)
    lat = _rms(z[:, _ZKV:_RQ]) * gkv_ref[...]
    put("lat", lat)
    kr = rope_swapped(z[:, _KRT:_KRT + 128], z[:, _KRT + 128:_KRT + 256], 128)
    put("kr", kr[:, :, :MLA_ROPE])
    lat_b = lat.astype(BF16)
    kn = _dot(lat_b, wuk_ref[...])
    for p in range(MLA_HEADS // 2):
        put("kp", kn[:, 128 * p:128 * (p + 1)], slice(256 * p, 256 * p + 128))
        put("kp", kr, slice(256 * p + 128, 256 * (p + 1)))
    vt_ref = out["vt"]
    tkk = vt_ref.shape[-1]
    for bi in range(bb):
        for kk in range(tl // tkk):
            r0 = bi * tl + kk * tkk
            vt_ref[bi, kk] = _dot_nt(wvt_ref[...], lat_b[r0:r0 + tkk]).astype(vt_ref.dtype)
    a1, a2 = rope(z[:, _RQ:_RQ + 128], z[:, _RQ + 128:_RK], cr_ref, sr_ref)
    put("rq", a1 * (RET_DK ** -0.5), slice(0, 128))
    put("rq", a2 * (RET_DK ** -0.5), slice(128, 256))
    b1, b2 = rope(z[:, _RK:_RK + 128], z[:, _RK + 128:_RV], cr_ref, sr_ref)
    put("rk", b1, slice(0, 128))
    put("rk", b2, slice(128, 256))
    put("rv", z[:, _RV:_RG])
    put("rg", z[:, _RG:_SQ])
    put("sq", z[:, _SQ:_SK] * (SB_DK ** -0.5))
    put("sk", z[:, _SK:_SV])
    put("sv", z[:, _SV:_KRT])


def _inproj(x, mod, g1, win, gq, wuq, gkv, wuk, wvt, cm, sm, cr, sr, earlier):
    b, l, d = x.shape
    tl = min(l, ROW_TILE)
    bb = ROW_TILE // tl
    grid = (b // bb, l // tl)
    depth = None if earlier is None else len(earlier) + 1
    tok = lambda w: pl.BlockSpec((bb, tl, w), lambda i, t: (i, t, 0))
    stk = lambda w: pl.BlockSpec((depth, bb, tl, w), lambda i, t: (0, i, t, 0))
    pos = pl.BlockSpec((tl, 128), lambda i, t: (t, 0))
    pos2 = pl.BlockSpec((tl, 256), lambda i, t: (t, 0))
    stacked = lambda st: st and depth is not None
    extra = [] if earlier is None else [a for row in earlier for a in row]
    tkk = min(l, ATT_TILE)
    vt_shape = jax.ShapeDtypeStruct((b, l // tkk, MLA_HEADS * MLA_V, tkk), BF16)
    vt_spec = pl.BlockSpec((bb, tl // tkk, MLA_HEADS * MLA_V, tkk), lambda i, t: (i, t, 0, 0))
    res = pl.pallas_call(
        _inproj_kernel,
        out_shape=[vt_shape if w is None else jax.ShapeDtypeStruct((depth, b, l, w) if stacked(st) else (b, l, w), dt)
                   for _, w, dt, st in _INPROJ_OUTPUTS],
        grid=grid,
        in_specs=[tok(d), pl.BlockSpec((bb, 6, d), lambda i, t: (i, 0, 0)), _const_spec(g1.shape),
                  _const_spec(win.shape), _const_spec(gq.shape), _const_spec(wuq.shape), _const_spec(gkv.shape),
                  _const_spec(wuk.shape), _const_spec(wvt.shape), pos2, pos2, pos, pos]
        + [tok(a.shape[-1]) for a in extra],
        out_specs=[vt_spec if w is None else (stk(w) if stacked(st) else tok(w)) for _, w, _, st in _INPROJ_OUTPUTS],
        compiler_params=_params("parallel", "parallel"),
        name="in_proj",
    )(x, mod, g1, win, gq, wuq, gkv, wuk, wvt, cm, sm, cr, sr, *extra)
    return dict(zip([o[0] for o in _INPROJ_OUTPUTS], res))


def _kvup_kernel(lat_ref, krt_ref, wuk_ref, wvt_ref, kp_ref, vt_ref):
    lat_b = lat_ref[0].astype(BF16)
    kn = _dot(lat_b, wuk_ref[...])
    for p in range(MLA_HEADS // 2):
        kp_ref[0, :, 256 * p:256 * p + 128] = kn[:, 128 * p:128 * (p + 1)].astype(kp_ref.dtype)
        kp_ref[0, :, 256 * p + 128:256 * (p + 1)] = krt_ref[0]
    tkk = vt_ref.shape[-1]
    for kk in range(vt_ref.shape[1]):
        vt_ref[0, kk] = _dot_nt(wvt_ref[...], lat_b[kk * tkk:(kk + 1) * tkk]).astype(vt_ref.dtype)


def _kvup(lat, layer, krt, wuk, wvt):
    _, b, lk, r = lat.shape
    nv = MLA_HEADS * MLA_V
    blk = lambda w: pl.BlockSpec((1, lk, w), lambda i: (i, 0, 0))
    return pl.pallas_call(
        _kvup_kernel,
        out_shape=(jax.ShapeDtypeStruct((b, lk, 1024), BF16),
                   jax.ShapeDtypeStruct((b, lk // ATT_TILE, nv, ATT_TILE), BF16)),
        grid=(b,),
        in_specs=[pl.BlockSpec((None, 1, lk, r), lambda i: (layer, i, 0, 0)), blk(128), _const_spec(wuk.shape),
                  _const_spec(wvt.shape)],
        out_specs=[blk(1024), pl.BlockSpec((1, lk // ATT_TILE, nv, ATT_TILE), lambda i: (i, 0, 0, 0))],
        compiler_params=_params("parallel"),
        name="kv_up",
    )(lat, krt, wuk, wvt)


def _fori_pairs(n, body, init):
    carry = lax.fori_loop(0, n // 2, lambda i, c: body(2 * i + 1, body(2 * i, c)), init)
    return lax.fori_loop(0, n % 2, lambda i, c: body(n - 1, c), carry)


def _fold8(reduce, x):
    return reduce(x.reshape(x.shape[0] // 8, 8, x.shape[1]), axis=0)


def _mla_kernel(qn_ref, qr_ref, kp_ref, vtp_ref, kd_ref, vtd_ref, o_ref, s_scr, q_scr, mx_scr, m_scr, l_scr, acc_scr,
                *, tq, tk, off, fused, hp):
    qi = pl.program_id(1)
    nfull = (off + qi * tq) // tk
    ntile = nfull + 1 if fused else nfull
    nslot = 4 // hp
    wl = hp * tq
    lane = lax.broadcasted_iota(jnp.int32, (1, 128), 1)
    key_c = lax.broadcasted_iota(jnp.int32, (tq, 1), 0) // CHUNK
    qry_c = (lax.broadcasted_iota(jnp.int32, (1, wl), 1) % tq) // CHUNK
    dmask = key_c <= qry_c
    neg = -1e30
    lane_v = lax.broadcasted_iota(jnp.int32, (1, HEAD_W), 1)
    feat = lax.broadcasted_iota(jnp.int32, (MLA_V, 1), 0)
    for g in range(MLA_HEADS // 4):
        gs = slice(HEAD_W * g, HEAD_W * (g + 1))
        qr = qr_ref[0, :, 128 * g:128 * (g + 1)]
        for hh in range(4):
            qn = qn_ref[0, :, 128 * (2 * g + hh // 2):128 * (2 * g + hh // 2 + 1)]
            rows = slice((hh % hp) * tq, (hh % hp + 1) * tq)
            q_scr[hh // hp, rows, 0:128] = jnp.where((lane >> 6) == hh % 2, qn, jnp.zeros_like(qn))
            q_scr[hh // hp, rows, 128:256] = jnp.where((lane >> 5) == hh, qr, jnp.zeros_like(qr))

        def slot_scores(keys_of_pair, u):
            return _dot_nt(keys_of_pair(2 * g + (u * hp) // 2), q_scr[u])

        def vt_rows(u):
            return slice(MLA_V * (4 * g + u * hp), MLA_V * (4 * g + (u + 1) * hp))

        def scores(j, carry):
            ks = pl.ds(pl.multiple_of(j * tk, tk), tk)
            for u in range(nslot):
                s = slot_scores(lambda p: kp_ref[0, ks, 256 * p:256 * (p + 1)], u)
                s_scr[j, u] = s
                mx_scr[u] = jnp.maximum(mx_scr[u], _fold8(jnp.max, s))
            return carry

        own = []
        for u in range(nslot):
            if fused:
                ks_d = pl.ds(pl.multiple_of(nfull * tk, tk), tk)
                s_d = jnp.where(dmask, slot_scores(lambda p: kp_ref[0, ks_d, 256 * p:256 * (p + 1)], u), neg)
                s_scr[nfull, u] = s_d
            else:
                s_d = jnp.where(dmask, slot_scores(lambda p: kd_ref[0, :, 256 * p:256 * (p + 1)], u), neg)
                own.append(s_d)
            mx_scr[u] = _fold8(jnp.max, s_d)
        _fori_pairs(nfull, scores, 0)
        for u in range(nslot):
            m_scr[u] = jnp.broadcast_to(jnp.max(mx_scr[u], axis=0, keepdims=True), (8, wl))

        def values(j, carry):
            for u in range(nslot):
                p = jnp.exp2(s_scr[j, u] - m_scr[u, 0:1, :])
                l_scr[u] += _fold8(jnp.sum, p)
                acc_scr[u] += _dot(vtp_ref[0, j, vt_rows(u), :], p.astype(BF16))
            return carry

        for u in range(nslot):
            if fused:
                l_scr[u] = jnp.zeros((8, wl), F32)
                acc_scr[u] = jnp.zeros((hp * MLA_V, wl), F32)
            else:
                p = jnp.exp2(own[u] - m_scr[u, 0:1, :])
                l_scr[u] = _fold8(jnp.sum, p)
                acc_scr[u] = _dot(vtd_ref[0, 0, vt_rows(u), :], p.astype(BF16))
        _fori_pairs(ntile, values, 0)
        out_g = jnp.zeros((tq, HEAD_W), F32)
        for hh in range(4):
            u, a = hh // hp, hh % hp
            inv = 1.0 / jnp.sum(l_scr[u], axis=0, keepdims=True)
            place = jnp.where(feat + MLA_V * hh == lane_v, 1.0, 0.0).astype(BF16)
            placed = _dot_tn((acc_scr[u, MLA_V * a:MLA_V * (a + 1), :] * inv).astype(BF16), place)
            out_g = out_g + placed[a * tq:(a + 1) * tq]
        o_ref[0, :, gs] = out_g.astype(o_ref.dtype)


def _mla(qn, qr, past, diag, off):
    b, l, _ = qn.shape
    lp = past[0].shape[1]
    tq = min(l, ATT_TILE)
    tk = ATT_TILE
    nv = MLA_HEADS * MLA_V
    fused = off == 0 and tq == tk
    qblk = lambda w: pl.BlockSpec((1, tq, w), lambda i, t: (i, t, 0))
    nmax = (off + l - tq) // tk + (1 if fused else 0)
    hp = 1 if tq % 128 == 0 else 2
    wl = hp * tq
    kern = functools.partial(_mla_kernel, tq=tq, tk=tk, off=off, fused=fused, hp=hp)
    return pl.pallas_call(
        kern,
        out_shape=jax.ShapeDtypeStruct((b, l, 512), BF16),
        grid=(b, l // tq),
        in_specs=[qblk(512), qblk(256), pl.BlockSpec((1, lp, 1024), lambda i, t: (i, 0, 0)),
                  pl.BlockSpec((1, lp // tk, nv, tk), lambda i, t: (i, 0, 0, 0)), qblk(1024),
                  pl.BlockSpec((1, 1, nv, tq), lambda i, t: (i, t, 0, 0))],
        out_specs=qblk(512),
        scratch_shapes=[pltpu.VMEM((nmax, 4 // hp, tk, wl), F32), pltpu.VMEM((4 // hp, wl, 256), BF16),
                        pltpu.VMEM((4 // hp, 8, wl), F32), pltpu.VMEM((4 // hp, 8, wl), F32),
                        pltpu.VMEM((4 // hp, 8, wl), F32), pltpu.VMEM((4 // hp, hp * MLA_V, wl), F32)],
        compiler_params=_params("parallel", "parallel"),
        name="mla_attn",
    )(qn, qr, *past, *diag)


def _neg_suffix_matrix(n):
    j = lax.broadcasted_iota(jnp.int32, (n, n), 0)
    s = lax.broadcasted_iota(jnp.int32, (n, n), 1)
    return jnp.where(j >= s, -1.0, 0.0).astype(BF16)


def _sb_kernel(q_ref, kp_ref, vp_ref, kd_ref, vd_ref, o_ref, kb_ref, vb_ref, q_scr, run_scr, acc_scr, *, tq, tk, off):
    qi = pl.program_id(1)

    @pl.when(qi == 0)
    def _():
        kb_ref[...] = kp_ref[0].astype(BF16)
        vb_ref[...] = vp_ref[0].astype(BF16)

    nfull = (off + qi * tq) // tk
    lane = lax.broadcasted_iota(jnp.int32, (1, HEAD_W), 1)
    nh = SB_HEADS
    dmask = (lax.broadcasted_iota(jnp.int32, (1, tq), 1)
             < lax.broadcasted_iota(jnp.int32, (nh * tq, 1), 0) % tq)
    t_diag = _neg_suffix_matrix(tq)
    t_full = t_diag if tk == tq else _neg_suffix_matrix(tk)
    q = q_ref[0]

    def exponent(k, mask, tmat):
        zz = _dot_nt(q_scr[...], k)
        sp = jnp.maximum(zz, 0.0) + jnp.log(1.0 + jnp.exp(-jnp.abs(zz)))
        if mask is not None:
            sp = jnp.where(mask, sp, 0.0)
        hi = sp.astype(BF16)
        lo = (sp - hi.astype(F32)).astype(BF16)
        return zz + _dot(hi, tmat) + _dot(lo, tmat), jnp.sum(sp, axis=-1, keepdims=True)

    def carry_on(run):
        run_scr[...] = jnp.broadcast_to(run, (nh * tq, 128))
        return jnp.max(run)

    def own_weights():
        e, rs = exponent(kd_ref[0].astype(BF16), dmask, t_diag)
        return jnp.where(dmask, jnp.exp(e), 0.0).astype(BF16), rs

    def own_only():
        w, rs = own_weights()
        acc_scr[...] = _dot(w, vd_ref[0].astype(BF16))
        return carry_on(-rs)

    def own_and_previous():
        ks = pl.ds(pl.multiple_of((nfull - 1) * tk, tk), tk)
        w, rs = own_weights()
        e1, rs1 = exponent(kb_ref[ks, :], None, t_full)
        w1 = jnp.exp(e1 - rs).astype(BF16)
        acc_scr[...] = _dot(w, vd_ref[0].astype(BF16)) + _dot(w1, vb_ref[ks, :])
        return carry_on(-(rs + rs1))

    for h in range(nh):
        q_scr[h * tq:(h + 1) * tq, :] = jnp.where((lane >> 6) == h, q, jnp.zeros_like(q))
    top = lax.cond(nfull >= 1, own_and_previous, own_only)

    def more(state):
        j, top = state
        return jnp.logical_and(j >= 0, top > SB_LOG_ZERO)

    def body(state):
        j, _ = state
        ks = pl.ds(pl.multiple_of(j * tk, tk), tk)
        e, rs = exponent(kb_ref[ks, :], None, t_full)
        run = run_scr[...]
        w = jnp.exp(e + jnp.concatenate([run] * (tk // 128), -1))
        acc_scr[...] += _dot(w.astype(BF16), vb_ref[ks, :])
        run = run - rs
        run_scr[...] = run
        return j - 1, jnp.max(run)

    lax.while_loop(more, body, (nfull - 2, top))
    out = jnp.zeros((tq, HEAD_W), F32)
    for h in range(nh):
        out = jnp.where((lane >> 6) == h, acc_scr[h * tq:(h + 1) * tq, :], out)
    o_ref[0] = out.astype(o_ref.dtype)


def _sb(sq, past, diag, off):
    b, l, _ = sq.shape
    lp = past[0].shape[2]
    tq = min(l, ATT_TILE)
    qblk = pl.BlockSpec((1, tq, HEAD_W), lambda i, t: (i, t, 0))
    dblk = pl.BlockSpec((None, 1, tq, HEAD_W), lambda i, t: (diag[2], i, t, 0))
    kblk = pl.BlockSpec((None, 1, lp, HEAD_W), lambda i, t: (past[2], i, 0, 0))
    kern = functools.partial(_sb_kernel, tq=tq, tk=ATT_TILE, off=off)
    return pl.pallas_call(
        kern,
        out_shape=jax.ShapeDtypeStruct((b, l, HEAD_W), BF16),
        grid=(b, l // tq),
        in_specs=[qblk, kblk, kblk, dblk, dblk],
        out_specs=qblk,
        scratch_shapes=[pltpu.VMEM((lp, HEAD_W), BF16), pltpu.VMEM((lp, HEAD_W), BF16),
                        pltpu.VMEM((SB_HEADS * tq, HEAD_W), BF16), pltpu.VMEM((SB_HEADS * tq, 128), F32),
                        pltpu.VMEM((SB_HEADS * tq, HEAD_W), F32)],
        compiler_params=_params("parallel", "arbitrary"),
        name="sb_attn",
    )(sq, *past[:2], *diag[:2])


def _group_mean(x, avg):
    hi = x.astype(BF16)
    lo = (x - hi.astype(F32)).astype(BF16)
    return _dot(hi, avg) + _dot(lo, avg)


def _ret_kernel(q_ref, k_ref, v_ref, g_ref, s0_ref, gn_ref, dec_ref, qd_ref, kd_ref, sd_ref,
                o_ref, s_ref, *, c):
    n = q_ref.shape[1] // c
    lane = lax.broadcasted_iota(jnp.int32, (1, HEAD_W), 1)
    krow = lax.broadcasted_iota(jnp.int32, (HEAD_W, 1), 0)
    bd_mask = ((krow & 127) >> 5) == (lane >> 6)
    avg = jnp.where((krow >> 6) == (lane >> 6), 1.0 / RET_DV, 0.0).astype(BF16)
    s_ref[0] = s0_ref[0]

    def step(i, carry):
        rs = pl.ds(pl.multiple_of(i * c, c), c)
        q = q_ref[0, rs, :]
        kf = k_ref[0, rs, :]
        k = kf.astype(BF16)
        v = v_ref[0, rs, :]
        state = s_ref[0]
        o = _dot(q, state.astype(BF16)) * qd_ref[...]
        qs = jnp.concatenate([jnp.where(((lane & 127) >> 5) == h, q, jnp.zeros_like(q)) for h in range(RET_HEADS)],
                             axis=0)
        pv = _dot((_dot_nt(qs, k) * dec_ref[...]).astype(BF16), v)
        for h in range(RET_HEADS):
            o = o + jnp.where((lane >> 6) == h, pv[h * c:(h + 1) * c], 0.0)
        upd = _dot_tn((kf * kd_ref[...]).astype(BF16), v)
        s_ref[0] = sd_ref[...] * state + jnp.where(bd_mask, upd, 0.0)
        dlt = o - _group_mean(o, avg)
        ro = dlt * lax.rsqrt(_group_mean(dlt * dlt, avg) + EPS) * gn_ref[...]
        o_ref[0, rs, :] = (_silu(g_ref[0, rs, :]) * ro).astype(o_ref.dtype)
        return carry

    _fori_pairs(n, step, 0)


def _ret_consts(c):
    lg = jnp.log(1.0 - 2.0 ** (-5.0 - jnp.arange(RET_HEADS, dtype=F32)))
    i = jnp.arange(c, dtype=F32)
    rel = i[:, None] - i[None, :]
    dec = jnp.where(rel >= 0, jnp.exp(lg[:, None, None] * jnp.maximum(rel, 0.0)), 0.0)
    v_head = jnp.arange(HEAD_W) // RET_DV
    k_head = (jnp.arange(HEAD_W) % 128) // (RET_DK // 2)
    qd = jnp.exp(lg[None, v_head] * (i[:, None] + 1.0))
    kd = jnp.exp(lg[None, k_head] * (c - 1.0 - i[:, None]))
    sd = jnp.exp(lg * c)[v_head][None, :]
    return dec, qd, kd, sd


def _ret(rq, rk, rv, rg, s0, gn):
    b, l, _ = rq.shape
    c = min(l, RET_CHUNK)
    dec, qd, kd, sd = _ret_consts(c)
    dec = dec.reshape(RET_HEADS * c, c)
    blk = pl.BlockSpec((1, l, HEAD_W), lambda i: (i, 0, 0))
    sblk = pl.BlockSpec((1, HEAD_W, HEAD_W), lambda i: (i, 0, 0))
    return pl.pallas_call(
        functools.partial(_ret_kernel, c=c),
        out_shape=[jax.ShapeDtypeStruct((b, l, HEAD_W), BF16), jax.ShapeDtypeStruct((b, HEAD_W, HEAD_W), F32)],
        grid=(b,),
        in_specs=[blk, blk, blk, blk, sblk, _const_spec(gn.shape), _const_spec(dec.shape), _const_spec(qd.shape),
                  _const_spec(kd.shape), _const_spec(sd.shape)],
        out_specs=[blk, sblk],
        compiler_params=_params("parallel"),
        name="retention",
    )(rq, rk, rv, rg, s0, gn, dec, qd, kd, sd)


def _ffn_kernel(mla_ref, ret_ref, sb_ref, x_ref, mod_ref, prev_ref, wo_ref, g2_ref, wup_ref, cw_ref, cb_ref,
                wdn_ref, gf_ref, o_ref, st_ref, a_scr, carry, *, final_norm):
    bb, tl, d = x_ref.shape
    rows = bb * tl
    t = pl.program_id(1)
    nt = pl.num_programs(1)

    @pl.when(t == 0)
    def _():
        carry[:, 6:8, :] = prev_ref[...]

    mod = mod_ref[...]
    cat = jnp.concatenate([mla_ref[...].reshape(rows, 512), ret_ref[...].reshape(rows, HEAD_W),
                           sb_ref[...].reshape(rows, HEAD_W)], axis=-1)
    x1 = x_ref[...] + mod[:, 2:3, :] * _dot(cat, wo_ref[...]).reshape(bb, tl, d)
    h = (_rms(x1) * g2_ref[...] * (1.0 + mod[:, 4:5, :]) + mod[:, 3:4, :]).reshape(rows, d).astype(BF16)
    f = jnp.zeros((rows, d), F32)
    for c0, c1 in FF_CHUNKS:
        cs, n = slice(c0, c1), c1 - c0
        a = _dot(h, wup_ref[:, cs]).reshape(bb, tl, n)
        b = _dot(h, wup_ref[:, D_FF + c0:D_FF + c1])
        a_scr[:, 8:, :n] = a
        a_scr[:, 6:8, :n] = carry[:, 6:8, cs]
        cw = cw_ref[...]
        conv = (cb_ref[:, cs] + cw[0:1, cs] * a_scr[:, 6:6 + tl, :n] + cw[1:2, cs] * a_scr[:, 7:7 + tl, :n]
                + cw[2:3, cs] * a)
        carry[:, 6:8, cs] = a_scr[:, tl + 6:tl + 8, :n]
        y = (_silu(conv).reshape(rows, n) * b).astype(BF16)
        f = f + _dot(y, wdn_ref[cs, :])
    x2 = x1 + mod[:, 5:6, :] * f.reshape(bb, tl, d)
    if final_norm:
        x2 = _rms(x2) * gf_ref[...]
    o_ref[...] = x2

    @pl.when(t == nt - 1)
    def _():
        st_ref[...] = carry[:, 6:8, :]


def _ffn(mla, ret, sb, x, mod, prev, wo, g2, wup, cw, cb, wdn, gf, final_norm):
    b, l, d = x.shape
    tl = min(l, ROW_TILE)
    bb = ROW_TILE // tl
    tok = lambda w: pl.BlockSpec((bb, tl, w), lambda i, t: (i, t, 0))
    st = pl.BlockSpec((bb, CONV_W - 1, D_FF), lambda i, t: (i, 0, 0))
    once = lambda a: pl.BlockSpec(a.shape, lambda i, t: (0, 0), pipeline_mode=pl.Buffered(1))
    return pl.pallas_call(
        functools.partial(_ffn_kernel, final_norm=final_norm),
        out_shape=[jax.ShapeDtypeStruct((b, l, d), F32), jax.ShapeDtypeStruct((b, CONV_W - 1, D_FF), F32)],
        grid=(b // bb, l // tl),
        in_specs=[tok(512), tok(HEAD_W), tok(HEAD_W), tok(d), pl.BlockSpec((bb, 6, d), lambda i, t: (i, 0, 0)), st,
                  once(wo), _const_spec(g2.shape), once(wup), _const_spec(cw.shape), _const_spec(cb.shape),
                  once(wdn), _const_spec(gf.shape)],
        out_specs=[tok(d), st],
        scratch_shapes=[pltpu.VMEM((bb, tl + 8, max(c1 - c0 for c0, c1 in FF_CHUNKS)), F32),
                        pltpu.VMEM((bb, 8, D_FF), F32)],
        compiler_params=_params("parallel", "arbitrary"),
        name="conv_ffn",
    )(mla, ret, sb, x, mod, prev, wo, g2, wup, cw, cb, wdn, gf)


def _half_split(w, heads):
    lead = w.shape[:-1]
    dim = w.shape[-1] // heads
    w = w.reshape(lead + (heads, 2, dim // 2))
    return jnp.swapaxes(w, -3, -2).reshape(lead + (heads * dim,))


def _permute_win(w):
    half = MLA_ROPE // 2
    kr = w[..., 640:672]
    kr_swapped = jnp.concatenate([kr[..., half:], kr[..., :half]], axis=-1)
    tiled = lambda a: jnp.tile(a, (1, 1, 4))
    return jnp.concatenate([w[..., 0:640], _half_split(w[..., 672:928], RET_HEADS),
                            _half_split(w[..., 928:1184], RET_HEADS), w[..., 1184:2464],
                            tiled(kr), tiled(kr_swapped)], axis=-1)


def _permute_wuq(w):
    half = MLA_ROPE // 2
    w = w.reshape(w.shape[:-1] + (MLA_HEADS, MLA_NOPE + MLA_ROPE))
    flat = lambda a: a.reshape(a.shape[:-2] + (-1,))
    rope = w[..., MLA_NOPE:]
    swapped = jnp.concatenate([rope[..., half:], rope[..., :half]], axis=-1)
    return jnp.concatenate([flat(w[..., :MLA_NOPE]), flat(rope), flat(swapped)], axis=-1)


def _permute_wukv(w):
    w = w.reshape(w.shape[:-1] + (MLA_HEADS, MLA_NOPE + MLA_V))
    flat = lambda a: a.reshape(a.shape[:-2] + (-1,))
    return jnp.concatenate([flat(w[..., :MLA_NOPE]), flat(w[..., MLA_NOPE:])], axis=-1)


def _ret_state_in(st):
    b = st.shape[0]
    st = st.reshape(b, RET_HEADS, 2, RET_DK // 2, RET_DV).transpose(0, 2, 1, 3, 4)
    eye = jnp.eye(RET_HEADS, dtype=st.dtype)
    full = st[:, :, :, :, None, :] * eye[None, None, :, None, :, None]
    return full.reshape(b, HEAD_W, HEAD_W)


def _ret_state_out(s):
    b = s.shape[0]
    s = s.reshape(b, 2, RET_HEADS, RET_DK // 2, RET_HEADS, RET_DV)
    blocks = [s[:, :, h, :, h, :].reshape(b, RET_DK, RET_DV) for h in range(RET_HEADS)]
    return jnp.stack(blocks, axis=1)


def _rope_angles(pos, dim):
    inv = ROPE_BASE ** (-jnp.arange(0, dim, 2, dtype=F32) / dim)
    return pos.astype(F32)[:, None] * inv[None, :]


def _rope_tables(pos, dim, reps):
    ang = _rope_angles(pos, dim)
    return jnp.tile(jnp.cos(ang), (1, reps)), jnp.tile(jnp.sin(ang), (1, reps))


def _rope_tables_swapped(pos, dim, reps):
    ang = _rope_angles(pos, dim)
    c, s = jnp.cos(ang), jnp.sin(ang)
    return jnp.tile(jnp.concatenate([c, c], axis=1), (1, reps)), jnp.tile(jnp.concatenate([-s, s], axis=1), (1, reps))


def _trunk(x, mods, pos0, cache, w):
    b, l, _ = x.shape
    depth = len(mods)
    pos = pos0 + jnp.arange(l)
    cm, sm = _rope_tables_swapped(pos, MLA_ROPE, MLA_HEADS)
    cr, sr = _rope_tables(pos, RET_DK, RET_HEADS)
    if cache is None:
        off = 0
    else:
        off = cache["lat"].shape[2]
        krt_cache = jnp.tile(cache["kr"], (1, 1, 1, 4)).astype(BF16)
        sb_cache = (cache["sk"].reshape(depth, b, off, HEAD_W), cache["sv"].reshape(depth, b, off, HEAD_W))
    earlier = []
    small = [[], []]
    for layer in range(depth):
        mod = mods[layer]
        last = layer == depth - 1
        p = _inproj(x, mod, w["g_norm1"][layer], w["w_in"][layer], w["g_q_norm"][layer], w["w_uq"][layer],
                    w["g_kv_norm"][layer], w["w_uk"][layer], w["w_vt"][layer], cm, sm, cr, sr,
                    earlier if last else None)
        if last:
            sb_new = (p["sk"], p["sv"], layer)
        else:
            earlier.append((p["lat"], p["kr"], p["sk"], p["sv"]))
            sb_new = (p["sk"][None], p["sv"][None], 0)
        if cache is None:
            mla_past, sb_past = (p["kp"], p["vt"]), sb_new
            s0 = jnp.zeros((b, HEAD_W, HEAD_W), F32)
            prev = jnp.zeros((b, CONV_W - 1, D_FF), F32)
        else:
            mla_past = _kvup(cache["lat"], layer, krt_cache[layer], w["w_uk"][layer], w["w_vt"][layer])
            sb_past = sb_cache + (layer,)
            s0 = _ret_state_in(cache["S"][layer])
            prev = cache["conv"][layer]
        mla = _mla(p["qn"], p["qr"], mla_past, (p["kp"], p["vt"]), off)
        sb = _sb(p["sq"], sb_past, sb_new, off)
        ret, s_new = _ret(p["rq"], p["rk"], p["rv"], p["rg"], s0, w["g_ret_norm"][layer])
        x, conv_state = _ffn(mla, ret, sb, x, mod, prev, w["w_o"][layer], w["g_norm2"][layer], w["w_up"][layer],
                             w["conv_w"][layer], w["conv_b"][layer], w["w_down"][layer], w["g_final"],
                             layer == depth - 1)
        small[0].append(_ret_state_out(s_new))
        small[1].append(conv_state)
    lat, kr, sk, sv = p["lat"], p["kr"], p["sk"], p["sv"]
    return x, [lat, kr, sk.reshape(depth, b, l, SB_HEADS, SB_DK), sv.reshape(depth, b, l, SB_HEADS, SB_DV),
               jnp.stack(small[0], axis=0), jnp.stack(small[1], axis=0)]


def kernel(x_prompt, x_sample, c_prompt, c_sample, cache_mla_latent, cache_mla_krope, cache_sb_k, cache_sb_v, state_ret, state_ffn_conv, w_in, g_q_norm, w_uq, g_kv_norm, w_ukv, g_ret_norm, w_o, w_up, conv_w, conv_b, w_down, g_norm1, g_norm2, w_ada, b_ada, g_final):
    depth = w_in.shape[0]
    bp = x_prompt.shape[0]
    row = lambda g: g.reshape(g.shape[0], 1, g.shape[-1])
    w = dict(
        w_in=_permute_win(w_in).astype(BF16),
        w_uq=_permute_wuq(w_uq).astype(BF16),
        w_uk=_permute_wukv(w_ukv)[..., :MLA_HEADS * MLA_NOPE].astype(BF16),
        w_vt=jnp.swapaxes(_permute_wukv(w_ukv)[..., MLA_HEADS * MLA_NOPE:], -1, -2).astype(BF16),
        w_o=w_o.astype(BF16), w_up=w_up.astype(BF16), w_down=w_down.astype(BF16),
        g_q_norm=row(g_q_norm), g_kv_norm=row(g_kv_norm), g_ret_norm=row(g_ret_norm),
        g_norm1=row(g_norm1), g_norm2=row(g_norm2), conv_w=conv_w, conv_b=row(conv_b),
        g_final=g_final.reshape(1, -1))
    mod = _ada(jnp.concatenate([c_prompt, c_sample], axis=0), w_ada, b_ada)
    mod = mod.reshape(depth, mod.shape[1], 6, D_MODEL)
    y_p, st_p = _trunk(x_prompt, [mod[l, :bp] for l in range(depth)], 0, None, w)
    cache = dict(lat=cache_mla_latent, kr=cache_mla_krope, sk=cache_sb_k, sv=cache_sb_v, S=state_ret,
                 conv=state_ffn_conv)
    y_s, st_s = _trunk(x_sample, [mod[l, bp:] for l in range(depth)], cache_mla_latent.shape[2], cache, w)
    return (y_p, y_s, *st_p, *st_s)
```

```python
import functools

import jax
import jax.numpy as jnp
from jax import lax
from jax.experimental import pallas as pl
from jax.experimental.pallas import tpu as pltpu

D_MODEL = 1024
CHUNK = 64
MLA_HEADS = 8
MLA_NOPE = 64
MLA_ROPE = 32
MLA_V = 64
MLA_Q_RANK = 384
MLA_KV_RANK = 256
RET_HEADS = 4
RET_DK = 64
RET_DV = 64
SB_HEADS = 4
SB_DK = 64
SB_DV = 64
D_FF = 2816
CONV_W = 3
ROPE_BASE = 10000.0
EPS = 1e-6

LANES = 128
HEAD_W = 256
MLA_NQ = MLA_HEADS * MLA_NOPE
MLA_NV = MLA_HEADS * MLA_V
MLA_KP_W = (MLA_HEADS // 2) * HEAD_W
ROW_TILE = 512
ATT_TILE = 256
RET_CHUNK = 256
FF_CHUNKS = ((0, 1536), (1536, D_FF))
MLA_EXP2_SCALE = (MLA_NOPE + MLA_ROPE) ** -0.5 * 1.4426950408889634
SB_LOG_ZERO = -104.0
VMEM_LIMIT = 56 * 1024 * 1024

F32 = jnp.float32
BF16 = jnp.bfloat16

_ZQ, _ZKV, _RQ, _RK, _RV, _RG, _SQ, _SK, _SV, _KRT, _IN_COLS_P = (
    0, 384, 640, 896, 1152, 1408, 1664, 1920, 2176, 2432, 2688)


def _dot(a, b):
    return jnp.dot(a, b, preferred_element_type=F32)


def _dot_nt(a, b):
    return lax.dot_general(a, b, (((1,), (1,)), ((), ())), preferred_element_type=F32)


def _dot_tn(a, b):
    return lax.dot_general(a, b, (((0,), (0,)), ((), ())), preferred_element_type=F32)


def _rms(x):
    return x * lax.rsqrt(jnp.mean(x * x, axis=-1, keepdims=True) + EPS)


def _silu(x):
    return x / (1.0 + jnp.exp(-x))


def _params(*sem):
    return pltpu.CompilerParams(dimension_semantics=sem, vmem_limit_bytes=VMEM_LIMIT)


def _const_spec(shape):
    nd = len(shape)
    return pl.BlockSpec(shape, lambda *_: (0,) * nd)


def _ada_kernel(c_ref, w_ref, b_ref, o_ref):
    a = _silu(c_ref[...]).astype(BF16)
    o_ref[0] = _dot(a, w_ref[0].astype(BF16)) + b_ref[0]


def _ada(c_all, w_ada, b_ada):
    depth, d, n = w_ada.shape
    rows = c_all.shape[0]
    tn = 1536
    return pl.pallas_call(
        _ada_kernel,
        out_shape=jax.ShapeDtypeStruct((depth, rows, n), F32),
        grid=(depth, n // tn),
        in_specs=[pl.BlockSpec((rows, d), lambda l, j: (0, 0)),
                  pl.BlockSpec((1, d, tn), lambda l, j: (l, 0, j)),
                  pl.BlockSpec((1, 1, tn), lambda l, j: (l, 0, j))],
        out_specs=pl.BlockSpec((1, rows, tn), lambda l, j: (l, 0, j)),
        compiler_params=_params("parallel", "parallel"),
        name="ada_mod",
    )(c_all, w_ada, b_ada.reshape(depth, 1, n))


_INPROJ_INPUTS = 13
_INPROJ_OUTPUTS = (("qn", MLA_NQ, BF16, False), ("qr", MLA_HEADS * MLA_ROPE, BF16, False), ("lat", MLA_KV_RANK, F32, True),
                   ("kp", MLA_KP_W, BF16, False), ("vt", None, BF16, False), ("kr", MLA_ROPE, F32, True),
                   ("rq", HEAD_W, BF16, False), ("rk", HEAD_W, F32, False), ("rv", HEAD_W, BF16, False),
                   ("rg", HEAD_W, F32, False), ("sq", HEAD_W, BF16, False), ("sk", HEAD_W, F32, True),
                   ("sv", HEAD_W, F32, True))


def _inproj_kernel(*refs):
    (x_ref, mod_ref, g1_ref, win_ref, gq_ref, wuq_ref, gkv_ref, wuk_ref, wvt_ref, cm_ref, sm_ref, cr_ref,
     sr_ref) = refs[:_INPROJ_INPUTS]
    names = [o[0] for o in _INPROJ_OUTPUTS]
    out = dict(zip(names, refs[len(refs) - len(names):]))
    stacked = [o[0] for o in _INPROJ_OUTPUTS if o[3]]
    earlier = refs[_INPROJ_INPUTS:len(refs) - len(names)]
    layer = len(earlier) // len(stacked)
    for r in range(layer):
        for k, name in enumerate(stacked):
            out[name][r] = earlier[r * len(stacked) + k][...]
    bb, tl, d = x_ref.shape
    rows = bb * tl
    mod = mod_ref[...]
    h = _rms(x_ref[...]) * g1_ref[...] * (1.0 + mod[:, 1:2, :]) + mod[:, 0:1, :]
    z = _dot(h.reshape(rows, d).astype(BF16), win_ref[...])

    def put(name, val, lanes=slice(None)):
        ref = out[name]
        lead = (layer,) if len(ref.shape) == 4 else ()
        ref[lead + (slice(None), slice(None), lanes)] = val.reshape(bb, tl, val.shape[-1]).astype(ref.dtype)

    def rope(x1, x2, c_ref, s_ref):
        c, s = c_ref[...][None], s_ref[...][None]
        x1 = x1.reshape(bb, tl, x1.shape[-1])
        x2 = x2.reshape(bb, tl, x2.shape[-1])
        return x1 * c - x2 * s, x1 * s + x2 * c

    def rope_swapped(a, b, width):
        a = a.reshape(bb, tl, width)
        b = b.reshape(bb, tl, width)
        return a * cm_ref[:, :width][None] + b * sm_ref[:, :width][None]

    q = _dot((_rms(z[:, _ZQ:_ZKV]) * gq_ref[...]).astype(BF16), wuq_ref[...])
    nr = MLA_HEADS * MLA_ROPE
    put("qn", q[:, :MLA_NQ] * MLA_EXP2_SCALE)
    put("qr", rope_swapped(q[:, MLA_NQ:MLA_NQ + nr], q[:, MLA_NQ + nr:], nr) * MLA_EXP2_SCALE)
    lat = _rms(z[:, _ZKV:_RQ]) * gkv_ref[...]
    put("lat", lat)
    kr = rope_swapped(z[:, _KRT:_KRT + LANES], z[:, _KRT + LANES:_KRT + 2 * LANES], LANES)
    put("kr", kr[:, :, :MLA_ROPE])
    lat_b = lat.astype(BF16)
    kn = _dot(lat_b, wuk_ref[...])
    for p in range(MLA_HEADS // 2):
        put("kp", kn[:, LANES * p:LANES * (p + 1)], slice(HEAD_W * p, HEAD_W * p + LANES))
        put("kp", kr, slice(HEAD_W * p + LANES, HEAD_W * (p + 1)))
    vt_ref = out["vt"]
    tkk = vt_ref.shape[-1]
    for bi in range(bb):
        for kk in range(tl // tkk):
            r0 = bi * tl + kk * tkk
            vt_ref[bi, kk] = _dot_nt(wvt_ref[...], lat_b[r0:r0 + tkk]).astype(vt_ref.dtype)
    a1, a2 = rope(z[:, _RQ:_RQ + LANES], z[:, _RQ + LANES:_RK], cr_ref, sr_ref)
    put("rq", a1 * (RET_DK ** -0.5), slice(0, LANES))
    put("rq", a2 * (RET_DK ** -0.5), slice(LANES, HEAD_W))
    b1, b2 = rope(z[:, _RK:_RK + LANES], z[:, _RK + LANES:_RV], cr_ref, sr_ref)
    put("rk", b1, slice(0, LANES))
    put("rk", b2, slice(LANES, HEAD_W))
    put("rv", z[:, _RV:_RG])
    put("rg", z[:, _RG:_SQ])
    put("sq", z[:, _SQ:_SK] * (SB_DK ** -0.5))
    put("sk", z[:, _SK:_SV])
    put("sv", z[:, _SV:_KRT])


def _inproj(x, mod, g1, win, gq, wuq, gkv, wuk, wvt, cm, sm, cr, sr, earlier):
    b, l, d = x.shape
    tl = min(l, ROW_TILE)
    bb = ROW_TILE // tl
    grid = (b // bb, l // tl)
    depth = None if earlier is None else len(earlier) + 1
    tok = lambda w: pl.BlockSpec((bb, tl, w), lambda i, t: (i, t, 0))
    stk = lambda w: pl.BlockSpec((depth, bb, tl, w), lambda i, t: (0, i, t, 0))
    pos = pl.BlockSpec((tl, LANES), lambda i, t: (t, 0))
    pos2 = pl.BlockSpec((tl, 2 * LANES), lambda i, t: (t, 0))
    stacked = lambda st: st and depth is not None
    extra = [] if earlier is None else [a for row in earlier for a in row]
    tkk = min(l, ATT_TILE)
    vt_shape = jax.ShapeDtypeStruct((b, l // tkk, MLA_HEADS * MLA_V, tkk), BF16)
    vt_spec = pl.BlockSpec((bb, tl // tkk, MLA_HEADS * MLA_V, tkk), lambda i, t: (i, t, 0, 0))
    res = pl.pallas_call(
        _inproj_kernel,
        out_shape=[vt_shape if w is None else jax.ShapeDtypeStruct((depth, b, l, w) if stacked(st) else (b, l, w), dt)
                   for _, w, dt, st in _INPROJ_OUTPUTS],
        grid=grid,
        in_specs=[tok(d), pl.BlockSpec((bb, 6, d), lambda i, t: (i, 0, 0)), _const_spec(g1.shape),
                  _const_spec(win.shape), _const_spec(gq.shape), _const_spec(wuq.shape), _const_spec(gkv.shape),
                  _const_spec(wuk.shape), _const_spec(wvt.shape), pos2, pos2, pos, pos]
        + [tok(a.shape[-1]) for a in extra],
        out_specs=[vt_spec if w is None else (stk(w) if stacked(st) else tok(w)) for _, w, _, st in _INPROJ_OUTPUTS],
        compiler_params=_params("parallel", "parallel"),
        name="in_proj",
    )(x, mod, g1, win, gq, wuq, gkv, wuk, wvt, cm, sm, cr, sr, *extra)
    return dict(zip([o[0] for o in _INPROJ_OUTPUTS], res))


def _kvup_kernel(lat_ref, krt_ref, wuk_ref, wvt_ref, kp_ref, vt_ref):
    lat_b = lat_ref[0].astype(BF16)
    kn = _dot(lat_b, wuk_ref[...])
    for p in range(MLA_HEADS // 2):
        kp_ref[0, :, HEAD_W * p:HEAD_W * p + LANES] = kn[:, LANES * p:LANES * (p + 1)].astype(kp_ref.dtype)
        kp_ref[0, :, HEAD_W * p + LANES:HEAD_W * (p + 1)] = krt_ref[0]
    tkk = vt_ref.shape[-1]
    for kk in range(vt_ref.shape[1]):
        vt_ref[0, kk] = _dot_nt(wvt_ref[...], lat_b[kk * tkk:(kk + 1) * tkk]).astype(vt_ref.dtype)


def _kvup(lat, layer, krt, wuk, wvt):
    _, b, lk, r = lat.shape
    nv = MLA_HEADS * MLA_V
    blk = lambda w: pl.BlockSpec((1, lk, w), lambda i: (i, 0, 0))
    return pl.pallas_call(
        _kvup_kernel,
        out_shape=(jax.ShapeDtypeStruct((b, lk, MLA_KP_W), BF16),
                   jax.ShapeDtypeStruct((b, lk // ATT_TILE, nv, ATT_TILE), BF16)),
        grid=(b,),
        in_specs=[pl.BlockSpec((None, 1, lk, r), lambda i: (layer, i, 0, 0)), blk(LANES), _const_spec(wuk.shape),
                  _const_spec(wvt.shape)],
        out_specs=[blk(MLA_KP_W), pl.BlockSpec((1, lk // ATT_TILE, nv, ATT_TILE), lambda i: (i, 0, 0, 0))],
        compiler_params=_params("parallel"),
        name="kv_up",
    )(lat, krt, wuk, wvt)


def _fori_pairs(n, body, init):
    carry = lax.fori_loop(0, n // 2, lambda i, c: body(2 * i + 1, body(2 * i, c)), init)
    return lax.fori_loop(0, n % 2, lambda i, c: body(n - 1, c), carry)


def _fold8(reduce, x):
    return reduce(x.reshape(x.shape[0] // 8, 8, x.shape[1]), axis=0)


def _mla_kernel(qn_ref, qr_ref, kp_ref, vtp_ref, kd_ref, vtd_ref, o_ref, s_scr, q_scr, mx_scr, m_scr, l_scr, acc_scr,
                *, tq, tk, off, fused, hp):
    qi = pl.program_id(1)
    nfull = (off + qi * tq) // tk
    nslot = 4 // hp
    wl = hp * tq
    lane = lax.broadcasted_iota(jnp.int32, (1, LANES), 1)
    key_c = lax.broadcasted_iota(jnp.int32, (tq, 1), 0) // CHUNK
    qry_c = (lax.broadcasted_iota(jnp.int32, (1, wl), 1) % tq) // CHUNK
    dmask = key_c <= qry_c
    neg = -1e30
    lane_v = lax.broadcasted_iota(jnp.int32, (1, HEAD_W), 1)
    feat = lax.broadcasted_iota(jnp.int32, (MLA_V, 1), 0)

    def build_queries(g):
        qr = qr_ref[0, :, LANES * g:LANES * (g + 1)]
        for hh in range(4):
            qn = qn_ref[0, :, LANES * (2 * g + hh // 2):LANES * (2 * g + hh // 2 + 1)]
            rows = slice((hh % hp) * tq, (hh % hp + 1) * tq)
            q_scr[g, hh // hp, rows, 0:LANES] = jnp.where((lane >> 6) == hh % 2, qn, jnp.zeros_like(qn))
            q_scr[g, hh // hp, rows, LANES:HEAD_W] = jnp.where((lane >> 5) == hh, qr, jnp.zeros_like(qr))

    def slot_scores(g, keys_of_pair, u):
        return _dot_nt(keys_of_pair(2 * g + (u * hp) // 2), q_scr[g, u])

    def vt_rows(g, u):
        return slice(MLA_V * (4 * g + u * hp), MLA_V * (4 * g + (u + 1) * hp))

    def scores_tile(g, j):
        ks = pl.ds(pl.multiple_of(j * tk, tk), tk)
        for u in range(nslot):
            s = slot_scores(g, lambda p: kp_ref[0, ks, HEAD_W * p:HEAD_W * (p + 1)], u)
            s_scr[g, j, u] = s
            mx_scr[g, u] = jnp.maximum(mx_scr[g, u], _fold8(jnp.max, s))

    def scores_own(g):
        own = []
        for u in range(nslot):
            if fused:
                ks_d = pl.ds(pl.multiple_of(nfull * tk, tk), tk)
                s_d = jnp.where(dmask, slot_scores(g, lambda p: kp_ref[0, ks_d, HEAD_W * p:HEAD_W * (p + 1)], u), neg)
                s_scr[g, nfull, u] = s_d
            else:
                s_d = jnp.where(dmask, slot_scores(g, lambda p: kd_ref[0, :, HEAD_W * p:HEAD_W * (p + 1)], u), neg)
                own.append(s_d)
            mx_scr[g, u] = _fold8(jnp.max, s_d)
        return own

    def values_tile(g, j):
        for u in range(nslot):
            p = jnp.exp2(s_scr[g, j, u] - m_scr[g, u, 0:1, :])
            l_scr[g, u] += _fold8(jnp.sum, p)
            acc_scr[g, u] += _dot(vtp_ref[0, j, vt_rows(g, u), :], p.astype(BF16))

    def values_start(g, own):
        for u in range(nslot):
            m_scr[g, u] = jnp.broadcast_to(jnp.max(mx_scr[g, u], axis=0, keepdims=True), (8, wl))
            if fused:
                l_scr[g, u] = jnp.zeros((8, wl), F32)
                acc_scr[g, u] = jnp.zeros((hp * MLA_V, wl), F32)
            else:
                p = jnp.exp2(own[u] - m_scr[g, u, 0:1, :])
                l_scr[g, u] = _fold8(jnp.sum, p)
                acc_scr[g, u] = _dot(vtd_ref[0, 0, vt_rows(g, u), :], p.astype(BF16))

    def emit(g):
        out_g = jnp.zeros((tq, HEAD_W), F32)
        for hh in range(4):
            u, a = hh // hp, hh % hp
            inv = 1.0 / jnp.sum(l_scr[g, u], axis=0, keepdims=True)
            place = jnp.where(feat + MLA_V * hh == lane_v, 1.0, 0.0).astype(BF16)
            placed = _dot_tn((acc_scr[g, u, MLA_V * a:MLA_V * (a + 1), :] * inv).astype(BF16), place)
            out_g = out_g + placed[a * tq:(a + 1) * tq]
        o_ref[0, :, HEAD_W * g:HEAD_W * (g + 1)] = out_g.astype(o_ref.dtype)

    def loop(n, *works):
        def body(j, carry):
            for w in works:
                w(j)
            return carry
        _fori_pairs(n, body, 0)

    build_queries(0)
    build_queries(1)
    own0 = scores_own(0)
    loop(nfull, lambda j: scores_tile(0, j))
    values_start(0, own0)
    own1 = scores_own(1)
    loop(nfull, lambda j: scores_tile(1, j), lambda j: values_tile(0, j))
    if fused:
        values_tile(0, nfull)
    values_start(1, own1)
    emit(0)
    loop(nfull + 1 if fused else nfull, lambda j: values_tile(1, j))
    emit(1)


def _mla(qn, qr, past, diag, off):
    b, l, _ = qn.shape
    lp = past[0].shape[1]
    tq = min(l, ATT_TILE)
    tk = ATT_TILE
    nv = MLA_NV
    fused = off == 0 and tq == tk
    qblk = lambda w: pl.BlockSpec((1, tq, w), lambda i, t: (i, t, 0))
    nmax = (off + l - tq) // tk + (1 if fused else 0)
    hp = 1 if tq % 128 == 0 else 2
    wl = hp * tq
    kern = functools.partial(_mla_kernel, tq=tq, tk=tk, off=off, fused=fused, hp=hp)
    return pl.pallas_call(
        kern,
        out_shape=jax.ShapeDtypeStruct((b, l, nv), BF16),
        grid=(b, l // tq),
        in_specs=[qblk(MLA_NQ), qblk(MLA_HEADS * MLA_ROPE), pl.BlockSpec((1, lp, MLA_KP_W), lambda i, t: (i, 0, 0)),
                  pl.BlockSpec((1, lp // tk, nv, tk), lambda i, t: (i, 0, 0, 0)), qblk(MLA_KP_W),
                  pl.BlockSpec((1, 1, nv, tq), lambda i, t: (i, t, 0, 0))],
        out_specs=qblk(nv),
        scratch_shapes=[pltpu.VMEM((2, nmax, 4 // hp, tk, wl), F32), pltpu.VMEM((2, 4 // hp, wl, HEAD_W), BF16),
                        pltpu.VMEM((2, 4 // hp, 8, wl), F32), pltpu.VMEM((2, 4 // hp, 8, wl), F32),
                        pltpu.VMEM((2, 4 // hp, 8, wl), F32), pltpu.VMEM((2, 4 // hp, hp * MLA_V, wl), F32)],
        compiler_params=_params("parallel", "parallel"),
        name="mla_attn",
    )(qn, qr, *past, *diag)


def _neg_suffix_matrix(n):
    j = lax.broadcasted_iota(jnp.int32, (n, n), 0)
    s = lax.broadcasted_iota(jnp.int32, (n, n), 1)
    return jnp.where(j >= s, -1.0, 0.0).astype(BF16)


def _sb_kernel(q_ref, kp_ref, vp_ref, kd_ref, vd_ref, o_ref, kb_ref, vb_ref, q_scr, run_scr, acc_scr, *, tq, tk, off):
    qi = pl.program_id(1)

    @pl.when(qi == 0)
    def _():
        kb_ref[...] = kp_ref[0].astype(BF16)
        vb_ref[...] = vp_ref[0].astype(BF16)

    nfull = (off + qi * tq) // tk
    lane = lax.broadcasted_iota(jnp.int32, (1, HEAD_W), 1)
    nh = SB_HEADS
    dmask = (lax.broadcasted_iota(jnp.int32, (1, tq), 1)
             < lax.broadcasted_iota(jnp.int32, (nh * tq, 1), 0) % tq)
    t_diag = _neg_suffix_matrix(tq)
    t_full = t_diag if tk == tq else _neg_suffix_matrix(tk)
    q = q_ref[0]

    def exponent(k, mask, tmat):
        zz = _dot_nt(q_scr[...], k)
        sp = jnp.maximum(zz, 0.0) + jnp.log(1.0 + jnp.exp(-jnp.abs(zz)))
        if mask is not None:
            sp = jnp.where(mask, sp, 0.0)
        hi = sp.astype(BF16)
        lo = (sp - hi.astype(F32)).astype(BF16)
        return zz + _dot(hi, tmat) + _dot(lo, tmat), jnp.sum(sp, axis=-1, keepdims=True)

    def carry_on(run):
        run_scr[...] = jnp.broadcast_to(run, (nh * tq, LANES))
        return jnp.max(run)

    def own_weights():
        e, rs = exponent(kd_ref[0].astype(BF16), dmask, t_diag)
        return jnp.where(dmask, jnp.exp(e), 0.0).astype(BF16), rs

    def own_only():
        w, rs = own_weights()
        acc_scr[...] = _dot(w, vd_ref[0].astype(BF16))
        return carry_on(-rs)

    def own_and_previous():
        ks = pl.ds(pl.multiple_of((nfull - 1) * tk, tk), tk)
        w, rs = own_weights()
        e1, rs1 = exponent(kb_ref[ks, :], None, t_full)
        w1 = jnp.exp(e1 - rs).astype(BF16)
        acc_scr[...] = _dot(w, vd_ref[0].astype(BF16)) + _dot(w1, vb_ref[ks, :])
        return carry_on(-(rs + rs1))

    for h in range(nh):
        q_scr[h * tq:(h + 1) * tq, :] = jnp.where((lane >> 6) == h, q, jnp.zeros_like(q))
    top = lax.cond(nfull >= 1, own_and_previous, own_only)

    def more(state):
        j, top = state
        return jnp.logical_and(j >= 0, top > SB_LOG_ZERO)

    def body(state):
        j, _ = state
        ks = pl.ds(pl.multiple_of(j * tk, tk), tk)
        e, rs = exponent(kb_ref[ks, :], None, t_full)
        run = run_scr[...]
        w = jnp.exp(e + jnp.concatenate([run] * (tk // LANES), -1))
        acc_scr[...] += _dot(w.astype(BF16), vb_ref[ks, :])
        run = run - rs
        run_scr[...] = run
        return j - 1, jnp.max(run)

    lax.while_loop(more, body, (nfull - 2, top))
    out = jnp.zeros((tq, HEAD_W), F32)
    for h in range(nh):
        out = jnp.where((lane >> 6) == h, acc_scr[h * tq:(h + 1) * tq, :], out)
    o_ref[0] = out.astype(o_ref.dtype)


def _sb(sq, past, diag, off):
    b, l, _ = sq.shape
    lp = past[0].shape[2]
    tq = min(l, ATT_TILE)
    qblk = pl.BlockSpec((1, tq, HEAD_W), lambda i, t: (i, t, 0))
    dblk = pl.BlockSpec((None, 1, tq, HEAD_W), lambda i, t: (diag[2], i, t, 0))
    kblk = pl.BlockSpec((None, 1, lp, HEAD_W), lambda i, t: (past[2], i, 0, 0))
    kern = functools.partial(_sb_kernel, tq=tq, tk=ATT_TILE, off=off)
    return pl.pallas_call(
        kern,
        out_shape=jax.ShapeDtypeStruct((b, l, HEAD_W), BF16),
        grid=(b, l // tq),
        in_specs=[qblk, kblk, kblk, dblk, dblk],
        out_specs=qblk,
        scratch_shapes=[pltpu.VMEM((lp, HEAD_W), BF16), pltpu.VMEM((lp, HEAD_W), BF16),
                        pltpu.VMEM((SB_HEADS * tq, HEAD_W), BF16), pltpu.VMEM((SB_HEADS * tq, LANES), F32),
                        pltpu.VMEM((SB_HEADS * tq, HEAD_W), F32)],
        compiler_params=_params("parallel", "arbitrary"),
        name="sb_attn",
    )(sq, *past[:2], *diag[:2])


def _group_mean(x, avg):
    hi = x.astype(BF16)
    lo = (x - hi.astype(F32)).astype(BF16)
    return _dot(hi, avg) + _dot(lo, avg)


def _ret_kernel(q_ref, k_ref, v_ref, g_ref, s0_ref, gn_ref, dec_ref, qd_ref, kd_ref, sd_ref,
                o_ref, s_ref, *, c):
    n = q_ref.shape[1] // c
    lane = lax.broadcasted_iota(jnp.int32, (1, HEAD_W), 1)
    krow = lax.broadcasted_iota(jnp.int32, (HEAD_W, 1), 0)
    bd_mask = ((krow & 127) >> 5) == (lane >> 6)
    avg = jnp.where((krow >> 6) == (lane >> 6), 1.0 / RET_DV, 0.0).astype(BF16)
    s_ref[0] = s0_ref[0]

    def step(i, carry):
        rs = pl.ds(pl.multiple_of(i * c, c), c)
        q = q_ref[0, rs, :]
        kf = k_ref[0, rs, :]
        k = kf.astype(BF16)
        v = v_ref[0, rs, :]
        state = s_ref[0]
        o = _dot(q, state.astype(BF16)) * qd_ref[...]
        qs = jnp.concatenate([jnp.where(((lane & 127) >> 5) == h, q, jnp.zeros_like(q)) for h in range(RET_HEADS)],
                             axis=0)
        pv = _dot((_dot_nt(qs, k) * dec_ref[...]).astype(BF16), v)
        for h in range(RET_HEADS):
            o = o + jnp.where((lane >> 6) == h, pv[h * c:(h + 1) * c], 0.0)
        upd = _dot_tn((kf * kd_ref[...]).astype(BF16), v)
        s_ref[0] = sd_ref[...] * state + jnp.where(bd_mask, upd, 0.0)
        dlt = o - _group_mean(o, avg)
        ro = dlt * lax.rsqrt(_group_mean(dlt * dlt, avg) + EPS) * gn_ref[...]
        o_ref[0, rs, :] = (_silu(g_ref[0, rs, :]) * ro).astype(o_ref.dtype)
        return carry

    _fori_pairs(n, step, 0)


def _ret_consts(c):
    lg = jnp.log(1.0 - 2.0 ** (-5.0 - jnp.arange(RET_HEADS, dtype=F32)))
    i = jnp.arange(c, dtype=F32)
    rel = i[:, None] - i[None, :]
    dec = jnp.where(rel >= 0, jnp.exp(lg[:, None, None] * jnp.maximum(rel, 0.0)), 0.0)
    v_head = jnp.arange(HEAD_W) // RET_DV
    k_head = (jnp.arange(HEAD_W) % 128) // (RET_DK // 2)
    qd = jnp.exp(lg[None, v_head] * (i[:, None] + 1.0))
    kd = jnp.exp(lg[None, k_head] * (c - 1.0 - i[:, None]))
    sd = jnp.exp(lg * c)[v_head][None, :]
    return dec, qd, kd, sd


def _ret(rq, rk, rv, rg, s0, gn):
    b, l, _ = rq.shape
    c = min(l, RET_CHUNK)
    dec, qd, kd, sd = _ret_consts(c)
    dec = dec.reshape(RET_HEADS * c, c)
    blk = pl.BlockSpec((1, l, HEAD_W), lambda i: (i, 0, 0))
    sblk = pl.BlockSpec((1, HEAD_W, HEAD_W), lambda i: (i, 0, 0))
    return pl.pallas_call(
        functools.partial(_ret_kernel, c=c),
        out_shape=[jax.ShapeDtypeStruct((b, l, HEAD_W), BF16), jax.ShapeDtypeStruct((b, HEAD_W, HEAD_W), F32)],
        grid=(b,),
        in_specs=[blk, blk, blk, blk, sblk, _const_spec(gn.shape), _const_spec(dec.shape), _const_spec(qd.shape),
                  _const_spec(kd.shape), _const_spec(sd.shape)],
        out_specs=[blk, sblk],
        compiler_params=_params("parallel"),
        name="retention",
    )(rq, rk, rv, rg, s0, gn, dec, qd, kd, sd)


def _ffn_kernel(mla_ref, ret_ref, sb_ref, x_ref, mod_ref, prev_ref, wo_ref, g2_ref, wup_ref, cw_ref, cb_ref,
                wdn_ref, gf_ref, o_ref, st_ref, a_scr, carry, *, final_norm):
    bb, tl, d = x_ref.shape
    rows = bb * tl
    t = pl.program_id(1)
    nt = pl.num_programs(1)

    @pl.when(t == 0)
    def _():
        carry[:, 6:8, :] = prev_ref[...]

    mod = mod_ref[...]
    cat = jnp.concatenate([mla_ref[...].reshape(rows, MLA_NV), ret_ref[...].reshape(rows, HEAD_W),
                           sb_ref[...].reshape(rows, HEAD_W)], axis=-1)
    x1 = x_ref[...] + mod[:, 2:3, :] * _dot(cat, wo_ref[...]).reshape(bb, tl, d)
    h = (_rms(x1) * g2_ref[...] * (1.0 + mod[:, 4:5, :]) + mod[:, 3:4, :]).reshape(rows, d).astype(BF16)
    f = jnp.zeros((rows, d), F32)
    for c0, c1 in FF_CHUNKS:
        cs, n = slice(c0, c1), c1 - c0
        a = _dot(h, wup_ref[:, cs]).reshape(bb, tl, n)
        b = _dot(h, wup_ref[:, D_FF + c0:D_FF + c1])
        a_scr[:, 8:, :n] = a
        a_scr[:, 6:8, :n] = carry[:, 6:8, cs]
        cw = cw_ref[...]
        conv = (cb_ref[:, cs] + cw[0:1, cs] * a_scr[:, 6:6 + tl, :n] + cw[1:2, cs] * a_scr[:, 7:7 + tl, :n]
                + cw[2:3, cs] * a)
        carry[:, 6:8, cs] = a_scr[:, tl + 6:tl + 8, :n]
        y = (_silu(conv).reshape(rows, n) * b).astype(BF16)
        f = f + _dot(y, wdn_ref[cs, :])
    x2 = x1 + mod[:, 5:6, :] * f.reshape(bb, tl, d)
    if final_norm:
        x2 = _rms(x2) * gf_ref[...]
    o_ref[...] = x2

    @pl.when(t == nt - 1)
    def _():
        st_ref[...] = carry[:, 6:8, :]


def _ffn(mla, ret, sb, x, mod, prev, wo, g2, wup, cw, cb, wdn, gf, final_norm):
    b, l, d = x.shape
    tl = min(l, ROW_TILE)
    bb = ROW_TILE // tl
    tok = lambda w: pl.BlockSpec((bb, tl, w), lambda i, t: (i, t, 0))
    st = pl.BlockSpec((bb, CONV_W - 1, D_FF), lambda i, t: (i, 0, 0))
    once = lambda a: pl.BlockSpec(a.shape, lambda i, t: (0, 0), pipeline_mode=pl.Buffered(1))
    return pl.pallas_call(
        functools.partial(_ffn_kernel, final_norm=final_norm),
        out_shape=[jax.ShapeDtypeStruct((b, l, d), F32), jax.ShapeDtypeStruct((b, CONV_W - 1, D_FF), F32)],
        grid=(b // bb, l // tl),
        in_specs=[tok(MLA_NV), tok(HEAD_W), tok(HEAD_W), tok(d), pl.BlockSpec((bb, 6, d), lambda i, t: (i, 0, 0)), st,
                  once(wo), _const_spec(g2.shape), once(wup), _const_spec(cw.shape), _const_spec(cb.shape),
                  once(wdn), _const_spec(gf.shape)],
        out_specs=[tok(d), st],
        scratch_shapes=[pltpu.VMEM((bb, tl + 8, max(c1 - c0 for c0, c1 in FF_CHUNKS)), F32),
                        pltpu.VMEM((bb, 8, D_FF), F32)],
        compiler_params=_params("parallel", "arbitrary"),
        name="conv_ffn",
    )(mla, ret, sb, x, mod, prev, wo, g2, wup, cw, cb, wdn, gf)


def _half_split(w, heads):
    lead = w.shape[:-1]
    dim = w.shape[-1] // heads
    w = w.reshape(lead + (heads, 2, dim // 2))
    return jnp.swapaxes(w, -3, -2).reshape(lead + (heads * dim,))


def _permute_win(w):
    half = MLA_ROPE // 2
    kr = w[..., 640:672]
    kr_swapped = jnp.concatenate([kr[..., half:], kr[..., :half]], axis=-1)
    tiled = lambda a: jnp.tile(a, (1, 1, 4))
    return jnp.concatenate([w[..., 0:640], _half_split(w[..., 672:928], RET_HEADS),
                            _half_split(w[..., 928:1184], RET_HEADS), w[..., 1184:2464],
                            tiled(kr), tiled(kr_swapped)], axis=-1)


def _permute_wuq(w):
    half = MLA_ROPE // 2
    w = w.reshape(w.shape[:-1] + (MLA_HEADS, MLA_NOPE + MLA_ROPE))
    flat = lambda a: a.reshape(a.shape[:-2] + (-1,))
    rope = w[..., MLA_NOPE:]
    swapped = jnp.concatenate([rope[..., half:], rope[..., :half]], axis=-1)
    return jnp.concatenate([flat(w[..., :MLA_NOPE]), flat(rope), flat(swapped)], axis=-1)


def _permute_wukv(w):
    w = w.reshape(w.shape[:-1] + (MLA_HEADS, MLA_NOPE + MLA_V))
    flat = lambda a: a.reshape(a.shape[:-2] + (-1,))
    return jnp.concatenate([flat(w[..., :MLA_NOPE]), flat(w[..., MLA_NOPE:])], axis=-1)


def _ret_state_in(st):
    b = st.shape[0]
    st = st.reshape(b, RET_HEADS, 2, RET_DK // 2, RET_DV).transpose(0, 2, 1, 3, 4)
    eye = jnp.eye(RET_HEADS, dtype=st.dtype)
    full = st[:, :, :, :, None, :] * eye[None, None, :, None, :, None]
    return full.reshape(b, HEAD_W, HEAD_W)


def _ret_state_out(s):
    b = s.shape[0]
    s = s.reshape(b, 2, RET_HEADS, RET_DK // 2, RET_HEADS, RET_DV)
    blocks = [s[:, :, h, :, h, :].reshape(b, RET_DK, RET_DV) for h in range(RET_HEADS)]
    return jnp.stack(blocks, axis=1)


def _rope_angles(pos, dim):
    inv = ROPE_BASE ** (-jnp.arange(0, dim, 2, dtype=F32) / dim)
    return pos.astype(F32)[:, None] * inv[None, :]


def _rope_tables(pos, dim, reps):
    ang = _rope_angles(pos, dim)
    return jnp.tile(jnp.cos(ang), (1, reps)), jnp.tile(jnp.sin(ang), (1, reps))


def _rope_tables_swapped(pos, dim, reps):
    ang = _rope_angles(pos, dim)
    c, s = jnp.cos(ang), jnp.sin(ang)
    return jnp.tile(jnp.concatenate([c, c], axis=1), (1, reps)), jnp.tile(jnp.concatenate([-s, s], axis=1), (1, reps))


def _trunk(x, mods, pos0, cache, w):
    b, l, _ = x.shape
    depth = len(mods)
    pos = pos0 + jnp.arange(l)
    cm, sm = _rope_tables_swapped(pos, MLA_ROPE, MLA_HEADS)
    cr, sr = _rope_tables(pos, RET_DK, RET_HEADS)
    if cache is None:
        off = 0
    else:
        off = cache["lat"].shape[2]
        krt_cache = jnp.tile(cache["kr"], (1, 1, 1, 4)).astype(BF16)
        sb_cache = (cache["sk"].reshape(depth, b, off, HEAD_W), cache["sv"].reshape(depth, b, off, HEAD_W))
    earlier = []
    small = [[], []]
    for layer in range(depth):
        mod = mods[layer]
        last = layer == depth - 1
        p = _inproj(x, mod, w["g_norm1"][layer], w["w_in"][layer], w["g_q_norm"][layer], w["w_uq"][layer],
                    w["g_kv_norm"][layer], w["w_uk"][layer], w["w_vt"][layer], cm, sm, cr, sr,
                    earlier if last else None)
        if last:
            sb_new = (p["sk"], p["sv"], layer)
        else:
            earlier.append((p["lat"], p["kr"], p["sk"], p["sv"]))
            sb_new = (p["sk"][None], p["sv"][None], 0)
        if cache is None:
            mla_past, sb_past = (p["kp"], p["vt"]), sb_new
            s0 = jnp.zeros((b, HEAD_W, HEAD_W), F32)
            prev = jnp.zeros((b, CONV_W - 1, D_FF), F32)
        else:
            mla_past = _kvup(cache["lat"], layer, krt_cache[layer], w["w_uk"][layer], w["w_vt"][layer])
            sb_past = sb_cache + (layer,)
            s0 = _ret_state_in(cache["S"][layer])
            prev = cache["conv"][layer]
        mla = _mla(p["qn"], p["qr"], mla_past, (p["kp"], p["vt"]), off)
        sb = _sb(p["sq"], sb_past, sb_new, off)
        ret, s_new = _ret(p["rq"], p["rk"], p["rv"], p["rg"], s0, w["g_ret_norm"][layer])
        x, conv_state = _ffn(mla, ret, sb, x, mod, prev, w["w_o"][layer], w["g_norm2"][layer], w["w_up"][layer],
                             w["conv_w"][layer], w["conv_b"][layer], w["w_down"][layer], w["g_final"],
                             layer == depth - 1)
        small[0].append(_ret_state_out(s_new))
        small[1].append(conv_state)
    lat, kr, sk, sv = p["lat"], p["kr"], p["sk"], p["sv"]
    return x, [lat, kr, sk.reshape(depth, b, l, SB_HEADS, SB_DK), sv.reshape(depth, b, l, SB_HEADS, SB_DV),
               jnp.stack(small[0], axis=0), jnp.stack(small[1], axis=0)]


def kernel(x_prompt, x_sample, c_prompt, c_sample, cache_mla_latent, cache_mla_krope, cache_sb_k, cache_sb_v, state_ret, state_ffn_conv, w_in, g_q_norm, w_uq, g_kv_norm, w_ukv, g_ret_norm, w_o, w_up, conv_w, conv_b, w_down, g_norm1, g_norm2, w_ada, b_ada, g_final):
    depth = w_in.shape[0]
    bp = x_prompt.shape[0]
    row = lambda g: g.reshape(g.shape[0], 1, g.shape[-1])
    w = dict(
        w_in=_permute_win(w_in).astype(BF16),
        w_uq=_permute_wuq(w_uq).astype(BF16),
        w_uk=_permute_wukv(w_ukv)[..., :MLA_HEADS * MLA_NOPE].astype(BF16),
        w_vt=jnp.swapaxes(_permute_wukv(w_ukv)[..., MLA_HEADS * MLA_NOPE:], -1, -2).astype(BF16),
        w_o=w_o.astype(BF16), w_up=w_up.astype(BF16), w_down=w_down.astype(BF16),
        g_q_norm=row(g_q_norm), g_kv_norm=row(g_kv_norm), g_ret_norm=row(g_ret_norm),
        g_norm1=row(g_norm1), g_norm2=row(g_norm2), conv_w=conv_w, conv_b=row(conv_b),
        g_final=g_final.reshape(1, -1))
    mod = _ada(jnp.concatenate([c_prompt, c_sample], axis=0), w_ada, b_ada)
    mod = mod.reshape(depth, mod.shape[1], 6, D_MODEL)
    y_p, st_p = _trunk(x_prompt, [mod[l, :bp] for l in range(depth)], 0, None, w)
    cache = dict(lat=cache_mla_latent, kr=cache_mla_krope, sk=cache_sb_k, sv=cache_sb_v, S=state_ret,
                 conv=state_ffn_conv)
    y_s, st_s = _trunk(x_sample, [mod[l, bp:] for l in range(depth)], cache_mla_latent.shape[2], cache, w)
    return (y_p, y_s, *st_p, *st_s)
```

```python
import functools

import jax
import jax.numpy as jnp
from jax import lax
from jax.experimental import pallas as pl
from jax.experimental.pallas import tpu as pltpu

D_MODEL = 1024
CHUNK = 64
MLA_HEADS = 8
MLA_NOPE = 64
MLA_ROPE = 32
MLA_V = 64
MLA_Q_RANK = 384
MLA_KV_RANK = 256
RET_HEADS = 4
RET_DK = 64
RET_DV = 64
SB_HEADS = 4
SB_DK = 64
SB_DV = 64
D_FF = 2816
CONV_W = 3
ROPE_BASE = 10000.0
EPS = 1e-6

LANES = 128
HEAD_W = 256
MLA_NQ = MLA_HEADS * MLA_NOPE
MLA_NV = MLA_HEADS * MLA_V
MLA_KP_W = (MLA_HEADS // 2) * HEAD_W
ROW_TILE = 512
ATT_TILE = 256
FF_CHUNKS = ((0, 1536), (1536, D_FF))
MLA_EXP2_SCALE = (MLA_NOPE + MLA_ROPE) ** -0.5 * 1.4426950408889634
SB_LOG_ZERO = -104.0
VMEM_LIMIT = 56 * 1024 * 1024

F32 = jnp.float32
BF16 = jnp.bfloat16

_ZQ, _ZKV, _RQ, _RK, _RV, _RG, _SQ, _SK, _SV, _KRT, _IN_COLS_P = (
    0, 384, 640, 896, 1152, 1408, 1664, 1920, 2176, 2432, 2688)


def _dot(a, b):
    return jnp.dot(a, b, preferred_element_type=F32)


def _dot_nt(a, b):
    return lax.dot_general(a, b, (((1,), (1,)), ((), ())), preferred_element_type=F32)


def _dot_tn(a, b):
    return lax.dot_general(a, b, (((0,), (0,)), ((), ())), preferred_element_type=F32)


def _rms(x):
    return x * lax.rsqrt(jnp.mean(x * x, axis=-1, keepdims=True) + EPS)


def _silu(x):
    return x / (1.0 + jnp.exp(-x))


def _params(*sem):
    return pltpu.CompilerParams(dimension_semantics=sem, vmem_limit_bytes=VMEM_LIMIT)


def _const_spec(shape):
    nd = len(shape)
    return pl.BlockSpec(shape, lambda *_: (0,) * nd)


def _ada_kernel(c_ref, w_ref, b_ref, o_ref):
    a = _silu(c_ref[...]).astype(BF16)
    o_ref[0] = _dot(a, w_ref[0].astype(BF16)) + b_ref[0]


def _ada(c_all, w_ada, b_ada):
    depth, d, n = w_ada.shape
    rows = c_all.shape[0]
    tn = 1536
    return pl.pallas_call(
        _ada_kernel,
        out_shape=jax.ShapeDtypeStruct((depth, rows, n), F32),
        grid=(depth, n // tn),
        in_specs=[pl.BlockSpec((rows, d), lambda l, j: (0, 0)),
                  pl.BlockSpec((1, d, tn), lambda l, j: (l, 0, j)),
                  pl.BlockSpec((1, 1, tn), lambda l, j: (l, 0, j))],
        out_specs=pl.BlockSpec((1, rows, tn), lambda l, j: (l, 0, j)),
        compiler_params=_params("parallel", "parallel"),
        name="ada_mod",
    )(c_all, w_ada, b_ada.reshape(depth, 1, n))


_INPROJ_INPUTS = 13
_INPROJ_OUTPUTS = (("qn", MLA_NQ, BF16, False), ("qr", MLA_HEADS * MLA_ROPE, BF16, False), ("lat", MLA_KV_RANK, F32, True),
                   ("kp", MLA_KP_W, BF16, False), ("vt", None, BF16, False), ("kr", MLA_ROPE, F32, True),
                   ("rq", HEAD_W, BF16, False), ("rk", HEAD_W, F32, False), ("rv", HEAD_W, BF16, False),
                   ("rg", HEAD_W, F32, False), ("sq", HEAD_W, BF16, False), ("sk", HEAD_W, F32, True),
                   ("sv", HEAD_W, F32, True))


def _inproj_kernel(*refs):
    (x_ref, mod_ref, g1_ref, win_ref, gq_ref, wuq_ref, gkv_ref, wuk_ref, wvt_ref, cm_ref, sm_ref, cr_ref,
     sr_ref) = refs[:_INPROJ_INPUTS]
    names = [o[0] for o in _INPROJ_OUTPUTS]
    out = dict(zip(names, refs[len(refs) - len(names):]))
    stacked = [o[0] for o in _INPROJ_OUTPUTS if o[3]]
    earlier = refs[_INPROJ_INPUTS:len(refs) - len(names)]
    layer = len(earlier) // len(stacked)
    for r in range(layer):
        for k, name in enumerate(stacked):
            out[name][r] = earlier[r * len(stacked) + k][...]
    bb, tl, d = x_ref.shape
    rows = bb * tl
    mod = mod_ref[...]
    h = _rms(x_ref[...]) * g1_ref[...] * (1.0 + mod[:, 1:2, :]) + mod[:, 0:1, :]
    z = _dot(h.reshape(rows, d).astype(BF16), win_ref[...])

    def put(name, val, lanes=slice(None)):
        ref = out[name]
        lead = (layer,) if len(ref.shape) == 4 else ()
        ref[lead + (slice(None), slice(None), lanes)] = val.reshape(bb, tl, val.shape[-1]).astype(ref.dtype)

    def rope(x1, x2, c_ref, s_ref):
        c, s = c_ref[...][None], s_ref[...][None]
        x1 = x1.reshape(bb, tl, x1.shape[-1])
        x2 = x2.reshape(bb, tl, x2.shape[-1])
        return x1 * c - x2 * s, x1 * s + x2 * c

    def rope_swapped(a, b, width):
        a = a.reshape(bb, tl, width)
        b = b.reshape(bb, tl, width)
        return a * cm_ref[:, :width][None] + b * sm_ref[:, :width][None]

    q = _dot((_rms(z[:, _ZQ:_ZKV]) * gq_ref[...]).astype(BF16), wuq_ref[...])
    nr = MLA_HEADS * MLA_ROPE
    put("qn", q[:, :MLA_NQ] * MLA_EXP2_SCALE)
    put("qr", rope_swapped(q[:, MLA_NQ:MLA_NQ + nr], q[:, MLA_NQ + nr:], nr) * MLA_EXP2_SCALE)
    lat = _rms(z[:, _ZKV:_RQ]) * gkv_ref[...]
    put("lat", lat)
    kr = rope_swapped(z[:, _KRT:_KRT + LANES], z[:, _KRT + LANES:_KRT + 2 * LANES], LANES)
    put("kr", kr[:, :, :MLA_ROPE])
    lat_b = lat.astype(BF16)
    kn = _dot(lat_b, wuk_ref[...])
    for p in range(MLA_HEADS // 2):
        put("kp", kn[:, LANES * p:LANES * (p + 1)], slice(HEAD_W * p, HEAD_W * p + LANES))
        put("kp", kr, slice(HEAD_W * p + LANES, HEAD_W * (p + 1)))
    vt_ref = out["vt"]
    tkk = vt_ref.shape[-1]
    for bi in range(bb):
        for kk in range(tl // tkk):
            r0 = bi * tl + kk * tkk
            vt_ref[bi, kk] = _dot_nt(wvt_ref[...], lat_b[r0:r0 + tkk]).astype(vt_ref.dtype)
    a1, a2 = rope(z[:, _RQ:_RQ + LANES], z[:, _RQ + LANES:_RK], cr_ref, sr_ref)
    put("rq", a1 * (RET_DK ** -0.5), slice(0, LANES))
    put("rq", a2 * (RET_DK ** -0.5), slice(LANES, HEAD_W))
    b1, b2 = rope(z[:, _RK:_RK + LANES], z[:, _RK + LANES:_RV], cr_ref, sr_ref)
    put("rk", b1, slice(0, LANES))
    put("rk", b2, slice(LANES, HEAD_W))
    put("rv", z[:, _RV:_RG])
    put("rg", z[:, _RG:_SQ])
    put("sq", z[:, _SQ:_SK] * (SB_DK ** -0.5))
    put("sk", z[:, _SK:_SV])
    put("sv", z[:, _SV:_KRT])


def _inproj(x, mod, g1, win, gq, wuq, gkv, wuk, wvt, cm, sm, cr, sr, earlier):
    b, l, d = x.shape
    tl = min(l, ROW_TILE)
    bb = ROW_TILE // tl
    grid = (b // bb, l // tl)
    depth = None if earlier is None else len(earlier) + 1
    tok = lambda w: pl.BlockSpec((bb, tl, w), lambda i, t: (i, t, 0))
    stk = lambda w: pl.BlockSpec((depth, bb, tl, w), lambda i, t: (0, i, t, 0))
    pos = pl.BlockSpec((tl, LANES), lambda i, t: (t, 0))
    pos2 = pl.BlockSpec((tl, 2 * LANES), lambda i, t: (t, 0))
    stacked = lambda st: st and depth is not None
    extra = [] if earlier is None else [a for row in earlier for a in row]
    tkk = min(l, ATT_TILE)
    vt_shape = jax.ShapeDtypeStruct((b, l // tkk, MLA_HEADS * MLA_V, tkk), BF16)
    vt_spec = pl.BlockSpec((bb, tl // tkk, MLA_HEADS * MLA_V, tkk), lambda i, t: (i, t, 0, 0))
    res = pl.pallas_call(
        _inproj_kernel,
        out_shape=[vt_shape if w is None else jax.ShapeDtypeStruct((depth, b, l, w) if stacked(st) else (b, l, w), dt)
                   for _, w, dt, st in _INPROJ_OUTPUTS],
        grid=grid,
        in_specs=[tok(d), pl.BlockSpec((bb, 6, d), lambda i, t: (i, 0, 0)), _const_spec(g1.shape),
                  _const_spec(win.shape), _const_spec(gq.shape), _const_spec(wuq.shape), _const_spec(gkv.shape),
                  _const_spec(wuk.shape), _const_spec(wvt.shape), pos2, pos2, pos, pos]
        + [tok(a.shape[-1]) for a in extra],
        out_specs=[vt_spec if w is None else (stk(w) if stacked(st) else tok(w)) for _, w, _, st in _INPROJ_OUTPUTS],
        compiler_params=_params("parallel", "parallel"),
        name="in_proj",
    )(x, mod, g1, win, gq, wuq, gkv, wuk, wvt, cm, sm, cr, sr, *extra)
    return dict(zip([o[0] for o in _INPROJ_OUTPUTS], res))


def _kvup_kernel(lat_ref, krt_ref, wuk_ref, wvt_ref, kp_ref, vt_ref):
    lat_b = lat_ref[0].astype(BF16)
    kn = _dot(lat_b, wuk_ref[...])
    for p in range(MLA_HEADS // 2):
        kp_ref[0, :, HEAD_W * p:HEAD_W * p + LANES] = kn[:, LANES * p:LANES * (p + 1)].astype(kp_ref.dtype)
        kp_ref[0, :, HEAD_W * p + LANES:HEAD_W * (p + 1)] = krt_ref[0]
    tkk = vt_ref.shape[-1]
    for kk in range(vt_ref.shape[1]):
        vt_ref[0, kk] = _dot_nt(wvt_ref[...], lat_b[kk * tkk:(kk + 1) * tkk]).astype(vt_ref.dtype)


def _kvup(lat, layer, krt, wuk, wvt):
    _, b, lk, r = lat.shape
    nv = MLA_HEADS * MLA_V
    blk = lambda w: pl.BlockSpec((1, lk, w), lambda i: (i, 0, 0))
    return pl.pallas_call(
        _kvup_kernel,
        out_shape=(jax.ShapeDtypeStruct((b, lk, MLA_KP_W), BF16),
                   jax.ShapeDtypeStruct((b, lk // ATT_TILE, nv, ATT_TILE), BF16)),
        grid=(b,),
        in_specs=[pl.BlockSpec((None, 1, lk, r), lambda i: (layer, i, 0, 0)), blk(LANES), _const_spec(wuk.shape),
                  _const_spec(wvt.shape)],
        out_specs=[blk(MLA_KP_W), pl.BlockSpec((1, lk // ATT_TILE, nv, ATT_TILE), lambda i: (i, 0, 0, 0))],
        compiler_params=_params("parallel"),
        name="kv_up",
    )(lat, krt, wuk, wvt)


def _fori_pairs(n, body, init):
    carry = lax.fori_loop(0, n // 2, lambda i, c: body(2 * i + 1, body(2 * i, c)), init)
    return lax.fori_loop(0, n % 2, lambda i, c: body(n - 1, c), carry)


def _fold8(reduce, x):
    return reduce(x.reshape(x.shape[0] // 8, 8, x.shape[1]), axis=0)


def _mla_kernel(qn_ref, qr_ref, kp_ref, vtp_ref, kd_ref, vtd_ref, o_ref, s_scr, q_scr, mx_scr, m_scr, l_scr, acc_scr,
                *, tq, tk, off, fused, hp):
    qi = pl.program_id(1)
    nfull = (off + qi * tq) // tk
    nslot = 4 // hp
    wl = hp * tq
    lane = lax.broadcasted_iota(jnp.int32, (1, LANES), 1)
    key_c = lax.broadcasted_iota(jnp.int32, (tq, 1), 0) // CHUNK
    qry_c = (lax.broadcasted_iota(jnp.int32, (1, wl), 1) % tq) // CHUNK
    dmask = key_c <= qry_c
    neg = -1e30
    lane_v = lax.broadcasted_iota(jnp.int32, (1, HEAD_W), 1)
    feat = lax.broadcasted_iota(jnp.int32, (MLA_V, 1), 0)

    def build_queries(g):
        qr = qr_ref[0, :, LANES * g:LANES * (g + 1)]
        for hh in range(4):
            qn = qn_ref[0, :, LANES * (2 * g + hh // 2):LANES * (2 * g + hh // 2 + 1)]
            rows = slice((hh % hp) * tq, (hh % hp + 1) * tq)
            q_scr[g, hh // hp, rows, 0:LANES] = jnp.where((lane >> 6) == hh % 2, qn, jnp.zeros_like(qn))
            q_scr[g, hh // hp, rows, LANES:HEAD_W] = jnp.where((lane >> 5) == hh, qr, jnp.zeros_like(qr))

    def slot_scores(g, keys_of_pair, u):
        return _dot_nt(keys_of_pair(2 * g + (u * hp) // 2), q_scr[g, u])

    def vt_rows(g, u):
        return slice(MLA_V * (4 * g + u * hp), MLA_V * (4 * g + (u + 1) * hp))

    def scores_tile(g, j):
        ks = pl.ds(pl.multiple_of(j * tk, tk), tk)
        for u in range(nslot):
            s = slot_scores(g, lambda p: kp_ref[0, ks, HEAD_W * p:HEAD_W * (p + 1)], u)
            s_scr[g, j, u] = s
            mx_scr[g, u] = jnp.maximum(mx_scr[g, u], _fold8(jnp.max, s))

    def scores_own(g):
        own = []
        for u in range(nslot):
            if fused:
                ks_d = pl.ds(pl.multiple_of(nfull * tk, tk), tk)
                s_d = jnp.where(dmask, slot_scores(g, lambda p: kp_ref[0, ks_d, HEAD_W * p:HEAD_W * (p + 1)], u), neg)
                s_scr[g, nfull, u] = s_d
            else:
                s_d = jnp.where(dmask, slot_scores(g, lambda p: kd_ref[0, :, HEAD_W * p:HEAD_W * (p + 1)], u), neg)
                own.append(s_d)
            mx_scr[g, u] = _fold8(jnp.max, s_d)
        return own

    def values_tile(g, j):
        for u in range(nslot):
            p = jnp.exp2(s_scr[g, j, u] - m_scr[g, u, 0:1, :])
            l_scr[g, u] += _fold8(jnp.sum, p)
            acc_scr[g, u] += _dot(vtp_ref[0, j, vt_rows(g, u), :], p.astype(BF16))

    def values_start(g, own):
        for u in range(nslot):
            m_scr[g, u] = jnp.broadcast_to(jnp.max(mx_scr[g, u], axis=0, keepdims=True), (8, wl))
            if fused:
                l_scr[g, u] = jnp.zeros((8, wl), F32)
                acc_scr[g, u] = jnp.zeros((hp * MLA_V, wl), F32)
            else:
                p = jnp.exp2(own[u] - m_scr[g, u, 0:1, :])
                l_scr[g, u] = _fold8(jnp.sum, p)
                acc_scr[g, u] = _dot(vtd_ref[0, 0, vt_rows(g, u), :], p.astype(BF16))

    def emit(g):
        out_g = jnp.zeros((tq, HEAD_W), F32)
        for hh in range(4):
            u, a = hh // hp, hh % hp
            inv = 1.0 / jnp.sum(l_scr[g, u], axis=0, keepdims=True)
            place = jnp.where(feat + MLA_V * hh == lane_v, 1.0, 0.0).astype(BF16)
            placed = _dot_tn((acc_scr[g, u, MLA_V * a:MLA_V * (a + 1), :] * inv).astype(BF16), place)
            out_g = out_g + placed[a * tq:(a + 1) * tq]
        o_ref[0, :, HEAD_W * g:HEAD_W * (g + 1)] = out_g.astype(o_ref.dtype)

    def loop(n, *works):
        def body(j, carry):
            for w in works:
                w(j)
            return carry
        _fori_pairs(n, body, 0)

    build_queries(0)
    build_queries(1)
    own0 = scores_own(0)
    loop(nfull, lambda j: scores_tile(0, j))
    values_start(0, own0)
    own1 = scores_own(1)
    loop(nfull, lambda j: scores_tile(1, j), lambda j: values_tile(0, j))
    if fused:
        values_tile(0, nfull)
    values_start(1, own1)
    emit(0)
    loop(nfull + 1 if fused else nfull, lambda j: values_tile(1, j))
    emit(1)


def _mla(qn, qr, past, diag, off):
    b, l, _ = qn.shape
    lp = past[0].shape[1]
    tq = min(l, ATT_TILE)
    tk = ATT_TILE
    nv = MLA_NV
    fused = off == 0 and tq == tk
    qblk = lambda w: pl.BlockSpec((1, tq, w), lambda i, t: (i, t, 0))
    nmax = (off + l - tq) // tk + (1 if fused else 0)
    hp = 1 if tq % 128 == 0 else 2
    wl = hp * tq
    kern = functools.partial(_mla_kernel, tq=tq, tk=tk, off=off, fused=fused, hp=hp)
    return pl.pallas_call(
        kern,
        out_shape=jax.ShapeDtypeStruct((b, l, nv), BF16),
        grid=(b, l // tq),
        in_specs=[qblk(MLA_NQ), qblk(MLA_HEADS * MLA_ROPE), pl.BlockSpec((1, lp, MLA_KP_W), lambda i, t: (i, 0, 0)),
                  pl.BlockSpec((1, lp // tk, nv, tk), lambda i, t: (i, 0, 0, 0)), qblk(MLA_KP_W),
                  pl.BlockSpec((1, 1, nv, tq), lambda i, t: (i, t, 0, 0))],
        out_specs=qblk(nv),
        scratch_shapes=[pltpu.VMEM((2, nmax, 4 // hp, tk, wl), F32), pltpu.VMEM((2, 4 // hp, wl, HEAD_W), BF16),
                        pltpu.VMEM((2, 4 // hp, 8, wl), F32), pltpu.VMEM((2, 4 // hp, 8, wl), F32),
                        pltpu.VMEM((2, 4 // hp, 8, wl), F32), pltpu.VMEM((2, 4 // hp, hp * MLA_V, wl), F32)],
        compiler_params=_params("parallel", "parallel"),
        name="mla_attn",
    )(qn, qr, *past, *diag)


def _neg_suffix_matrix(n):
    j = lax.broadcasted_iota(jnp.int32, (n, n), 0)
    s = lax.broadcasted_iota(jnp.int32, (n, n), 1)
    return jnp.where(j >= s, -1.0, 0.0).astype(BF16)


def _retention_chunk(rq_ref, rk_ref, rv_ref, rg_ref, gn_ref, dec_ref, qd_ref, kdec_ref, sd_ref, ro_ref, s_ref):
    c = rq_ref.shape[1]
    lane = lax.broadcasted_iota(jnp.int32, (1, HEAD_W), 1)
    krow = lax.broadcasted_iota(jnp.int32, (HEAD_W, 1), 0)
    bd_mask = ((krow & 127) >> 5) == (lane >> 6)
    avg = jnp.where((krow >> 6) == (lane >> 6), 1.0 / RET_DV, 0.0).astype(BF16)
    q = rq_ref[0]
    kf = rk_ref[0]
    k = kf.astype(BF16)
    v = rv_ref[0]
    state = s_ref[0]
    o = _dot(q, state.astype(BF16)) * qd_ref[...]
    qs = jnp.concatenate([jnp.where(((lane & 127) >> 5) == h, q, jnp.zeros_like(q)) for h in range(RET_HEADS)],
                         axis=0)
    pv = _dot((_dot_nt(qs, k) * dec_ref[...]).astype(BF16), v)
    for h in range(RET_HEADS):
        o = o + jnp.where((lane >> 6) == h, pv[h * c:(h + 1) * c], 0.0)
    upd = _dot_tn((kf * kdec_ref[...]).astype(BF16), v)
    s_ref[0] = sd_ref[...] * state + jnp.where(bd_mask, upd, 0.0)
    dlt = o - _group_mean(o, avg)
    ro = dlt * lax.rsqrt(_group_mean(dlt * dlt, avg) + EPS) * gn_ref[...]
    ro_ref[0] = (_silu(rg_ref[0]) * ro).astype(ro_ref.dtype)


def _sb_kernel(q_ref, kp_ref, vp_ref, kd_ref, vd_ref, rq_ref, rk_ref, rv_ref, rg_ref, s0_ref, gn_ref, dec_ref, qd_ref,
               kdec_ref, sd_ref, o_ref, ro_ref, s_ref, kb_ref, vb_ref, q_scr, run_scr, acc_scr, *, tq, tk, off):
    qi = pl.program_id(1)

    @pl.when(qi == 0)
    def _():
        kb_ref[...] = kp_ref[0].astype(BF16)
        vb_ref[...] = vp_ref[0].astype(BF16)
        s_ref[0] = s0_ref[0]

    def retention():
        _retention_chunk(rq_ref, rk_ref, rv_ref, rg_ref, gn_ref, dec_ref, qd_ref, kdec_ref, sd_ref, ro_ref, s_ref)

    nfull = (off + qi * tq) // tk
    lane = lax.broadcasted_iota(jnp.int32, (1, HEAD_W), 1)
    nh = SB_HEADS
    dmask = (lax.broadcasted_iota(jnp.int32, (1, tq), 1)
             < lax.broadcasted_iota(jnp.int32, (nh * tq, 1), 0) % tq)
    t_diag = _neg_suffix_matrix(tq)
    t_full = t_diag if tk == tq else _neg_suffix_matrix(tk)
    q = q_ref[0]

    def exponent(k, mask, tmat):
        zz = _dot_nt(q_scr[...], k)
        sp = jnp.maximum(zz, 0.0) + jnp.log(1.0 + jnp.exp(-jnp.abs(zz)))
        if mask is not None:
            sp = jnp.where(mask, sp, 0.0)
        hi = sp.astype(BF16)
        lo = (sp - hi.astype(F32)).astype(BF16)
        return zz + _dot(hi, tmat) + _dot(lo, tmat), jnp.sum(sp, axis=-1, keepdims=True)

    def carry_on(run):
        run_scr[...] = jnp.broadcast_to(run, (nh * tq, LANES))
        return jnp.max(run)

    def own_weights():
        e, rs = exponent(kd_ref[0].astype(BF16), dmask, t_diag)
        return jnp.where(dmask, jnp.exp(e), 0.0).astype(BF16), rs

    def own_only():
        retention()
        w, rs = own_weights()
        acc_scr[...] = _dot(w, vd_ref[0].astype(BF16))
        return carry_on(-rs)

    def own_and_previous():
        retention()
        ks = pl.ds(pl.multiple_of((nfull - 1) * tk, tk), tk)
        w, rs = own_weights()
        e1, rs1 = exponent(kb_ref[ks, :], None, t_full)
        w1 = jnp.exp(e1 - rs).astype(BF16)
        acc_scr[...] = _dot(w, vd_ref[0].astype(BF16)) + _dot(w1, vb_ref[ks, :])
        return carry_on(-(rs + rs1))

    for h in range(nh):
        q_scr[h * tq:(h + 1) * tq, :] = jnp.where((lane >> 6) == h, q, jnp.zeros_like(q))
    top = lax.cond(nfull >= 1, own_and_previous, own_only)

    def more(state):
        j, top = state
        return jnp.logical_and(j >= 0, top > SB_LOG_ZERO)

    def body(state):
        j, _ = state
        ks = pl.ds(pl.multiple_of(j * tk, tk), tk)
        e, rs = exponent(kb_ref[ks, :], None, t_full)
        run = run_scr[...]
        w = jnp.exp(e + jnp.concatenate([run] * (tk // LANES), -1))
        acc_scr[...] += _dot(w.astype(BF16), vb_ref[ks, :])
        run = run - rs
        run_scr[...] = run
        return j - 1, jnp.max(run)

    lax.while_loop(more, body, (nfull - 2, top))
    out = jnp.zeros((tq, HEAD_W), F32)
    for h in range(nh):
        out = jnp.where((lane >> 6) == h, acc_scr[h * tq:(h + 1) * tq, :], out)
    o_ref[0] = out.astype(o_ref.dtype)


def _sb_ret(sq, past, diag, off, rq, rk, rv, rg, s0, gn):
    b, l, _ = sq.shape
    lp = past[0].shape[2]
    tq = min(l, ATT_TILE)
    dec, qd, kd, sd = _ret_consts(tq)
    dec = dec.reshape(RET_HEADS * tq, tq)
    qblk = pl.BlockSpec((1, tq, HEAD_W), lambda i, t: (i, t, 0))
    dblk = pl.BlockSpec((None, 1, tq, HEAD_W), lambda i, t: (diag[2], i, t, 0))
    kblk = pl.BlockSpec((None, 1, lp, HEAD_W), lambda i, t: (past[2], i, 0, 0))
    sblk = pl.BlockSpec((1, HEAD_W, HEAD_W), lambda i, t: (i, 0, 0))
    kern = functools.partial(_sb_kernel, tq=tq, tk=ATT_TILE, off=off)
    return pl.pallas_call(
        kern,
        out_shape=[jax.ShapeDtypeStruct((b, l, HEAD_W), BF16), jax.ShapeDtypeStruct((b, l, HEAD_W), BF16),
                   jax.ShapeDtypeStruct((b, HEAD_W, HEAD_W), F32)],
        grid=(b, l // tq),
        in_specs=[qblk, kblk, kblk, dblk, dblk, qblk, qblk, qblk, qblk, sblk, _const_spec(gn.shape),
                  _const_spec(dec.shape), _const_spec(qd.shape), _const_spec(kd.shape), _const_spec(sd.shape)],
        out_specs=[qblk, qblk, sblk],
        scratch_shapes=[pltpu.VMEM((lp, HEAD_W), BF16), pltpu.VMEM((lp, HEAD_W), BF16),
                        pltpu.VMEM((SB_HEADS * tq, HEAD_W), BF16), pltpu.VMEM((SB_HEADS * tq, LANES), F32),
                        pltpu.VMEM((SB_HEADS * tq, HEAD_W), F32)],
        compiler_params=_params("parallel", "arbitrary"),
        name="sb_ret",
    )(sq, *past[:2], *diag[:2], rq, rk, rv, rg, s0, gn, dec, qd, kd, sd)


def _group_mean(x, avg):
    hi = x.astype(BF16)
    lo = (x - hi.astype(F32)).astype(BF16)
    return _dot(hi, avg) + _dot(lo, avg)


def _ret_consts(c):
    lg = jnp.log(1.0 - 2.0 ** (-5.0 - jnp.arange(RET_HEADS, dtype=F32)))
    i = jnp.arange(c, dtype=F32)
    rel = i[:, None] - i[None, :]
    dec = jnp.where(rel >= 0, jnp.exp(lg[:, None, None] * jnp.maximum(rel, 0.0)), 0.0)
    v_head = jnp.arange(HEAD_W) // RET_DV
    k_head = (jnp.arange(HEAD_W) % 128) // (RET_DK // 2)
    qd = jnp.exp(lg[None, v_head] * (i[:, None] + 1.0))
    kd = jnp.exp(lg[None, k_head] * (c - 1.0 - i[:, None]))
    sd = jnp.exp(lg * c)[v_head][None, :]
    return dec, qd, kd, sd


def _ffn_kernel(mla_ref, ret_ref, sb_ref, x_ref, mod_ref, prev_ref, wo_ref, g2_ref, wup_ref, cw_ref, cb_ref,
                wdn_ref, gf_ref, o_ref, st_ref, a_scr, carry, *, final_norm):
    bb, tl, d = x_ref.shape
    rows = bb * tl
    t = pl.program_id(1)
    nt = pl.num_programs(1)

    @pl.when(t == 0)
    def _():
        carry[:, 6:8, :] = prev_ref[...]

    mod = mod_ref[...]
    cat = jnp.concatenate([mla_ref[...].reshape(rows, MLA_NV), ret_ref[...].reshape(rows, HEAD_W),
                           sb_ref[...].reshape(rows, HEAD_W)], axis=-1)
    x1 = x_ref[...] + mod[:, 2:3, :] * _dot(cat, wo_ref[...]).reshape(bb, tl, d)
    h = (_rms(x1) * g2_ref[...] * (1.0 + mod[:, 4:5, :]) + mod[:, 3:4, :]).reshape(rows, d).astype(BF16)
    f = jnp.zeros((rows, d), F32)
    for c0, c1 in FF_CHUNKS:
        cs, n = slice(c0, c1), c1 - c0
        a = _dot(h, wup_ref[:, cs]).reshape(bb, tl, n)
        b = _dot(h, wup_ref[:, D_FF + c0:D_FF + c1])
        a_scr[:, 8:, :n] = a
        a_scr[:, 6:8, :n] = carry[:, 6:8, cs]
        cw = cw_ref[...]
        conv = (cb_ref[:, cs] + cw[0:1, cs] * a_scr[:, 6:6 + tl, :n] + cw[1:2, cs] * a_scr[:, 7:7 + tl, :n]
                + cw[2:3, cs] * a)
        carry[:, 6:8, cs] = a_scr[:, tl + 6:tl + 8, :n]
        y = (_silu(conv).reshape(rows, n) * b).astype(BF16)
        f = f + _dot(y, wdn_ref[cs, :])
    x2 = x1 + mod[:, 5:6, :] * f.reshape(bb, tl, d)
    if final_norm:
        x2 = _rms(x2) * gf_ref[...]
    o_ref[...] = x2

    @pl.when(t == nt - 1)
    def _():
        st_ref[...] = carry[:, 6:8, :]


def _ffn(mla, ret, sb, x, mod, prev, wo, g2, wup, cw, cb, wdn, gf, final_norm):
    b, l, d = x.shape
    tl = min(l, ROW_TILE)
    bb = ROW_TILE // tl
    tok = lambda w: pl.BlockSpec((bb, tl, w), lambda i, t: (i, t, 0))
    st = pl.BlockSpec((bb, CONV_W - 1, D_FF), lambda i, t: (i, 0, 0))
    once = lambda a: pl.BlockSpec(a.shape, lambda i, t: (0, 0), pipeline_mode=pl.Buffered(1))
    return pl.pallas_call(
        functools.partial(_ffn_kernel, final_norm=final_norm),
        out_shape=[jax.ShapeDtypeStruct((b, l, d), F32), jax.ShapeDtypeStruct((b, CONV_W - 1, D_FF), F32)],
        grid=(b // bb, l // tl),
        in_specs=[tok(MLA_NV), tok(HEAD_W), tok(HEAD_W), tok(d), pl.BlockSpec((bb, 6, d), lambda i, t: (i, 0, 0)), st,
                  once(wo), _const_spec(g2.shape), once(wup), _const_spec(cw.shape), _const_spec(cb.shape),
                  once(wdn), _const_spec(gf.shape)],
        out_specs=[tok(d), st],
        scratch_shapes=[pltpu.VMEM((bb, tl + 8, max(c1 - c0 for c0, c1 in FF_CHUNKS)), F32),
                        pltpu.VMEM((bb, 8, D_FF), F32)],
        compiler_params=_params("parallel", "arbitrary"),
        name="conv_ffn",
    )(mla, ret, sb, x, mod, prev, wo, g2, wup, cw, cb, wdn, gf)


def _half_split(w, heads):
    lead = w.shape[:-1]
    dim = w.shape[-1] // heads
    w = w.reshape(lead + (heads, 2, dim // 2))
    return jnp.swapaxes(w, -3, -2).reshape(lead + (heads * dim,))


def _permute_win(w):
    half = MLA_ROPE // 2
    kr = w[..., 640:672]
    kr_swapped = jnp.concatenate([kr[..., half:], kr[..., :half]], axis=-1)
    tiled = lambda a: jnp.tile(a, (1, 1, 4))
    return jnp.concatenate([w[..., 0:640], _half_split(w[..., 672:928], RET_HEADS),
                            _half_split(w[..., 928:1184], RET_HEADS), w[..., 1184:2464],
                            tiled(kr), tiled(kr_swapped)], axis=-1)


def _permute_wuq(w):
    half = MLA_ROPE // 2
    w = w.reshape(w.shape[:-1] + (MLA_HEADS, MLA_NOPE + MLA_ROPE))
    flat = lambda a: a.reshape(a.shape[:-2] + (-1,))
    rope = w[..., MLA_NOPE:]
    swapped = jnp.concatenate([rope[..., half:], rope[..., :half]], axis=-1)
    return jnp.concatenate([flat(w[..., :MLA_NOPE]), flat(rope), flat(swapped)], axis=-1)


def _permute_wukv(w):
    w = w.reshape(w.shape[:-1] + (MLA_HEADS, MLA_NOPE + MLA_V))
    flat = lambda a: a.reshape(a.shape[:-2] + (-1,))
    return jnp.concatenate([flat(w[..., :MLA_NOPE]), flat(w[..., MLA_NOPE:])], axis=-1)


def _ret_state_in(st):
    b = st.shape[0]
    st = st.reshape(b, RET_HEADS, 2, RET_DK // 2, RET_DV).transpose(0, 2, 1, 3, 4)
    eye = jnp.eye(RET_HEADS, dtype=st.dtype)
    full = st[:, :, :, :, None, :] * eye[None, None, :, None, :, None]
    return full.reshape(b, HEAD_W, HEAD_W)


def _ret_state_out(s):
    b = s.shape[0]
    s = s.reshape(b, 2, RET_HEADS, RET_DK // 2, RET_HEADS, RET_DV)
    blocks = [s[:, :, h, :, h, :].reshape(b, RET_DK, RET_DV) for h in range(RET_HEADS)]
    return jnp.stack(blocks, axis=1)


def _rope_angles(pos, dim):
    inv = ROPE_BASE ** (-jnp.arange(0, dim, 2, dtype=F32) / dim)
    return pos.astype(F32)[:, None] * inv[None, :]


def _rope_tables(pos, dim, reps):
    ang = _rope_angles(pos, dim)
    return jnp.tile(jnp.cos(ang), (1, reps)), jnp.tile(jnp.sin(ang), (1, reps))


def _rope_tables_swapped(pos, dim, reps):
    ang = _rope_angles(pos, dim)
    c, s = jnp.cos(ang), jnp.sin(ang)
    return jnp.tile(jnp.concatenate([c, c], axis=1), (1, reps)), jnp.tile(jnp.concatenate([-s, s], axis=1), (1, reps))


def _trunk(x, mods, pos0, cache, w):
    b, l, _ = x.shape
    depth = len(mods)
    pos = pos0 + jnp.arange(l)
    cm, sm = _rope_tables_swapped(pos, MLA_ROPE, MLA_HEADS)
    cr, sr = _rope_tables(pos, RET_DK, RET_HEADS)
    if cache is None:
        off = 0
    else:
        off = cache["lat"].shape[2]
        krt_cache = jnp.tile(cache["kr"], (1, 1, 1, 4)).astype(BF16)
        sb_cache = (cache["sk"].reshape(depth, b, off, HEAD_W), cache["sv"].reshape(depth, b, off, HEAD_W))
    earlier = []
    small = [[], []]
    for layer in range(depth):
        mod = mods[layer]
        last = layer == depth - 1
        p = _inproj(x, mod, w["g_norm1"][layer], w["w_in"][layer], w["g_q_norm"][layer], w["w_uq"][layer],
                    w["g_kv_norm"][layer], w["w_uk"][layer], w["w_vt"][layer], cm, sm, cr, sr,
                    earlier if last else None)
        if last:
            sb_new = (p["sk"], p["sv"], layer)
        else:
            earlier.append((p["lat"], p["kr"], p["sk"], p["sv"]))
            sb_new = (p["sk"][None], p["sv"][None], 0)
        if cache is None:
            mla_past, sb_past = (p["kp"], p["vt"]), sb_new
            s0 = jnp.zeros((b, HEAD_W, HEAD_W), F32)
            prev = jnp.zeros((b, CONV_W - 1, D_FF), F32)
        else:
            mla_past = _kvup(cache["lat"], layer, krt_cache[layer], w["w_uk"][layer], w["w_vt"][layer])
            sb_past = sb_cache + (layer,)
            s0 = _ret_state_in(cache["S"][layer])
            prev = cache["conv"][layer]
        mla = _mla(p["qn"], p["qr"], mla_past, (p["kp"], p["vt"]), off)
        sb, ret, s_new = _sb_ret(p["sq"], sb_past, sb_new, off, p["rq"], p["rk"], p["rv"], p["rg"], s0,
                                 w["g_ret_norm"][layer])
        x, conv_state = _ffn(mla, ret, sb, x, mod, prev, w["w_o"][layer], w["g_norm2"][layer], w["w_up"][layer],
                             w["conv_w"][layer], w["conv_b"][layer], w["w_down"][layer], w["g_final"],
                             layer == depth - 1)
        small[0].append(_ret_state_out(s_new))
        small[1].append(conv_state)
    lat, kr, sk, sv = p["lat"], p["kr"], p["sk"], p["sv"]
    return x, [lat, kr, sk.reshape(depth, b, l, SB_HEADS, SB_DK), sv.reshape(depth, b, l, SB_HEADS, SB_DV),
               jnp.stack(small[0], axis=0), jnp.stack(small[1], axis=0)]


def kernel(x_prompt, x_sample, c_prompt, c_sample, cache_mla_latent, cache_mla_krope, cache_sb_k, cache_sb_v, state_ret, state_ffn_conv, w_in, g_q_norm, w_uq, g_kv_norm, w_ukv, g_ret_norm, w_o, w_up, conv_w, conv_b, w_down, g_norm1, g_norm2, w_ada, b_ada, g_final):
    depth = w_in.shape[0]
    bp = x_prompt.shape[0]
    row = lambda g: g.reshape(g.shape[0], 1, g.shape[-1])
    w = dict(
        w_in=_permute_win(w_in).astype(BF16),
        w_uq=_permute_wuq(w_uq).astype(BF16),
        w_uk=_permute_wukv(w_ukv)[..., :MLA_HEADS * MLA_NOPE].astype(BF16),
        w_vt=jnp.swapaxes(_permute_wukv(w_ukv)[..., MLA_HEADS * MLA_NOPE:], -1, -2).astype(BF16),
        w_o=w_o.astype(BF16), w_up=w_up.astype(BF16), w_down=w_down.astype(BF16),
        g_q_norm=row(g_q_norm), g_kv_norm=row(g_kv_norm), g_ret_norm=row(g_ret_norm),
        g_norm1=row(g_norm1), g_norm2=row(g_norm2), conv_w=conv_w, conv_b=row(conv_b),
        g_final=g_final.reshape(1, -1))
    mod = _ada(jnp.concatenate([c_prompt, c_sample], axis=0), w_ada, b_ada)
    mod = mod.reshape(depth, mod.shape[1], 6, D_MODEL)
    y_p, st_p = _trunk(x_prompt, [mod[l, :bp] for l in range(depth)], 0, None, w)
    cache = dict(lat=cache_mla_latent, kr=cache_mla_krope, sk=cache_sb_k, sv=cache_sb_v, S=state_ret,
                 conv=state_ffn_conv)
    y_s, st_s = _trunk(x_sample, [mod[l, bp:] for l in range(depth)], cache_mla_latent.shape[2], cache, w)
    return (y_p, y_s, *st_p, *st_s)
```

```python
import functools

import jax
import jax.numpy as jnp
from jax import lax
from jax.experimental import pallas as pl
from jax.experimental.pallas import tpu as pltpu

D_MODEL = 1024
CHUNK = 64
MLA_HEADS = 8
MLA_NOPE = 64
MLA_ROPE = 32
MLA_V = 64
MLA_Q_RANK = 384
MLA_KV_RANK = 256
RET_HEADS = 4
RET_DK = 64
RET_DV = 64
SB_HEADS = 4
SB_DK = 64
SB_DV = 64
D_FF = 2816
CONV_W = 3
ROPE_BASE = 10000.0
EPS = 1e-6

LANES = 128
HEAD_W = 256
MLA_NQ = MLA_HEADS * MLA_NOPE
MLA_NV = MLA_HEADS * MLA_V
ROW_TILE = 512
ATT_TILE = 256
FF_CHUNKS = ((0, 1536), (1536, D_FF))
MLA_EXP2_SCALE = (MLA_NOPE + MLA_ROPE) ** -0.5 * 1.4426950408889634
SB_LOG_ZERO = -104.0
VMEM_LIMIT = 56 * 1024 * 1024

F32 = jnp.float32
BF16 = jnp.bfloat16

_ZQ, _ZKV, _RQ, _RK, _RV, _RG, _SQ, _SK, _SV, _KRT, _IN_COLS_P = (
    0, 384, 640, 896, 1152, 1408, 1664, 1920, 2176, 2432, 2688)


def _dot(a, b):
    return jnp.dot(a, b, preferred_element_type=F32)


def _dot_nt(a, b):
    return lax.dot_general(a, b, (((1,), (1,)), ((), ())), preferred_element_type=F32)


def _dot_tn(a, b):
    return lax.dot_general(a, b, (((0,), (0,)), ((), ())), preferred_element_type=F32)


def _rms(x):
    return x * lax.rsqrt(jnp.mean(x * x, axis=-1, keepdims=True) + EPS)


def _silu(x):
    return x / (1.0 + jnp.exp(-x))


def _params(*sem):
    return pltpu.CompilerParams(dimension_semantics=sem, vmem_limit_bytes=VMEM_LIMIT)


def _const_spec(shape):
    nd = len(shape)
    return pl.BlockSpec(shape, lambda *_: (0,) * nd)


def _ada_kernel(c_ref, w_ref, b_ref, o_ref):
    a = _silu(c_ref[...]).astype(BF16)
    o_ref[0] = _dot(a, w_ref[0].astype(BF16)) + b_ref[0]


def _ada(c_all, w_ada, b_ada):
    depth, d, n = w_ada.shape
    rows = c_all.shape[0]
    tn = 1536
    return pl.pallas_call(
        _ada_kernel,
        out_shape=jax.ShapeDtypeStruct((depth, rows, n), F32),
        grid=(depth, n // tn),
        in_specs=[pl.BlockSpec((rows, d), lambda l, j: (0, 0)),
                  pl.BlockSpec((1, d, tn), lambda l, j: (l, 0, j)),
                  pl.BlockSpec((1, 1, tn), lambda l, j: (l, 0, j))],
        out_specs=pl.BlockSpec((1, rows, tn), lambda l, j: (l, 0, j)),
        compiler_params=_params("parallel", "parallel"),
        name="ada_mod",
    )(c_all, w_ada, b_ada.reshape(depth, 1, n))


_INPROJ_INPUTS = 13
_INPROJ_OUTPUTS = (("qn", MLA_NQ, BF16, False), ("qr", MLA_HEADS * MLA_ROPE, BF16, False), ("lat", MLA_KV_RANK, F32, True),
                   ("kn", MLA_NQ, BF16, False), ("krt", LANES, BF16, False), ("vt", None, BF16, False), ("kr", MLA_ROPE, F32, True),
                   ("rq", HEAD_W, BF16, False), ("rk", HEAD_W, F32, False), ("rv", HEAD_W, BF16, False),
                   ("rg", HEAD_W, F32, False), ("sq", HEAD_W, BF16, False), ("sk", HEAD_W, F32, True),
                   ("sv", HEAD_W, F32, True))


def _inproj_kernel(*refs):
    (x_ref, mod_ref, g1_ref, win_ref, gq_ref, wuq_ref, gkv_ref, wuk_ref, wvt_ref, cm_ref, sm_ref, cr_ref,
     sr_ref) = refs[:_INPROJ_INPUTS]
    names = [o[0] for o in _INPROJ_OUTPUTS]
    out = dict(zip(names, refs[len(refs) - len(names):]))
    stacked = [o[0] for o in _INPROJ_OUTPUTS if o[3]]
    earlier = refs[_INPROJ_INPUTS:len(refs) - len(names)]
    layer = len(earlier) // len(stacked)
    for r in range(layer):
        for k, name in enumerate(stacked):
            out[name][r] = earlier[r * len(stacked) + k][...]
    bb, tl, d = x_ref.shape
    rows = bb * tl
    mod = mod_ref[...]
    h = _rms(x_ref[...]) * g1_ref[...] * (1.0 + mod[:, 1:2, :]) + mod[:, 0:1, :]
    z = _dot(h.reshape(rows, d).astype(BF16), win_ref[...])

    def put(name, val, lanes=slice(None)):
        ref = out[name]
        lead = (layer,) if len(ref.shape) == 4 else ()
        ref[lead + (slice(None), slice(None), lanes)] = val.reshape(bb, tl, val.shape[-1]).astype(ref.dtype)

    def rope(x1, x2, c_ref, s_ref):
        c, s = c_ref[...][None], s_ref[...][None]
        x1 = x1.reshape(bb, tl, x1.shape[-1])
        x2 = x2.reshape(bb, tl, x2.shape[-1])
        return x1 * c - x2 * s, x1 * s + x2 * c

    def rope_swapped(a, b, width):
        a = a.reshape(bb, tl, width)
        b = b.reshape(bb, tl, width)
        return a * cm_ref[:, :width][None] + b * sm_ref[:, :width][None]

    q = _dot((_rms(z[:, _ZQ:_ZKV]) * gq_ref[...]).astype(BF16), wuq_ref[...])
    nr = MLA_HEADS * MLA_ROPE
    put("qn", q[:, :MLA_NQ] * MLA_EXP2_SCALE)
    put("qr", rope_swapped(q[:, MLA_NQ:MLA_NQ + nr], q[:, MLA_NQ + nr:], nr) * MLA_EXP2_SCALE)
    lat = _rms(z[:, _ZKV:_RQ]) * gkv_ref[...]
    put("lat", lat)
    kr = rope_swapped(z[:, _KRT:_KRT + LANES], z[:, _KRT + LANES:_KRT + 2 * LANES], LANES)
    put("kr", kr[:, :, :MLA_ROPE])
    lat_b = lat.astype(BF16)
    put("kn", _dot(lat_b, wuk_ref[...]))
    put("krt", kr)
    vt_ref = out["vt"]
    tkk = vt_ref.shape[-1]
    for bi in range(bb):
        for kk in range(tl // tkk):
            r0 = bi * tl + kk * tkk
            vt_ref[bi, kk] = _dot_nt(wvt_ref[...], lat_b[r0:r0 + tkk]).astype(vt_ref.dtype)
    a1, a2 = rope(z[:, _RQ:_RQ + LANES], z[:, _RQ + LANES:_RK], cr_ref, sr_ref)
    put("rq", a1 * (RET_DK ** -0.5), slice(0, LANES))
    put("rq", a2 * (RET_DK ** -0.5), slice(LANES, HEAD_W))
    b1, b2 = rope(z[:, _RK:_RK + LANES], z[:, _RK + LANES:_RV], cr_ref, sr_ref)
    put("rk", b1, slice(0, LANES))
    put("rk", b2, slice(LANES, HEAD_W))
    put("rv", z[:, _RV:_RG])
    put("rg", z[:, _RG:_SQ])
    put("sq", z[:, _SQ:_SK] * (SB_DK ** -0.5))
    put("sk", z[:, _SK:_SV])
    put("sv", z[:, _SV:_KRT])


def _inproj(x, mod, g1, win, gq, wuq, gkv, wuk, wvt, cm, sm, cr, sr, earlier):
    b, l, d = x.shape
    tl = min(l, ROW_TILE)
    bb = ROW_TILE // tl
    grid = (b // bb, l // tl)
    depth = None if earlier is None else len(earlier) + 1
    tok = lambda w: pl.BlockSpec((bb, tl, w), lambda i, t: (i, t, 0))
    stk = lambda w: pl.BlockSpec((depth, bb, tl, w), lambda i, t: (0, i, t, 0))
    pos = pl.BlockSpec((tl, LANES), lambda i, t: (t, 0))
    pos2 = pl.BlockSpec((tl, 2 * LANES), lambda i, t: (t, 0))
    stacked = lambda st: st and depth is not None
    extra = [] if earlier is None else [a for row in earlier for a in row]
    tkk = min(l, ATT_TILE)
    vt_shape = jax.ShapeDtypeStruct((b, l // tkk, MLA_HEADS * MLA_V, tkk), BF16)
    vt_spec = pl.BlockSpec((bb, tl // tkk, MLA_HEADS * MLA_V, tkk), lambda i, t: (i, t, 0, 0))
    res = pl.pallas_call(
        _inproj_kernel,
        out_shape=[vt_shape if w is None else jax.ShapeDtypeStruct((depth, b, l, w) if stacked(st) else (b, l, w), dt)
                   for _, w, dt, st in _INPROJ_OUTPUTS],
        grid=grid,
        in_specs=[tok(d), pl.BlockSpec((bb, 6, d), lambda i, t: (i, 0, 0)), _const_spec(g1.shape),
                  _const_spec(win.shape), _const_spec(gq.shape), _const_spec(wuq.shape), _const_spec(gkv.shape),
                  _const_spec(wuk.shape), _const_spec(wvt.shape), pos2, pos2, pos, pos]
        + [tok(a.shape[-1]) for a in extra],
        out_specs=[vt_spec if w is None else (stk(w) if stacked(st) else tok(w)) for _, w, _, st in _INPROJ_OUTPUTS],
        compiler_params=_params("parallel", "parallel"),
        name="in_proj",
    )(x, mod, g1, win, gq, wuq, gkv, wuk, wvt, cm, sm, cr, sr, *extra)
    return dict(zip([o[0] for o in _INPROJ_OUTPUTS], res))


def _kvup_kernel(lat_ref, wuk_ref, wvt_ref, kn_ref, vt_ref):
    lat_b = lat_ref[0].astype(BF16)
    kn_ref[0] = _dot(lat_b, wuk_ref[...]).astype(kn_ref.dtype)
    tkk = vt_ref.shape[-1]
    for kk in range(vt_ref.shape[1]):
        vt_ref[0, kk] = _dot_nt(wvt_ref[...], lat_b[kk * tkk:(kk + 1) * tkk]).astype(vt_ref.dtype)


def _kvup(lat, layer, wuk, wvt):
    _, b, lk, r = lat.shape
    nv = MLA_HEADS * MLA_V
    blk = lambda w: pl.BlockSpec((1, lk, w), lambda i: (i, 0, 0))
    return pl.pallas_call(
        _kvup_kernel,
        out_shape=(jax.ShapeDtypeStruct((b, lk, MLA_NQ), BF16),
                   jax.ShapeDtypeStruct((b, lk // ATT_TILE, nv, ATT_TILE), BF16)),
        grid=(b,),
        in_specs=[pl.BlockSpec((None, 1, lk, r), lambda i: (layer, i, 0, 0)), _const_spec(wuk.shape),
                  _const_spec(wvt.shape)],
        out_specs=[blk(MLA_NQ), pl.BlockSpec((1, lk // ATT_TILE, nv, ATT_TILE), lambda i: (i, 0, 0, 0))],
        compiler_params=_params("parallel"),
        name="kv_up",
    )(lat, wuk, wvt)


def _fori_pairs(n, body, init):
    carry = lax.fori_loop(0, n // 2, lambda i, c: body(2 * i + 1, body(2 * i, c)), init)
    return lax.fori_loop(0, n % 2, lambda i, c: body(n - 1, c), carry)


def _fold8(reduce, x):
    return reduce(x.reshape(x.shape[0] // 8, 8, x.shape[1]), axis=0)


def _mla_kernel(qn_ref, qr_ref, knp_ref, krp_ref, vtp_ref, knd_ref, krd_ref, vtd_ref, o_ref, s_scr, q_scr, mx_scr, m_scr, l_scr, acc_scr,
                *, tq, tk, off, fused, hp):
    qi = pl.program_id(1)
    nfull = (off + qi * tq) // tk
    nslot = 4 // hp
    wl = hp * tq
    lane = lax.broadcasted_iota(jnp.int32, (1, LANES), 1)
    key_c = lax.broadcasted_iota(jnp.int32, (tq, 1), 0) // CHUNK
    qry_c = (lax.broadcasted_iota(jnp.int32, (1, wl), 1) % tq) // CHUNK
    dmask = key_c <= qry_c
    neg = -1e30
    lane_v = lax.broadcasted_iota(jnp.int32, (1, HEAD_W), 1)
    feat = lax.broadcasted_iota(jnp.int32, (MLA_V, 1), 0)

    def build_queries(g):
        qr = qr_ref[0, :, LANES * g:LANES * (g + 1)]
        for hh in range(4):
            qn = qn_ref[0, :, LANES * (2 * g + hh // 2):LANES * (2 * g + hh // 2 + 1)]
            rows = slice((hh % hp) * tq, (hh % hp + 1) * tq)
            q_scr[g, hh // hp, rows, 0:LANES] = jnp.where((lane >> 6) == hh % 2, qn, jnp.zeros_like(qn))
            q_scr[g, hh // hp, rows, LANES:HEAD_W] = jnp.where((lane >> 5) == hh, qr, jnp.zeros_like(qr))

    def slot_scores(g, kn_ref, kr_ref, ks, u):
        p = 2 * g + (u * hp) // 2
        keys = jnp.concatenate([kn_ref[0, ks, LANES * p:LANES * (p + 1)], kr_ref[0, ks, :]], axis=-1)
        return _dot_nt(keys, q_scr[g, u])

    def vt_rows(g, u):
        return slice(MLA_V * (4 * g + u * hp), MLA_V * (4 * g + (u + 1) * hp))

    def scores_tile(g, j):
        ks = pl.ds(pl.multiple_of(j * tk, tk), tk)
        for u in range(nslot):
            s = slot_scores(g, knp_ref, krp_ref, ks, u)
            s_scr[g, j, u] = s
            mx_scr[g, u] = jnp.maximum(mx_scr[g, u], _fold8(jnp.max, s))

    def scores_own(g):
        own = []
        for u in range(nslot):
            if fused:
                ks_d = pl.ds(pl.multiple_of(nfull * tk, tk), tk)
                s_d = jnp.where(dmask, slot_scores(g, knp_ref, krp_ref, ks_d, u), neg)
                s_scr[g, nfull, u] = s_d
            else:
                s_d = jnp.where(dmask, slot_scores(g, knd_ref, krd_ref, slice(None), u), neg)
                own.append(s_d)
            mx_scr[g, u] = _fold8(jnp.max, s_d)
        return own

    def values_tile(g, j):
        for u in range(nslot):
            p = jnp.exp2(s_scr[g, j, u] - m_scr[g, u, 0:1, :])
            l_scr[g, u] += _fold8(jnp.sum, p)
            acc_scr[g, u] += _dot(vtp_ref[0, j, vt_rows(g, u), :], p.astype(BF16))

    def values_start(g, own):
        for u in range(nslot):
            m_scr[g, u] = jnp.broadcast_to(jnp.max(mx_scr[g, u], axis=0, keepdims=True), (8, wl))
            if fused:
                l_scr[g, u] = jnp.zeros((8, wl), F32)
                acc_scr[g, u] = jnp.zeros((hp * MLA_V, wl), F32)
            else:
                p = jnp.exp2(own[u] - m_scr[g, u, 0:1, :])
                l_scr[g, u] = _fold8(jnp.sum, p)
                acc_scr[g, u] = _dot(vtd_ref[0, 0, vt_rows(g, u), :], p.astype(BF16))

    def emit(g):
        out_g = jnp.zeros((tq, HEAD_W), F32)
        for hh in range(4):
            u, a = hh // hp, hh % hp
            inv = 1.0 / jnp.sum(l_scr[g, u], axis=0, keepdims=True)
            place = jnp.where(feat + MLA_V * hh == lane_v, 1.0, 0.0).astype(BF16)
            placed = _dot_tn((acc_scr[g, u, MLA_V * a:MLA_V * (a + 1), :] * inv).astype(BF16), place)
            out_g = out_g + placed[a * tq:(a + 1) * tq]
        o_ref[0, :, HEAD_W * g:HEAD_W * (g + 1)] = out_g.astype(o_ref.dtype)

    def loop(n, *works):
        def body(j, carry):
            for w in works:
                w(j)
            return carry
        _fori_pairs(n, body, 0)

    build_queries(0)
    build_queries(1)
    own0 = scores_own(0)
    loop(nfull, lambda j: scores_tile(0, j))
    values_start(0, own0)
    own1 = scores_own(1)
    loop(nfull, lambda j: scores_tile(1, j), lambda j: values_tile(0, j))
    if fused:
        values_tile(0, nfull)
    values_start(1, own1)
    emit(0)
    loop(nfull + 1 if fused else nfull, lambda j: values_tile(1, j))
    emit(1)


def _mla(qn, qr, past, diag, off):
    b, l, _ = qn.shape
    lp = past[0].shape[1]
    tq = min(l, ATT_TILE)
    tk = ATT_TILE
    nv = MLA_NV
    fused = off == 0 and tq == tk
    qblk = lambda w: pl.BlockSpec((1, tq, w), lambda i, t: (i, t, 0))
    nmax = (off + l - tq) // tk + (1 if fused else 0)
    hp = 1 if tq % 128 == 0 else 2
    wl = hp * tq
    kern = functools.partial(_mla_kernel, tq=tq, tk=tk, off=off, fused=fused, hp=hp)
    return pl.pallas_call(
        kern,
        out_shape=jax.ShapeDtypeStruct((b, l, nv), BF16),
        grid=(b, l // tq),
        in_specs=[qblk(MLA_NQ), qblk(MLA_HEADS * MLA_ROPE), pl.BlockSpec((1, lp, MLA_NQ), lambda i, t: (i, 0, 0)),
                  pl.BlockSpec((1, lp, LANES), lambda i, t: (i, 0, 0)),
                  pl.BlockSpec((1, lp // tk, nv, tk), lambda i, t: (i, 0, 0, 0)), qblk(MLA_NQ), qblk(LANES),
                  pl.BlockSpec((1, 1, nv, tq), lambda i, t: (i, t, 0, 0))],
        out_specs=qblk(nv),
        scratch_shapes=[pltpu.VMEM((2, nmax, 4 // hp, tk, wl), F32), pltpu.VMEM((2, 4 // hp, wl, HEAD_W), BF16),
                        pltpu.VMEM((2, 4 // hp, 8, wl), F32), pltpu.VMEM((2, 4 // hp, 8, wl), F32),
                        pltpu.VMEM((2, 4 // hp, 8, wl), F32), pltpu.VMEM((2, 4 // hp, hp * MLA_V, wl), F32)],
        compiler_params=_params("parallel", "parallel"),
        name="mla_attn",
    )(qn, qr, *past, *diag)


def _neg_suffix_matrix(n):
    j = lax.broadcasted_iota(jnp.int32, (n, n), 0)
    s = lax.broadcasted_iota(jnp.int32, (n, n), 1)
    return jnp.where(j >= s, -1.0, 0.0).astype(BF16)


def _retention_chunk(rq_ref, rk_ref, rv_ref, rg_ref, gn_ref, dec_ref, qd_ref, kdec_ref, sd_ref, ro_ref, s_ref):
    c = rq_ref.shape[1]
    lane = lax.broadcasted_iota(jnp.int32, (1, HEAD_W), 1)
    krow = lax.broadcasted_iota(jnp.int32, (HEAD_W, 1), 0)
    bd_mask = ((krow & 127) >> 5) == (lane >> 6)
    avg = jnp.where((krow >> 6) == (lane >> 6), 1.0 / RET_DV, 0.0).astype(BF16)
    q = rq_ref[0]
    kf = rk_ref[0]
    k = kf.astype(BF16)
    v = rv_ref[0]
    state = s_ref[0]
    o = _dot(q, state.astype(BF16)) * qd_ref[...]
    qs = jnp.concatenate([jnp.where(((lane & 127) >> 5) == h, q, jnp.zeros_like(q)) for h in range(RET_HEADS)],
                         axis=0)
    pv = _dot((_dot_nt(qs, k) * dec_ref[...]).astype(BF16), v)
    for h in range(RET_HEADS):
        o = o + jnp.where((lane >> 6) == h, pv[h * c:(h + 1) * c], 0.0)
    upd = _dot_tn((kf * kdec_ref[...]).astype(BF16), v)
    s_ref[0] = sd_ref[...] * state + jnp.where(bd_mask, upd, 0.0)
    dlt = o - _group_mean(o, avg)
    ro = dlt * lax.rsqrt(_group_mean(dlt * dlt, avg) + EPS) * gn_ref[...]
    ro_ref[0] = (_silu(rg_ref[0]) * ro).astype(ro_ref.dtype)


def _sb_kernel(q_ref, kp_ref, vp_ref, kd_ref, vd_ref, rq_ref, rk_ref, rv_ref, rg_ref, s0_ref, gn_ref, dec_ref, qd_ref,
               kdec_ref, sd_ref, o_ref, ro_ref, s_ref, kb_ref, vb_ref, q_scr, run_scr, acc_scr, *, tq, tk, off):
    qi = pl.program_id(1)

    @pl.when(qi == 0)
    def _():
        kb_ref[...] = kp_ref[0].astype(BF16)
        vb_ref[...] = vp_ref[0].astype(BF16)
        s_ref[0] = s0_ref[0]

    def retention():
        _retention_chunk(rq_ref, rk_ref, rv_ref, rg_ref, gn_ref, dec_ref, qd_ref, kdec_ref, sd_ref, ro_ref, s_ref)

    nfull = (off + qi * tq) // tk
    lane = lax.broadcasted_iota(jnp.int32, (1, HEAD_W), 1)
    nh = SB_HEADS
    dmask = (lax.broadcasted_iota(jnp.int32, (1, tq), 1)
             < lax.broadcasted_iota(jnp.int32, (nh * tq, 1), 0) % tq)
    t_diag = _neg_suffix_matrix(tq)
    t_full = t_diag if tk == tq else _neg_suffix_matrix(tk)
    q = q_ref[0]

    def exponent(k, mask, tmat):
        zz = _dot_nt(q_scr[...], k)
        sp = jnp.maximum(zz, 0.0) + jnp.log(1.0 + jnp.exp(-jnp.abs(zz)))
        if mask is not None:
            sp = jnp.where(mask, sp, 0.0)
        hi = sp.astype(BF16)
        lo = (sp - hi.astype(F32)).astype(BF16)
        return zz + _dot(hi, tmat) + _dot(lo, tmat), jnp.sum(sp, axis=-1, keepdims=True)

    def carry_on(run):
        run_scr[...] = jnp.broadcast_to(run, (nh * tq, LANES))
        return jnp.max(run)

    def own_weights():
        e, rs = exponent(kd_ref[0].astype(BF16), dmask, t_diag)
        return jnp.where(dmask, jnp.exp(e), 0.0).astype(BF16), rs

    def own_only():
        retention()
        w, rs = own_weights()
        acc_scr[...] = _dot(w, vd_ref[0].astype(BF16))
        return carry_on(-rs)

    def own_and_previous():
        retention()
        ks = pl.ds(pl.multiple_of((nfull - 1) * tk, tk), tk)
        w, rs = own_weights()
        e1, rs1 = exponent(kb_ref[ks, :], None, t_full)
        w1 = jnp.exp(e1 - rs).astype(BF16)
        acc_scr[...] = _dot(w, vd_ref[0].astype(BF16)) + _dot(w1, vb_ref[ks, :])
        return carry_on(-(rs + rs1))

    for h in range(nh):
        q_scr[h * tq:(h + 1) * tq, :] = jnp.where((lane >> 6) == h, q, jnp.zeros_like(q))
    top = lax.cond(nfull >= 1, own_and_previous, own_only)

    def more(state):
        j, top = state
        return jnp.logical_and(j >= 0, top > SB_LOG_ZERO)

    def body(state):
        j, _ = state
        ks = pl.ds(pl.multiple_of(j * tk, tk), tk)
        e, rs = exponent(kb_ref[ks, :], None, t_full)
        run = run_scr[...]
        w = jnp.exp(e + jnp.concatenate([run] * (tk // LANES), -1))
        acc_scr[...] += _dot(w.astype(BF16), vb_ref[ks, :])
        run = run - rs
        run_scr[...] = run
        return j - 1, jnp.max(run)

    lax.while_loop(more, body, (nfull - 2, top))
    out = jnp.zeros((tq, HEAD_W), F32)
    for h in range(nh):
        out = jnp.where((lane >> 6) == h, acc_scr[h * tq:(h + 1) * tq, :], out)
    o_ref[0] = out.astype(o_ref.dtype)


def _sb_ret(sq, past, diag, off, rq, rk, rv, rg, s0, gn):
    b, l, _ = sq.shape
    lp = past[0].shape[2]
    tq = min(l, ATT_TILE)
    dec, qd, kd, sd = _ret_consts(tq)
    dec = dec.reshape(RET_HEADS * tq, tq)
    qblk = pl.BlockSpec((1, tq, HEAD_W), lambda i, t: (i, t, 0))
    dblk = pl.BlockSpec((None, 1, tq, HEAD_W), lambda i, t: (diag[2], i, t, 0))
    kblk = pl.BlockSpec((None, 1, lp, HEAD_W), lambda i, t: (past[2], i, 0, 0))
    sblk = pl.BlockSpec((1, HEAD_W, HEAD_W), lambda i, t: (i, 0, 0))
    kern = functools.partial(_sb_kernel, tq=tq, tk=ATT_TILE, off=off)
    return pl.pallas_call(
        kern,
        out_shape=[jax.ShapeDtypeStruct((b, l, HEAD_W), BF16), jax.ShapeDtypeStruct((b, l, HEAD_W), BF16),
                   jax.ShapeDtypeStruct((b, HEAD_W, HEAD_W), F32)],
        grid=(b, l // tq),
        in_specs=[qblk, kblk, kblk, dblk, dblk, qblk, qblk, qblk, qblk, sblk, _const_spec(gn.shape),
                  _const_spec(dec.shape), _const_spec(qd.shape), _const_spec(kd.shape), _const_spec(sd.shape)],
        out_specs=[qblk, qblk, sblk],
        scratch_shapes=[pltpu.VMEM((lp, HEAD_W), BF16), pltpu.VMEM((lp, HEAD_W), BF16),
                        pltpu.VMEM((SB_HEADS * tq, HEAD_W), BF16), pltpu.VMEM((SB_HEADS * tq, LANES), F32),
                        pltpu.VMEM((SB_HEADS * tq, HEAD_W), F32)],
        compiler_params=_params("parallel", "arbitrary"),
        name="sb_ret",
    )(sq, *past[:2], *diag[:2], rq, rk, rv, rg, s0, gn, dec, qd, kd, sd)


def _group_mean(x, avg):
    hi = x.astype(BF16)
    lo = (x - hi.astype(F32)).astype(BF16)
    return _dot(hi, avg) + _dot(lo, avg)


def _ret_consts(c):
    lg = jnp.log(1.0 - 2.0 ** (-5.0 - jnp.arange(RET_HEADS, dtype=F32)))
    i = jnp.arange(c, dtype=F32)
    rel = i[:, None] - i[None, :]
    dec = jnp.where(rel >= 0, jnp.exp(lg[:, None, None] * jnp.maximum(rel, 0.0)), 0.0)
    v_head = jnp.arange(HEAD_W) // RET_DV
    k_head = (jnp.arange(HEAD_W) % 128) // (RET_DK // 2)
    qd = jnp.exp(lg[None, v_head] * (i[:, None] + 1.0))
    kd = jnp.exp(lg[None, k_head] * (c - 1.0 - i[:, None]))
    sd = jnp.exp(lg * c)[v_head][None, :]
    return dec, qd, kd, sd


def _ffn_kernel(mla_ref, ret_ref, sb_ref, x_ref, mod_ref, prev_ref, wo_ref, g2_ref, wup_ref, cw_ref, cb_ref,
                wdn_ref, gf_ref, o_ref, st_ref, a_scr, carry, *, final_norm):
    bb, tl, d = x_ref.shape
    rows = bb * tl
    t = pl.program_id(1)
    nt = pl.num_programs(1)

    @pl.when(t == 0)
    def _():
        carry[:, 6:8, :] = prev_ref[...]

    mod = mod_ref[...]
    cat = jnp.concatenate([mla_ref[...].reshape(rows, MLA_NV), ret_ref[...].reshape(rows, HEAD_W),
                           sb_ref[...].reshape(rows, HEAD_W)], axis=-1)
    x1 = x_ref[...] + mod[:, 2:3, :] * _dot(cat, wo_ref[...]).reshape(bb, tl, d)
    h = (_rms(x1) * g2_ref[...] * (1.0 + mod[:, 4:5, :]) + mod[:, 3:4, :]).reshape(rows, d).astype(BF16)
    f = jnp.zeros((rows, d), F32)
    for c0, c1 in FF_CHUNKS:
        cs, n = slice(c0, c1), c1 - c0
        a = _dot(h, wup_ref[:, cs]).reshape(bb, tl, n)
        b = _dot(h, wup_ref[:, D_FF + c0:D_FF + c1])
        a_scr[:, 8:, :n] = a
        a_scr[:, 6:8, :n] = carry[:, 6:8, cs]
        cw = cw_ref[...]
        conv = (cb_ref[:, cs] + cw[0:1, cs] * a_scr[:, 6:6 + tl, :n] + cw[1:2, cs] * a_scr[:, 7:7 + tl, :n]
                + cw[2:3, cs] * a)
        carry[:, 6:8, cs] = a_scr[:, tl + 6:tl + 8, :n]
        y = (_silu(conv).reshape(rows, n) * b).astype(BF16)
        f = f + _dot(y, wdn_ref[cs, :])
    x2 = x1 + mod[:, 5:6, :] * f.reshape(bb, tl, d)
    if final_norm:
        x2 = _rms(x2) * gf_ref[...]
    o_ref[...] = x2

    @pl.when(t == nt - 1)
    def _():
        st_ref[...] = carry[:, 6:8, :]


def _ffn(mla, ret, sb, x, mod, prev, wo, g2, wup, cw, cb, wdn, gf, final_norm):
    b, l, d = x.shape
    tl = min(l, ROW_TILE)
    bb = ROW_TILE // tl
    tok = lambda w: pl.BlockSpec((bb, tl, w), lambda i, t: (i, t, 0))
    st = pl.BlockSpec((bb, CONV_W - 1, D_FF), lambda i, t: (i, 0, 0))
    once = lambda a: pl.BlockSpec(a.shape, lambda i, t: (0, 0), pipeline_mode=pl.Buffered(1))
    return pl.pallas_call(
        functools.partial(_ffn_kernel, final_norm=final_norm),
        out_shape=[jax.ShapeDtypeStruct((b, l, d), F32), jax.ShapeDtypeStruct((b, CONV_W - 1, D_FF), F32)],
        grid=(b // bb, l // tl),
        in_specs=[tok(MLA_NV), tok(HEAD_W), tok(HEAD_W), tok(d), pl.BlockSpec((bb, 6, d), lambda i, t: (i, 0, 0)), st,
                  once(wo), _const_spec(g2.shape), once(wup), _const_spec(cw.shape), _const_spec(cb.shape),
                  once(wdn), _const_spec(gf.shape)],
        out_specs=[tok(d), st],
        scratch_shapes=[pltpu.VMEM((bb, tl + 8, max(c1 - c0 for c0, c1 in FF_CHUNKS)), F32),
                        pltpu.VMEM((bb, 8, D_FF), F32)],
        compiler_params=_params("parallel", "arbitrary"),
        name="conv_ffn",
    )(mla, ret, sb, x, mod, prev, wo, g2, wup, cw, cb, wdn, gf)


def _half_split(w, heads):
    lead = w.shape[:-1]
    dim = w.shape[-1] // heads
    w = w.reshape(lead + (heads, 2, dim // 2))
    return jnp.swapaxes(w, -3, -2).reshape(lead + (heads * dim,))


def _permute_win(w):
    half = MLA_ROPE // 2
    kr = w[..., 640:672]
    kr_swapped = jnp.concatenate([kr[..., half:], kr[..., :half]], axis=-1)
    tiled = lambda a: jnp.tile(a, (1, 1, 4))
    return jnp.concatenate([w[..., 0:640], _half_split(w[..., 672:928], RET_HEADS),
                            _half_split(w[..., 928:1184], RET_HEADS), w[..., 1184:2464],
                            tiled(kr), tiled(kr_swapped)], axis=-1)


def _permute_wuq(w):
    half = MLA_ROPE // 2
    w = w.reshape(w.shape[:-1] + (MLA_HEADS, MLA_NOPE + MLA_ROPE))
    flat = lambda a: a.reshape(a.shape[:-2] + (-1,))
    rope = w[..., MLA_NOPE:]
    swapped = jnp.concatenate([rope[..., half:], rope[..., :half]], axis=-1)
    return jnp.concatenate([flat(w[..., :MLA_NOPE]), flat(rope), flat(swapped)], axis=-1)


def _permute_wukv(w):
    w = w.reshape(w.shape[:-1] + (MLA_HEADS, MLA_NOPE + MLA_V))
    flat = lambda a: a.reshape(a.shape[:-2] + (-1,))
    return jnp.concatenate([flat(w[..., :MLA_NOPE]), flat(w[..., MLA_NOPE:])], axis=-1)


def _ret_state_in(st):
    b = st.shape[0]
    st = st.reshape(b, RET_HEADS, 2, RET_DK // 2, RET_DV).transpose(0, 2, 1, 3, 4)
    eye = jnp.eye(RET_HEADS, dtype=st.dtype)
    full = st[:, :, :, :, None, :] * eye[None, None, :, None, :, None]
    return full.reshape(b, HEAD_W, HEAD_W)


def _ret_state_out(s):
    b = s.shape[0]
    s = s.reshape(b, 2, RET_HEADS, RET_DK // 2, RET_HEADS, RET_DV)
    blocks = [s[:, :, h, :, h, :].reshape(b, RET_DK, RET_DV) for h in range(RET_HEADS)]
    return jnp.stack(blocks, axis=1)


def _rope_angles(pos, dim):
    inv = ROPE_BASE ** (-jnp.arange(0, dim, 2, dtype=F32) / dim)
    return pos.astype(F32)[:, None] * inv[None, :]


def _rope_tables(pos, dim, reps):
    ang = _rope_angles(pos, dim)
    return jnp.tile(jnp.cos(ang), (1, reps)), jnp.tile(jnp.sin(ang), (1, reps))


def _rope_tables_swapped(pos, dim, reps):
    ang = _rope_angles(pos, dim)
    c, s = jnp.cos(ang), jnp.sin(ang)
    return jnp.tile(jnp.concatenate([c, c], axis=1), (1, reps)), jnp.tile(jnp.concatenate([-s, s], axis=1), (1, reps))


def _trunk(x, mods, pos0, cache, w):
    b, l, _ = x.shape
    depth = len(mods)
    pos = pos0 + jnp.arange(l)
    cm, sm = _rope_tables_swapped(pos, MLA_ROPE, MLA_HEADS)
    cr, sr = _rope_tables(pos, RET_DK, RET_HEADS)
    if cache is None:
        off = 0
    else:
        off = cache["lat"].shape[2]
        krt_cache = jnp.tile(cache["kr"], (1, 1, 1, 4)).astype(BF16)
        sb_cache = (cache["sk"].reshape(depth, b, off, HEAD_W), cache["sv"].reshape(depth, b, off, HEAD_W))
    earlier = []
    small = [[], []]
    for layer in range(depth):
        mod = mods[layer]
        last = layer == depth - 1
        p = _inproj(x, mod, w["g_norm1"][layer], w["w_in"][layer], w["g_q_norm"][layer], w["w_uq"][layer],
                    w["g_kv_norm"][layer], w["w_uk"][layer], w["w_vt"][layer], cm, sm, cr, sr,
                    earlier if last else None)
        if last:
            sb_new = (p["sk"], p["sv"], layer)
        else:
            earlier.append((p["lat"], p["kr"], p["sk"], p["sv"]))
            sb_new = (p["sk"][None], p["sv"][None], 0)
        if cache is None:
            mla_past, sb_past = (p["kn"], p["krt"], p["vt"]), sb_new
            s0 = jnp.zeros((b, HEAD_W, HEAD_W), F32)
            prev = jnp.zeros((b, CONV_W - 1, D_FF), F32)
        else:
            kn_c, vt_c = _kvup(cache["lat"], layer, w["w_uk"][layer], w["w_vt"][layer])
            mla_past = (kn_c, krt_cache[layer], vt_c)
            sb_past = sb_cache + (layer,)
            s0 = _ret_state_in(cache["S"][layer])
            prev = cache["conv"][layer]
        mla = _mla(p["qn"], p["qr"], mla_past, (p["kn"], p["krt"], p["vt"]), off)
        sb, ret, s_new = _sb_ret(p["sq"], sb_past, sb_new, off, p["rq"], p["rk"], p["rv"], p["rg"], s0,
                                 w["g_ret_norm"][layer])
        x, conv_state = _ffn(mla, ret, sb, x, mod, prev, w["w_o"][layer], w["g_norm2"][layer], w["w_up"][layer],
                             w["conv_w"][layer], w["conv_b"][layer], w["w_down"][layer], w["g_final"],
                             layer == depth - 1)
        small[0].append(_ret_state_out(s_new))
        small[1].append(conv_state)
    lat, kr, sk, sv = p["lat"], p["kr"], p["sk"], p["sv"]
    return x, [lat, kr, sk.reshape(depth, b, l, SB_HEADS, SB_DK), sv.reshape(depth, b, l, SB_HEADS, SB_DV),
               jnp.stack(small[0], axis=0), jnp.stack(small[1], axis=0)]


def kernel(x_prompt, x_sample, c_prompt, c_sample, cache_mla_latent, cache_mla_krope, cache_sb_k, cache_sb_v, state_ret, state_ffn_conv, w_in, g_q_norm, w_uq, g_kv_norm, w_ukv, g_ret_norm, w_o, w_up, conv_w, conv_b, w_down, g_norm1, g_norm2, w_ada, b_ada, g_final):
    depth = w_in.shape[0]
    bp = x_prompt.shape[0]
    row = lambda g: g.reshape(g.shape[0], 1, g.shape[-1])
    w = dict(
        w_in=_permute_win(w_in).astype(BF16),
        w_uq=_permute_wuq(w_uq).astype(BF16),
        w_uk=_permute_wukv(w_ukv)[..., :MLA_HEADS * MLA_NOPE].astype(BF16),
        w_vt=jnp.swapaxes(_permute_wukv(w_ukv)[..., MLA_HEADS * MLA_NOPE:], -1, -2).astype(BF16),
        w_o=w_o.astype(BF16), w_up=w_up.astype(BF16), w_down=w_down.astype(BF16),
        g_q_norm=row(g_q_norm), g_kv_norm=row(g_kv_norm), g_ret_norm=row(g_ret_norm),
        g_norm1=row(g_norm1), g_norm2=row(g_norm2), conv_w=conv_w, conv_b=row(conv_b),
        g_final=g_final.reshape(1, -1))
    mod = _ada(jnp.concatenate([c_prompt, c_sample], axis=0), w_ada, b_ada)
    mod = mod.reshape(depth, mod.shape[1], 6, D_MODEL)
    y_p, st_p = _trunk(x_prompt, [mod[l, :bp] for l in range(depth)], 0, None, w)
    cache = dict(lat=cache_mla_latent, kr=cache_mla_krope, sk=cache_sb_k, sv=cache_sb_v, S=state_ret,
                 conv=state_ffn_conv)
    y_s, st_s = _trunk(x_sample, [mod[l, bp:] for l in range(depth)], cache_mla_latent.shape[2], cache, w)
    return (y_p, y_s, *st_p, *st_s)
```

```python
import functools

import jax
import jax.numpy as jnp
from jax import lax
from jax.experimental import pallas as pl
from jax.experimental.pallas import tpu as pltpu

D_MODEL = 1024
CHUNK = 64
MLA_HEADS = 8
MLA_NOPE = 64
MLA_ROPE = 32
MLA_V = 64
MLA_Q_RANK = 384
MLA_KV_RANK = 256
RET_HEADS = 4
RET_DK = 64
RET_DV = 64
SB_HEADS = 4
SB_DK = 64
SB_DV = 64
D_FF = 2816
CONV_W = 3
ROPE_BASE = 10000.0
EPS = 1e-6

LANES = 128
HEAD_W = 256
MLA_NQ = MLA_HEADS * MLA_NOPE
MLA_NV = MLA_HEADS * MLA_V
ROW_TILE = 512
ATT_TILE = 256
FF_CHUNKS = ((0, 1536), (1536, D_FF))
MLA_EXP2_SCALE = (MLA_NOPE + MLA_ROPE) ** -0.5 * 1.4426950408889634
SB_LOG_ZERO = -104.0
VMEM_LIMIT = 56 * 1024 * 1024

F32 = jnp.float32
BF16 = jnp.bfloat16

_ZQ, _ZKV, _RQ, _RK, _RV, _RG, _SQ, _SK, _SV, _KRT, _IN_COLS_P = (
    0, 384, 640, 896, 1152, 1408, 1664, 1920, 2176, 2432, 2688)


def _dot(a, b):
    return jnp.dot(a, b, preferred_element_type=F32)


def _dot_nt(a, b):
    return lax.dot_general(a, b, (((1,), (1,)), ((), ())), preferred_element_type=F32)


def _dot_tn(a, b):
    return lax.dot_general(a, b, (((0,), (0,)), ((), ())), preferred_element_type=F32)


def _rms(x):
    return x * lax.rsqrt(jnp.mean(x * x, axis=-1, keepdims=True) + EPS)


def _silu(x):
    return x / (1.0 + jnp.exp(-x))


def _params(*sem):
    return pltpu.CompilerParams(dimension_semantics=sem, vmem_limit_bytes=VMEM_LIMIT)


def _const_spec(shape):
    nd = len(shape)
    return pl.BlockSpec(shape, lambda *_: (0,) * nd)


def _ada_kernel(c_ref, w_ref, b_ref, o_ref):
    a = _silu(c_ref[...]).astype(BF16)
    o_ref[0] = _dot(a, w_ref[0].astype(BF16)) + b_ref[0]


def _ada(c_all, w_ada, b_ada):
    depth, d, n = w_ada.shape
    rows = c_all.shape[0]
    tn = 1536
    return pl.pallas_call(
        _ada_kernel,
        out_shape=jax.ShapeDtypeStruct((depth, rows, n), F32),
        grid=(depth, n // tn),
        in_specs=[pl.BlockSpec((rows, d), lambda l, j: (0, 0)),
                  pl.BlockSpec((1, d, tn), lambda l, j: (l, 0, j)),
                  pl.BlockSpec((1, 1, tn), lambda l, j: (l, 0, j))],
        out_specs=pl.BlockSpec((1, rows, tn), lambda l, j: (l, 0, j)),
        compiler_params=_params("parallel", "parallel"),
        name="ada_mod",
    )(c_all, w_ada, b_ada.reshape(depth, 1, n))


_INPROJ_INPUTS = 13
_INPROJ_OUTPUTS = (("qn", MLA_NQ, BF16, False), ("qr", MLA_HEADS * MLA_ROPE, BF16, False), ("lat", MLA_KV_RANK, F32, True),
                   ("kn", MLA_NQ, BF16, False), ("krt", LANES, BF16, False), ("vt", None, BF16, False), ("kr", MLA_ROPE, F32, True),
                   ("rq", HEAD_W, BF16, False), ("rk", HEAD_W, F32, False), ("rv", HEAD_W, BF16, False),
                   ("rg", HEAD_W, F32, False), ("sq", HEAD_W, BF16, False), ("sk", HEAD_W, F32, True),
                   ("sv", HEAD_W, F32, True))


def _inproj_kernel(*refs):
    (x_ref, mod_ref, g1_ref, win_ref, gq_ref, wuq_ref, gkv_ref, wuk_ref, wvt_ref, cm_ref, sm_ref, cr_ref,
     sr_ref) = refs[:_INPROJ_INPUTS]
    names = [o[0] for o in _INPROJ_OUTPUTS]
    out = dict(zip(names, refs[len(refs) - len(names):]))
    stacked = [o[0] for o in _INPROJ_OUTPUTS if o[3]]
    earlier = refs[_INPROJ_INPUTS:len(refs) - len(names)]
    layer = len(earlier) // len(stacked)
    for r in range(layer):
        for k, name in enumerate(stacked):
            out[name][r] = earlier[r * len(stacked) + k][...]
    bb, tl, d = x_ref.shape
    rows = bb * tl
    mod = mod_ref[...]
    h = _rms(x_ref[...]) * g1_ref[...] * (1.0 + mod[:, 1:2, :]) + mod[:, 0:1, :]
    z = _dot(h.reshape(rows, d).astype(BF16), win_ref[...])

    def put(name, val, lanes=slice(None)):
        ref = out[name]
        lead = (layer,) if len(ref.shape) == 4 else ()
        ref[lead + (slice(None), slice(None), lanes)] = val.reshape(bb, tl, val.shape[-1]).astype(ref.dtype)

    def rope(x1, x2, c_ref, s_ref):
        c, s = c_ref[...][None], s_ref[...][None]
        x1 = x1.reshape(bb, tl, x1.shape[-1])
        x2 = x2.reshape(bb, tl, x2.shape[-1])
        return x1 * c - x2 * s, x1 * s + x2 * c

    def rope_swapped(a, b, width):
        a = a.reshape(bb, tl, width)
        b = b.reshape(bb, tl, width)
        return a * cm_ref[:, :width][None] + b * sm_ref[:, :width][None]

    q = _dot((_rms(z[:, _ZQ:_ZKV]) * gq_ref[...]).astype(BF16), wuq_ref[...])
    nr = MLA_HEADS * MLA_ROPE
    put("qn", q[:, :MLA_NQ] * MLA_EXP2_SCALE)
    put("qr", rope_swapped(q[:, MLA_NQ:MLA_NQ + nr], q[:, MLA_NQ + nr:], nr) * MLA_EXP2_SCALE)
    lat = _rms(z[:, _ZKV:_RQ]) * gkv_ref[...]
    put("lat", lat)
    kr = rope_swapped(z[:, _KRT:_KRT + LANES], z[:, _KRT + LANES:_KRT + 2 * LANES], LANES)
    put("kr", kr[:, :, :MLA_ROPE])
    lat_b = lat.astype(BF16)
    put("kn", _dot(lat_b, wuk_ref[...]))
    put("krt", kr)
    vt_ref = out["vt"]
    tkk = vt_ref.shape[-1]
    for bi in range(bb):
        for kk in range(tl // tkk):
            r0 = bi * tl + kk * tkk
            vt_ref[bi, kk] = _dot_nt(wvt_ref[...], lat_b[r0:r0 + tkk]).astype(vt_ref.dtype)
    a1, a2 = rope(z[:, _RQ:_RQ + LANES], z[:, _RQ + LANES:_RK], cr_ref, sr_ref)
    put("rq", a1 * (RET_DK ** -0.5), slice(0, LANES))
    put("rq", a2 * (RET_DK ** -0.5), slice(LANES, HEAD_W))
    b1, b2 = rope(z[:, _RK:_RK + LANES], z[:, _RK + LANES:_RV], cr_ref, sr_ref)
    put("rk", b1, slice(0, LANES))
    put("rk", b2, slice(LANES, HEAD_W))
    put("rv", z[:, _RV:_RG])
    put("rg", z[:, _RG:_SQ])
    put("sq", z[:, _SQ:_SK] * (SB_DK ** -0.5))
    put("sk", z[:, _SK:_SV])
    put("sv", z[:, _SV:_KRT])


def _inproj(x, mod, g1, win, gq, wuq, gkv, wuk, wvt, cm, sm, cr, sr, earlier):
    b, l, d = x.shape
    tl = min(l, ROW_TILE)
    bb = ROW_TILE // tl
    grid = (b // bb, l // tl)
    depth = None if earlier is None else len(earlier) + 1
    tok = lambda w: pl.BlockSpec((bb, tl, w), lambda i, t: (i, t, 0))
    stk = lambda w: pl.BlockSpec((depth, bb, tl, w), lambda i, t: (0, i, t, 0))
    pos = pl.BlockSpec((tl, LANES), lambda i, t: (t, 0))
    pos2 = pl.BlockSpec((tl, 2 * LANES), lambda i, t: (t, 0))
    stacked = lambda st: st and depth is not None
    extra = [] if earlier is None else [a for row in earlier for a in row]
    tkk = min(l, ATT_TILE)
    vt_shape = jax.ShapeDtypeStruct((b, l // tkk, MLA_HEADS * MLA_V, tkk), BF16)
    vt_spec = pl.BlockSpec((bb, tl // tkk, MLA_HEADS * MLA_V, tkk), lambda i, t: (i, t, 0, 0))
    res = pl.pallas_call(
        _inproj_kernel,
        out_shape=[vt_shape if w is None else jax.ShapeDtypeStruct((depth, b, l, w) if stacked(st) else (b, l, w), dt)
                   for _, w, dt, st in _INPROJ_OUTPUTS],
        grid=grid,
        in_specs=[tok(d), pl.BlockSpec((bb, 6, d), lambda i, t: (i, 0, 0)), _const_spec(g1.shape),
                  _const_spec(win.shape), _const_spec(gq.shape), _const_spec(wuq.shape), _const_spec(gkv.shape),
                  _const_spec(wuk.shape), _const_spec(wvt.shape), pos2, pos2, pos, pos]
        + [tok(a.shape[-1]) for a in extra],
        out_specs=[vt_spec if w is None else (stk(w) if stacked(st) else tok(w)) for _, w, _, st in _INPROJ_OUTPUTS],
        compiler_params=_params("parallel", "parallel"),
        name="in_proj",
    )(x, mod, g1, win, gq, wuq, gkv, wuk, wvt, cm, sm, cr, sr, *extra)
    return dict(zip([o[0] for o in _INPROJ_OUTPUTS], res))


def _kvup_kernel(lat_ref, wuk_ref, wvt_ref, kn_ref, vt_ref):
    lat_b = lat_ref[0].astype(BF16)
    kn_ref[0] = _dot(lat_b, wuk_ref[...]).astype(kn_ref.dtype)
    tkk = vt_ref.shape[-1]
    for kk in range(vt_ref.shape[1]):
        vt_ref[0, kk] = _dot_nt(wvt_ref[...], lat_b[kk * tkk:(kk + 1) * tkk]).astype(vt_ref.dtype)


def _kvup(lat, layer, wuk, wvt):
    _, b, lk, r = lat.shape
    nv = MLA_HEADS * MLA_V
    blk = lambda w: pl.BlockSpec((1, lk, w), lambda i: (i, 0, 0))
    return pl.pallas_call(
        _kvup_kernel,
        out_shape=(jax.ShapeDtypeStruct((b, lk, MLA_NQ), BF16),
                   jax.ShapeDtypeStruct((b, lk // ATT_TILE, nv, ATT_TILE), BF16)),
        grid=(b,),
        in_specs=[pl.BlockSpec((None, 1, lk, r), lambda i: (layer, i, 0, 0)), _const_spec(wuk.shape),
                  _const_spec(wvt.shape)],
        out_specs=[blk(MLA_NQ), pl.BlockSpec((1, lk // ATT_TILE, nv, ATT_TILE), lambda i: (i, 0, 0, 0))],
        compiler_params=_params("parallel"),
        name="kv_up",
    )(lat, wuk, wvt)


def _fori_pairs(n, body, init):
    carry = lax.fori_loop(0, n // 3, lambda i, c: body(3 * i + 2, body(3 * i + 1, body(3 * i, c))), init)
    return lax.fori_loop(3 * (n // 3), n, lambda i, c: body(i, c), carry)


def _fold8(reduce, x):
    return reduce(x.reshape(x.shape[0] // 8, 8, x.shape[1]), axis=0)


def _mla_kernel(qn_ref, qr_ref, knp_ref, krp_ref, vtp_ref, knd_ref, krd_ref, vtd_ref, o_ref, s_scr, q_scr, mx_scr, m_scr, l_scr, acc_scr,
                *, tq, tk, off, fused, hp):
    qi = pl.program_id(1)
    nfull = (off + qi * tq) // tk
    nslot = 4 // hp
    wl = hp * tq
    lane = lax.broadcasted_iota(jnp.int32, (1, LANES), 1)
    key_c = lax.broadcasted_iota(jnp.int32, (tq, 1), 0) // CHUNK
    qry_c = (lax.broadcasted_iota(jnp.int32, (1, wl), 1) % tq) // CHUNK
    dmask = key_c <= qry_c
    neg = -1e30
    lane_v = lax.broadcasted_iota(jnp.int32, (1, HEAD_W), 1)
    feat = lax.broadcasted_iota(jnp.int32, (MLA_V, 1), 0)

    def build_queries(g):
        qr = qr_ref[0, :, LANES * g:LANES * (g + 1)]
        for hh in range(4):
            qn = qn_ref[0, :, LANES * (2 * g + hh // 2):LANES * (2 * g + hh // 2 + 1)]
            rows = slice((hh % hp) * tq, (hh % hp + 1) * tq)
            q_scr[g, hh // hp, rows, 0:LANES] = jnp.where((lane >> 6) == hh % 2, qn, jnp.zeros_like(qn))
            q_scr[g, hh // hp, rows, LANES:HEAD_W] = jnp.where((lane >> 5) == hh, qr, jnp.zeros_like(qr))

    def slot_scores(g, kn_ref, kr_ref, ks, u):
        p = 2 * g + (u * hp) // 2
        keys = jnp.concatenate([kn_ref[0, ks, LANES * p:LANES * (p + 1)], kr_ref[0, ks, :]], axis=-1)
        return _dot_nt(keys, q_scr[g, u])

    def vt_rows(g, u):
        return slice(MLA_V * (4 * g + u * hp), MLA_V * (4 * g + (u + 1) * hp))

    def scores_tile(g, j):
        ks = pl.ds(pl.multiple_of(j * tk, tk), tk)
        for u in range(nslot):
            s = slot_scores(g, knp_ref, krp_ref, ks, u)
            s_scr[g, j, u] = s
            mx_scr[g, u] = jnp.maximum(mx_scr[g, u], _fold8(jnp.max, s))

    def scores_own(g):
        own = []
        for u in range(nslot):
            if fused:
                ks_d = pl.ds(pl.multiple_of(nfull * tk, tk), tk)
                s_d = jnp.where(dmask, slot_scores(g, knp_ref, krp_ref, ks_d, u), neg)
                s_scr[g, nfull, u] = s_d
            else:
                s_d = jnp.where(dmask, slot_scores(g, knd_ref, krd_ref, slice(None), u), neg)
                own.append(s_d)
            mx_scr[g, u] = _fold8(jnp.max, s_d)
        return own

    def values_tile(g, j):
        for u in range(nslot):
            p = jnp.exp2(s_scr[g, j, u] - m_scr[g, u, 0:1, :])
            l_scr[g, u] += _fold8(jnp.sum, p)
            acc_scr[g, u] += _dot(vtp_ref[0, j, vt_rows(g, u), :], p.astype(BF16))

    def values_start(g, own):
        for u in range(nslot):
            m_scr[g, u] = jnp.broadcast_to(jnp.max(mx_scr[g, u], axis=0, keepdims=True), (8, wl))
            if fused:
                l_scr[g, u] = jnp.zeros((8, wl), F32)
                acc_scr[g, u] = jnp.zeros((hp * MLA_V, wl), F32)
            else:
                p = jnp.exp2(own[u] - m_scr[g, u, 0:1, :])
                l_scr[g, u] = _fold8(jnp.sum, p)
                acc_scr[g, u] = _dot(vtd_ref[0, 0, vt_rows(g, u), :], p.astype(BF16))

    def emit(g):
        out_g = jnp.zeros((tq, HEAD_W), F32)
        for hh in range(4):
            u, a = hh // hp, hh % hp
            inv = 1.0 / jnp.sum(l_scr[g, u], axis=0, keepdims=True)
            place = jnp.where(feat + MLA_V * hh == lane_v, 1.0, 0.0).astype(BF16)
            placed = _dot_tn((acc_scr[g, u, MLA_V * a:MLA_V * (a + 1), :] * inv).astype(BF16), place)
            out_g = out_g + placed[a * tq:(a + 1) * tq]
        o_ref[0, :, HEAD_W * g:HEAD_W * (g + 1)] = out_g.astype(o_ref.dtype)

    def loop(n, *works):
        def body(j, carry):
            for w in works:
                w(j)
            return carry
        _fori_pairs(n, body, 0)

    build_queries(0)
    build_queries(1)
    own0 = scores_own(0)
    loop(nfull, lambda j: scores_tile(0, j))
    values_start(0, own0)
    own1 = scores_own(1)
    loop(nfull, lambda j: scores_tile(1, j), lambda j: values_tile(0, j))
    if fused:
        values_tile(0, nfull)
    values_start(1, own1)
    emit(0)
    loop(nfull + 1 if fused else nfull, lambda j: values_tile(1, j))
    emit(1)


def _mla(qn, qr, past, diag, off):
    b, l, _ = qn.shape
    lp = past[0].shape[1]
    tq = min(l, ATT_TILE)
    tk = ATT_TILE
    nv = MLA_NV
    fused = off == 0 and tq == tk
    qblk = lambda w: pl.BlockSpec((1, tq, w), lambda i, t: (i, t, 0))
    nmax = (off + l - tq) // tk + (1 if fused else 0)
    hp = 1 if tq % 128 == 0 else 2
    wl = hp * tq
    kern = functools.partial(_mla_kernel, tq=tq, tk=tk, off=off, fused=fused, hp=hp)
    return pl.pallas_call(
        kern,
        out_shape=jax.ShapeDtypeStruct((b, l, nv), BF16),
        grid=(b, l // tq),
        in_specs=[qblk(MLA_NQ), qblk(MLA_HEADS * MLA_ROPE), pl.BlockSpec((1, lp, MLA_NQ), lambda i, t: (i, 0, 0)),
                  pl.BlockSpec((1, lp, LANES), lambda i, t: (i, 0, 0)),
                  pl.BlockSpec((1, lp // tk, nv, tk), lambda i, t: (i, 0, 0, 0)), qblk(MLA_NQ), qblk(LANES),
                  pl.BlockSpec((1, 1, nv, tq), lambda i, t: (i, t, 0, 0))],
        out_specs=qblk(nv),
        scratch_shapes=[pltpu.VMEM((2, nmax, 4 // hp, tk, wl), F32), pltpu.VMEM((2, 4 // hp, wl, HEAD_W), BF16),
                        pltpu.VMEM((2, 4 // hp, 8, wl), F32), pltpu.VMEM((2, 4 // hp, 8, wl), F32),
                        pltpu.VMEM((2, 4 // hp, 8, wl), F32), pltpu.VMEM((2, 4 // hp, hp * MLA_V, wl), F32)],
        compiler_params=_params("parallel", "parallel"),
        name="mla_attn",
    )(qn, qr, *past, *diag)


def _neg_suffix_matrix(n):
    j = lax.broadcasted_iota(jnp.int32, (n, n), 0)
    s = lax.broadcasted_iota(jnp.int32, (n, n), 1)
    return jnp.where(j >= s, -1.0, 0.0).astype(BF16)


def _retention_chunk(rq_ref, rk_ref, rv_ref, rg_ref, gn_ref, dec_ref, qd_ref, kdec_ref, sd_ref, ro_ref, s_ref):
    c = rq_ref.shape[1]
    lane = lax.broadcasted_iota(jnp.int32, (1, HEAD_W), 1)
    krow = lax.broadcasted_iota(jnp.int32, (HEAD_W, 1), 0)
    bd_mask = ((krow & 127) >> 5) == (lane >> 6)
    avg = jnp.where((krow >> 6) == (lane >> 6), 1.0 / RET_DV, 0.0).astype(BF16)
    q = rq_ref[0]
    kf = rk_ref[0]
    k = kf.astype(BF16)
    v = rv_ref[0]
    state = s_ref[0]
    o = _dot(q, state.astype(BF16)) * qd_ref[...]
    qs = jnp.concatenate([jnp.where(((lane & 127) >> 5) == h, q, jnp.zeros_like(q)) for h in range(RET_HEADS)],
                         axis=0)
    pv = _dot((_dot_nt(qs, k) * dec_ref[...]).astype(BF16), v)
    for h in range(RET_HEADS):
        o = o + jnp.where((lane >> 6) == h, pv[h * c:(h + 1) * c], 0.0)
    upd = _dot_tn((kf * kdec_ref[...]).astype(BF16), v)
    s_ref[0] = sd_ref[...] * state + jnp.where(bd_mask, upd, 0.0)
    dlt = o - _group_mean(o, avg)
    ro = dlt * lax.rsqrt(_group_mean(dlt * dlt, avg) + EPS) * gn_ref[...]
    ro_ref[0] = (_silu(rg_ref[0]) * ro).astype(ro_ref.dtype)


def _sb_kernel(q_ref, kp_ref, vp_ref, kd_ref, vd_ref, rq_ref, rk_ref, rv_ref, rg_ref, s0_ref, gn_ref, dec_ref, qd_ref,
               kdec_ref, sd_ref, o_ref, ro_ref, s_ref, kb_ref, vb_ref, q_scr, run_scr, acc_scr, *, tq, tk, off):
    qi = pl.program_id(1)

    @pl.when(qi == 0)
    def _():
        kb_ref[...] = kp_ref[0].astype(BF16)
        vb_ref[...] = vp_ref[0].astype(BF16)
        s_ref[0] = s0_ref[0]

    def retention():
        _retention_chunk(rq_ref, rk_ref, rv_ref, rg_ref, gn_ref, dec_ref, qd_ref, kdec_ref, sd_ref, ro_ref, s_ref)

    nfull = (off + qi * tq) // tk
    lane = lax.broadcasted_iota(jnp.int32, (1, HEAD_W), 1)
    nh = SB_HEADS
    dmask = (lax.broadcasted_iota(jnp.int32, (1, tq), 1)
             < lax.broadcasted_iota(jnp.int32, (nh * tq, 1), 0) % tq)
    t_diag = _neg_suffix_matrix(tq)
    t_full = t_diag if tk == tq else _neg_suffix_matrix(tk)
    q = q_ref[0]

    def exponent(k, mask, tmat):
        zz = _dot_nt(q_scr[...], k)
        sp = jnp.maximum(zz, 0.0) + jnp.log(1.0 + jnp.exp(-jnp.abs(zz)))
        if mask is not None:
            sp = jnp.where(mask, sp, 0.0)
        hi = sp.astype(BF16)
        lo = (sp - hi.astype(F32)).astype(BF16)
        return zz + _dot(hi, tmat) + _dot(lo, tmat), jnp.sum(sp, axis=-1, keepdims=True)

    def carry_on(run):
        run_scr[...] = jnp.broadcast_to(run, (nh * tq, LANES))
        return jnp.max(run)

    def own_weights():
        e, rs = exponent(kd_ref[0].astype(BF16), dmask, t_diag)
        return jnp.where(dmask, jnp.exp(e), 0.0).astype(BF16), rs

    def own_only():
        retention()
        w, rs = own_weights()
        acc_scr[...] = _dot(w, vd_ref[0].astype(BF16))
        return carry_on(-rs)

    def own_and_previous():
        retention()
        ks = pl.ds(pl.multiple_of((nfull - 1) * tk, tk), tk)
        w, rs = own_weights()
        e1, rs1 = exponent(kb_ref[ks, :], None, t_full)
        w1 = jnp.exp(e1 - rs).astype(BF16)
        acc_scr[...] = _dot(w, vd_ref[0].astype(BF16)) + _dot(w1, vb_ref[ks, :])
        return carry_on(-(rs + rs1))

    for h in range(nh):
        q_scr[h * tq:(h + 1) * tq, :] = jnp.where((lane >> 6) == h, q, jnp.zeros_like(q))
    top = lax.cond(nfull >= 1, own_and_previous, own_only)

    def more(state):
        j, top = state
        return jnp.logical_and(j >= 0, top > SB_LOG_ZERO)

    def body(state):
        j, _ = state
        ks = pl.ds(pl.multiple_of(j * tk, tk), tk)
        e, rs = exponent(kb_ref[ks, :], None, t_full)
        run = run_scr[...]
        w = jnp.exp(e + jnp.concatenate([run] * (tk // LANES), -1))
        acc_scr[...] += _dot(w.astype(BF16), vb_ref[ks, :])
        run = run - rs
        run_scr[...] = run
        return j - 1, jnp.max(run)

    lax.while_loop(more, body, (nfull - 2, top))
    out = jnp.zeros((tq, HEAD_W), F32)
    for h in range(nh):
        out = jnp.where((lane >> 6) == h, acc_scr[h * tq:(h + 1) * tq, :], out)
    o_ref[0] = out.astype(o_ref.dtype)


def _sb_ret(sq, past, diag, off, rq, rk, rv, rg, s0, gn):
    b, l, _ = sq.shape
    lp = past[0].shape[2]
    tq = min(l, ATT_TILE)
    dec, qd, kd, sd = _ret_consts(tq)
    dec = dec.reshape(RET_HEADS * tq, tq)
    qblk = pl.BlockSpec((1, tq, HEAD_W), lambda i, t: (i, t, 0))
    dblk = pl.BlockSpec((None, 1, tq, HEAD_W), lambda i, t: (diag[2], i, t, 0))
    kblk = pl.BlockSpec((None, 1, lp, HEAD_W), lambda i, t: (past[2], i, 0, 0))
    sblk = pl.BlockSpec((1, HEAD_W, HEAD_W), lambda i, t: (i, 0, 0))
    kern = functools.partial(_sb_kernel, tq=tq, tk=ATT_TILE, off=off)
    return pl.pallas_call(
        kern,
        out_shape=[jax.ShapeDtypeStruct((b, l, HEAD_W), BF16), jax.ShapeDtypeStruct((b, l, HEAD_W), BF16),
                   jax.ShapeDtypeStruct((b, HEAD_W, HEAD_W), F32)],
        grid=(b, l // tq),
        in_specs=[qblk, kblk, kblk, dblk, dblk, qblk, qblk, qblk, qblk, sblk, _const_spec(gn.shape),
                  _const_spec(dec.shape), _const_spec(qd.shape), _const_spec(kd.shape), _const_spec(sd.shape)],
        out_specs=[qblk, qblk, sblk],
        scratch_shapes=[pltpu.VMEM((lp, HEAD_W), BF16), pltpu.VMEM((lp, HEAD_W), BF16),
                        pltpu.VMEM((SB_HEADS * tq, HEAD_W), BF16), pltpu.VMEM((SB_HEADS * tq, LANES), F32),
                        pltpu.VMEM((SB_HEADS * tq, HEAD_W), F32)],
        compiler_params=_params("parallel", "arbitrary"),
        name="sb_ret",
    )(sq, *past[:2], *diag[:2], rq, rk, rv, rg, s0, gn, dec, qd, kd, sd)


def _group_mean(x, avg):
    hi = x.astype(BF16)
    lo = (x - hi.astype(F32)).astype(BF16)
    return _dot(hi, avg) + _dot(lo, avg)


def _ret_consts(c):
    lg = jnp.log(1.0 - 2.0 ** (-5.0 - jnp.arange(RET_HEADS, dtype=F32)))
    i = jnp.arange(c, dtype=F32)
    rel = i[:, None] - i[None, :]
    dec = jnp.where(rel >= 0, jnp.exp(lg[:, None, None] * jnp.maximum(rel, 0.0)), 0.0)
    v_head = jnp.arange(HEAD_W) // RET_DV
    k_head = (jnp.arange(HEAD_W) % 128) // (RET_DK // 2)
    qd = jnp.exp(lg[None, v_head] * (i[:, None] + 1.0))
    kd = jnp.exp(lg[None, k_head] * (c - 1.0 - i[:, None]))
    sd = jnp.exp(lg * c)[v_head][None, :]
    return dec, qd, kd, sd


def _ffn_kernel(mla_ref, ret_ref, sb_ref, x_ref, mod_ref, prev_ref, wo_ref, g2_ref, wup_ref, cw_ref, cb_ref,
                wdn_ref, gf_ref, o_ref, st_ref, a_scr, carry, *, final_norm):
    bb, tl, d = x_ref.shape
    rows = bb * tl
    t = pl.program_id(1)
    nt = pl.num_programs(1)

    @pl.when(t == 0)
    def _():
        carry[:, 6:8, :] = prev_ref[...]

    mod = mod_ref[...]
    cat = jnp.concatenate([mla_ref[...].reshape(rows, MLA_NV), ret_ref[...].reshape(rows, HEAD_W),
                           sb_ref[...].reshape(rows, HEAD_W)], axis=-1)
    x1 = x_ref[...] + mod[:, 2:3, :] * _dot(cat, wo_ref[...]).reshape(bb, tl, d)
    h = (_rms(x1) * g2_ref[...] * (1.0 + mod[:, 4:5, :]) + mod[:, 3:4, :]).reshape(rows, d).astype(BF16)
    f = jnp.zeros((rows, d), F32)
    for c0, c1 in FF_CHUNKS:
        cs, n = slice(c0, c1), c1 - c0
        a = _dot(h, wup_ref[:, cs]).reshape(bb, tl, n)
        b = _dot(h, wup_ref[:, D_FF + c0:D_FF + c1])
        a_scr[:, 8:, :n] = a
        a_scr[:, 6:8, :n] = carry[:, 6:8, cs]
        cw = cw_ref[...]
        conv = (cb_ref[:, cs] + cw[0:1, cs] * a_scr[:, 6:6 + tl, :n] + cw[1:2, cs] * a_scr[:, 7:7 + tl, :n]
                + cw[2:3, cs] * a)
        carry[:, 6:8, cs] = a_scr[:, tl + 6:tl + 8, :n]
        y = (_silu(conv).reshape(rows, n) * b).astype(BF16)
        f = f + _dot(y, wdn_ref[cs, :])
    x2 = x1 + mod[:, 5:6, :] * f.reshape(bb, tl, d)
    if final_norm:
        x2 = _rms(x2) * gf_ref[...]
    o_ref[...] = x2

    @pl.when(t == nt - 1)
    def _():
        st_ref[...] = carry[:, 6:8, :]


def _ffn(mla, ret, sb, x, mod, prev, wo, g2, wup, cw, cb, wdn, gf, final_norm):
    b, l, d = x.shape
    tl = min(l, ROW_TILE)
    bb = ROW_TILE // tl
    tok = lambda w: pl.BlockSpec((bb, tl, w), lambda i, t: (i, t, 0))
    st = pl.BlockSpec((bb, CONV_W - 1, D_FF), lambda i, t: (i, 0, 0))
    once = lambda a: pl.BlockSpec(a.shape, lambda i, t: (0, 0), pipeline_mode=pl.Buffered(1))
    return pl.pallas_call(
        functools.partial(_ffn_kernel, final_norm=final_norm),
        out_shape=[jax.ShapeDtypeStruct((b, l, d), F32), jax.ShapeDtypeStruct((b, CONV_W - 1, D_FF), F32)],
        grid=(b // bb, l // tl),
        in_specs=[tok(MLA_NV), tok(HEAD_W), tok(HEAD_W), tok(d), pl.BlockSpec((bb, 6, d), lambda i, t: (i, 0, 0)), st,
                  once(wo), _const_spec(g2.shape), once(wup), _const_spec(cw.shape), _const_spec(cb.shape),
                  once(wdn), _const_spec(gf.shape)],
        out_specs=[tok(d), st],
        scratch_shapes=[pltpu.VMEM((bb, tl + 8, max(c1 - c0 for c0, c1 in FF_CHUNKS)), F32),
                        pltpu.VMEM((bb, 8, D_FF), F32)],
        compiler_params=_params("parallel", "arbitrary"),
        name="conv_ffn",
    )(mla, ret, sb, x, mod, prev, wo, g2, wup, cw, cb, wdn, gf)


def _half_split(w, heads):
    lead = w.shape[:-1]
    dim = w.shape[-1] // heads
    w = w.reshape(lead + (heads, 2, dim // 2))
    return jnp.swapaxes(w, -3, -2).reshape(lead + (heads * dim,))


def _permute_win(w):
    half = MLA_ROPE // 2
    kr = w[..., 640:672]
    kr_swapped = jnp.concatenate([kr[..., half:], kr[..., :half]], axis=-1)
    tiled = lambda a: jnp.tile(a, (1, 1, 4))
    return jnp.concatenate([w[..., 0:640], _half_split(w[..., 672:928], RET_HEADS),
                            _half_split(w[..., 928:1184], RET_HEADS), w[..., 1184:2464],
                            tiled(kr), tiled(kr_swapped)], axis=-1)


def _permute_wuq(w):
    half = MLA_ROPE // 2
    w = w.reshape(w.shape[:-1] + (MLA_HEADS, MLA_NOPE + MLA_ROPE))
    flat = lambda a: a.reshape(a.shape[:-2] + (-1,))
    rope = w[..., MLA_NOPE:]
    swapped = jnp.concatenate([rope[..., half:], rope[..., :half]], axis=-1)
    return jnp.concatenate([flat(w[..., :MLA_NOPE]), flat(rope), flat(swapped)], axis=-1)


def _permute_wukv(w):
    w = w.reshape(w.shape[:-1] + (MLA_HEADS, MLA_NOPE + MLA_V))
    flat = lambda a: a.reshape(a.shape[:-2] + (-1,))
    return jnp.concatenate([flat(w[..., :MLA_NOPE]), flat(w[..., MLA_NOPE:])], axis=-1)


def _ret_state_in(st):
    b = st.shape[0]
    st = st.reshape(b, RET_HEADS, 2, RET_DK // 2, RET_DV).transpose(0, 2, 1, 3, 4)
    eye = jnp.eye(RET_HEADS, dtype=st.dtype)
    full = st[:, :, :, :, None, :] * eye[None, None, :, None, :, None]
    return full.reshape(b, HEAD_W, HEAD_W)


def _ret_state_out(s):
    b = s.shape[0]
    s = s.reshape(b, 2, RET_HEADS, RET_DK // 2, RET_HEADS, RET_DV)
    blocks = [s[:, :, h, :, h, :].reshape(b, RET_DK, RET_DV) for h in range(RET_HEADS)]
    return jnp.stack(blocks, axis=1)


def _rope_angles(pos, dim):
    inv = ROPE_BASE ** (-jnp.arange(0, dim, 2, dtype=F32) / dim)
    return pos.astype(F32)[:, None] * inv[None, :]


def _rope_tables(pos, dim, reps):
    ang = _rope_angles(pos, dim)
    return jnp.tile(jnp.cos(ang), (1, reps)), jnp.tile(jnp.sin(ang), (1, reps))


def _rope_tables_swapped(pos, dim, reps):
    ang = _rope_angles(pos, dim)
    c, s = jnp.cos(ang), jnp.sin(ang)
    return jnp.tile(jnp.concatenate([c, c], axis=1), (1, reps)), jnp.tile(jnp.concatenate([-s, s], axis=1), (1, reps))


def _trunk(x, mods, pos0, cache, w):
    b, l, _ = x.shape
    depth = len(mods)
    pos = pos0 + jnp.arange(l)
    cm, sm = _rope_tables_swapped(pos, MLA_ROPE, MLA_HEADS)
    cr, sr = _rope_tables(pos, RET_DK, RET_HEADS)
    if cache is None:
        off = 0
    else:
        off = cache["lat"].shape[2]
        krt_cache = jnp.tile(cache["kr"], (1, 1, 1, 4)).astype(BF16)
        sb_cache = (cache["sk"].reshape(depth, b, off, HEAD_W), cache["sv"].reshape(depth, b, off, HEAD_W))
    earlier = []
    small = [[], []]
    for layer in range(depth):
        mod = mods[layer]
        last = layer == depth - 1
        p = _inproj(x, mod, w["g_norm1"][layer], w["w_in"][layer], w["g_q_norm"][layer], w["w_uq"][layer],
                    w["g_kv_norm"][layer], w["w_uk"][layer], w["w_vt"][layer], cm, sm, cr, sr,
                    earlier if last else None)
        if last:
            sb_new = (p["sk"], p["sv"], layer)
        else:
            earlier.append((p["lat"], p["kr"], p["sk"], p["sv"]))
            sb_new = (p["sk"][None], p["sv"][None], 0)
        if cache is None:
            mla_past, sb_past = (p["kn"], p["krt"], p["vt"]), sb_new
            s0 = jnp.zeros((b, HEAD_W, HEAD_W), F32)
            prev = jnp.zeros((b, CONV_W - 1, D_FF), F32)
        else:
            kn_c, vt_c = _kvup(cache["lat"], layer, w["w_uk"][layer], w["w_vt"][layer])
            mla_past = (kn_c, krt_cache[layer], vt_c)
            sb_past = sb_cache + (layer,)
            s0 = _ret_state_in(cache["S"][layer])
            prev = cache["conv"][layer]
        mla = _mla(p["qn"], p["qr"], mla_past, (p["kn"], p["krt"], p["vt"]), off)
        sb, ret, s_new = _sb_ret(p["sq"], sb_past, sb_new, off, p["rq"], p["rk"], p["rv"], p["rg"], s0,
                                 w["g_ret_norm"][layer])
        x, conv_state = _ffn(mla, ret, sb, x, mod, prev, w["w_o"][layer], w["g_norm2"][layer], w["w_up"][layer],
                             w["conv_w"][layer], w["conv_b"][layer], w["w_down"][layer], w["g_final"],
                             layer == depth - 1)
        small[0].append(_ret_state_out(s_new))
        small[1].append(conv_state)
    lat, kr, sk, sv = p["lat"], p["kr"], p["sk"], p["sv"]
    return x, [lat, kr, sk.reshape(depth, b, l, SB_HEADS, SB_DK), sv.reshape(depth, b, l, SB_HEADS, SB_DV),
               jnp.stack(small[0], axis=0), jnp.stack(small[1], axis=0)]


def kernel(x_prompt, x_sample, c_prompt, c_sample, cache_mla_latent, cache_mla_krope, cache_sb_k, cache_sb_v, state_ret, state_ffn_conv, w_in, g_q_norm, w_uq, g_kv_norm, w_ukv, g_ret_norm, w_o, w_up, conv_w, conv_b, w_down, g_norm1, g_norm2, w_ada, b_ada, g_final):
    depth = w_in.shape[0]
    bp = x_prompt.shape[0]
    row = lambda g: g.reshape(g.shape[0], 1, g.shape[-1])
    w = dict(
        w_in=_permute_win(w_in).astype(BF16),
        w_uq=_permute_wuq(w_uq).astype(BF16),
        w_uk=_permute_wukv(w_ukv)[..., :MLA_HEADS * MLA_NOPE].astype(BF16),
        w_vt=jnp.swapaxes(_permute_wukv(w_ukv)[..., MLA_HEADS * MLA_NOPE:], -1, -2).astype(BF16),
        w_o=w_o.astype(BF16), w_up=w_up.astype(BF16), w_down=w_down.astype(BF16),
        g_q_norm=row(g_q_norm), g_kv_norm=row(g_kv_norm), g_ret_norm=row(g_ret_norm),
        g_norm1=row(g_norm1), g_norm2=row(g_norm2), conv_w=conv_w, conv_b=row(conv_b),
        g_final=g_final.reshape(1, -1))
    mod = _ada(jnp.concatenate([c_prompt, c_sample], axis=0), w_ada, b_ada)
    mod = mod.reshape(depth, mod.shape[1], 6, D_MODEL)
    y_p, st_p = _trunk(x_prompt, [mod[l, :bp] for l in range(depth)], 0, None, w)
    cache = dict(lat=cache_mla_latent, kr=cache_mla_krope, sk=cache_sb_k, sv=cache_sb_v, S=state_ret,
                 conv=state_ffn_conv)
    y_s, st_s = _trunk(x_sample, [mod[l, bp:] for l in range(depth)], cache_mla_latent.shape[2], cache, w)
    return (y_p, y_s, *st_p, *st_s)
```
